```python
import math
import jax
import jax.numpy as jnp
from jax import lax
import numpy as np

D_MODEL = 1024
BATCH = 16
SEQ = 256
DEPTH = 2
DEC_BATCH = 8
DEC_SEQ = 2048
PAST_LEN = 256

GRID_W = 64
NA_HEADS = 16
HEAD_DIM = 64
NA_WIDTH = NA_HEADS * HEAD_DIM
WIN_R = 8
WIN_C = 16
SSD_INNER = 2 * D_MODEL
SSD_HEADDIM = 64
SSD_HEADS = SSD_INNER // SSD_HEADDIM
SSD_GROUPS = 4
D_STATE = 128
SSD_CONV = 4
CHUNK = 128
CONV_DIM = SSD_INNER + 2 * SSD_GROUPS * D_STATE
D_FF = 2816
FFN_CONV = 3
Q_BLOCK = 128
SPLIT_SIZES = (NA_WIDTH, NA_WIDTH, NA_WIDTH, SSD_INNER, CONV_DIM, 2 * SSD_HEADS, D_MODEL, D_MODEL)
P_IN = sum(SPLIT_SIZES)
EPS = 1e-6

kernel_name = 'hybrid_na_ssd_prefix_dit_step'


def _split(x, sizes):
    out, start = [], 0
    for s in sizes:
        out.append(x[..., start:start + s])
        start += s
    return out


def _rmsnorm(x, g):
    xf = x.astype(jnp.float32)
    y = xf * lax.rsqrt(jnp.mean(xf * xf, axis=-1, keepdims=True) + EPS)
    return y.astype(x.dtype) * g


def _dwconv(x, w, b):
    k, ch = w.shape
    left = k // 2
    y = lax.conv_general_dilated(x, w[:, None, :], window_strides=(1,), padding=[(left, k - 1 - left)],
                                 dimension_numbers=('NWC', 'WIO', 'NWC'), feature_group_count=ch)
    return y + b


def _adaln(cond, lp):
    m = jax.nn.silu(cond) @ lp['w_ada'] + lp['b_ada']
    return _split(m, (D_MODEL,) * 6)


def _mixer_inputs(x, mod, lp):
    bsz, t, _ = x.shape
    h = _rmsnorm(x, lp['norm1_g']) * (1 + mod[1]) + mod[0]
    q, k, v, z, xbc, dt_raw, ga, gb = _split(h @ lp['w_in'], SPLIT_SIZES)
    q = _rmsnorm(q.reshape(bsz, t, NA_HEADS, HEAD_DIM), lp['q_norm_g'])
    k = _rmsnorm(k.reshape(bsz, t, NA_HEADS, HEAD_DIM), lp['k_norm_g'])
    v = v.reshape(bsz, t, NA_HEADS, HEAD_DIM)
    return q, k, v, z, xbc, dt_raw, ga, gb


def _context_attention(q, k, v):
    bsz, length, _, _ = q.shape
    nb = length // Q_BLOCK
    qb = jnp.moveaxis(q.reshape(bsz, nb, Q_BLOCK, NA_HEADS, HEAD_DIM), 1, 0)
    scale = HEAD_DIM ** -0.5

    def block(qi):
        s = jnp.einsum('bqhd,bkhd->bhqk', qi, k).astype(jnp.float32) * scale
        p = jax.nn.softmax(s, axis=-1).astype(v.dtype)
        return jnp.einsum('bhqk,bkhd->bqhd', p, v)

    o = lax.map(block, qb)
    return jnp.moveaxis(o, 0, 1).reshape(bsz, length, NA_WIDTH)


def _neighbourhood_attention(q, k, v, k_ctx, v_ctx, rpb):
    bsz, t, _, _ = q.shape
    rows = t // GRID_W
    kr = min(WIN_R, rows)
    qg = q.reshape(bsz, rows, GRID_W, NA_HEADS, HEAD_DIM)
    kg = k.reshape(bsz, rows, GRID_W, NA_HEADS, HEAD_DIM)
    vg = v.reshape(bsz, rows, GRID_W, NA_HEADS, HEAD_DIM)
    cols = jnp.arange(GRID_W)
    col_start = jnp.clip(cols - WIN_C // 2, 0, GRID_W - WIN_C)
    col_mask = (cols[None, :] >= col_start[:, None]) & (cols[None, :] < col_start[:, None] + WIN_C)
    col_idx = jnp.clip(cols[None, :] - cols[:, None] + WIN_C - 1, 0, 2 * WIN_C - 2)
    col_bias = rpb[:, :, col_idx].astype(jnp.float32)
    scale = HEAD_DIM ** -0.5

    def row_block(r):
        rs = jnp.clip(r - kr // 2, 0, rows - kr)
        q_r = lax.dynamic_index_in_dim(qg, r, axis=1, keepdims=False)
        k_r = lax.dynamic_slice_in_dim(kg, rs, kr, axis=1)
        v_r = lax.dynamic_slice_in_dim(vg, rs, kr, axis=1)
        s_win = jnp.einsum('bqhd,bjkhd->bhqjk', q_r, k_r).astype(jnp.float32) * scale
        row_idx = rs + jnp.arange(kr) - r + WIN_R - 1
        bias = jnp.transpose(col_bias[:, row_idx], (0, 2, 1, 3))
        s_win = jnp.where(col_mask[:, None, :], s_win + bias, -jnp.inf)
        s_ctx = jnp.einsum('bqhd,bkhd->bhqk', q_r, k_ctx).astype(jnp.float32) * scale
        s = jnp.concatenate([s_win.reshape(bsz, NA_HEADS, GRID_W, kr * GRID_W), s_ctx], axis=-1)
        p = jax.nn.softmax(s, axis=-1).astype(v.dtype)
        p_win = p[..., :kr * GRID_W].reshape(bsz, NA_HEADS, GRID_W, kr, GRID_W)
        p_ctx = p[..., kr * GRID_W:]
        return jnp.einsum('bhqjk,bjkhd->bqhd', p_win, v_r) + jnp.einsum('bhqk,bkhd->bqhd', p_ctx, v_ctx)

    o = lax.map(row_block, jnp.arange(rows))
    return jnp.moveaxis(o, 0, 1).reshape(bsz, t, NA_WIDTH)


def _ssd_scan(x, dt, a, bm, cm, h0):
    bsz, t, nh, hp = x.shape
    nc = t // CHUNK
    hg = nh // SSD_GROUPS
    f32 = jnp.float32
    xc = x.astype(f32).reshape(bsz, nc, CHUNK, SSD_GROUPS, hg, hp)
    dtc = dt.reshape(bsz, nc, CHUNK, SSD_GROUPS, hg)
    bc = bm.astype(f32).reshape(bsz, nc, CHUNK, SSD_GROUPS, D_STATE)
    cc = cm.astype(f32).reshape(bsz, nc, CHUNK, SSD_GROUPS, D_STATE)
    acs = jnp.cumsum(dtc * a.reshape(SSD_GROUPS, hg), axis=2)
    causal = jnp.tril(jnp.ones((CHUNK, CHUNK), dtype=bool))[None, None, :, :, None, None]
    seg = acs[:, :, :, None] - acs[:, :, None, :]
    lmat = jnp.exp(jnp.where(causal, seg, -jnp.inf))
    cb = jnp.einsum('bcign,bcjgn->bcijg', cc, bc)
    w = cb[..., None] * lmat * dtc[:, :, None]
    y_diag = jnp.einsum('bcijgh,bcjghp->bcighp', w, xc)
    xw = xc * (jnp.exp(acs[:, :, -1:] - acs) * dtc)[..., None]
    states = jnp.einsum('bcjgn,bcjghp->bcghpn', bc, xw)
    chunk_decay = jnp.exp(acs[:, :, -1])

    def step(h, inp):
        st, dec = inp
        return h * dec[..., None, None] + st, h

    h_init = h0.astype(f32).reshape(bsz, SSD_GROUPS, hg, hp, D_STATE)
    h_final, h_prev = lax.scan(step, h_init, (jnp.moveaxis(states, 1, 0), jnp.moveaxis(chunk_decay, 1, 0)))
    h_prev = jnp.moveaxis(h_prev, 0, 1)
    y_off = jnp.einsum('bcign,bcghpn->bcighp', cc, h_prev) * jnp.exp(acs)[..., None]
    y = (y_diag + y_off).reshape(bsz, t, nh, hp).astype(x.dtype)
    return y, h_final.reshape(bsz, nh, hp, D_STATE).astype(x.dtype)


def _ssd_branch(z, xbc, dt_raw, lp, h0_f, h0_b):
    bsz, t, _ = z.shape
    xbc = jax.nn.silu(_dwconv(xbc, lp['ssd_conv_w'], lp['ssd_conv_b']))
    xs, bm, cm = _split(xbc, (SSD_INNER, SSD_GROUPS * D_STATE, SSD_GROUPS * D_STATE))
    xs = xs.reshape(bsz, t, SSD_HEADS, SSD_HEADDIM)
    bm = bm.reshape(bsz, t, SSD_GROUPS, D_STATE)
    cm = cm.reshape(bsz, t, SSD_GROUPS, D_STATE)
    a = -jnp.exp(lp['a_log'].astype(jnp.float32))
    dt = jax.nn.softplus(dt_raw.astype(jnp.float32).reshape(bsz, t, 2, SSD_HEADS) + lp['dt_bias'].astype(jnp.float32))
    y_f, h_f = _ssd_scan(xs, dt[:, :, 0], a[0], bm, cm, h0_f)
    y_b, h_b = _ssd_scan(xs[:, ::-1], dt[:, ::-1, 1], a[1], bm[:, ::-1], cm[:, ::-1], h0_b)
    y = y_f + y_b[:, ::-1] + lp['d_skip'][:, None] * xs
    y = _rmsnorm(y.reshape(bsz, t, SSD_INNER) * jax.nn.silu(z), lp['ssd_norm_g'])
    return y, h_f, h_b


def _finish_layer(x, mod, lp, na, ssd, ga, gb):
    mix = jax.nn.sigmoid(ga) * (na @ lp['w_na_out']) + jax.nn.sigmoid(gb) * (ssd @ lp['w_ssd_out'])
    x = x + mod[2] * (mix @ lp['w_o'])
    h2 = _rmsnorm(x, lp['norm2_g']) * (1 + mod[4]) + mod[3]
    u = _dwconv(h2 @ lp['w_up'], lp['ffn_conv_w'], lp['ffn_conv_b'])
    val, gate = _split(u, (D_FF, D_FF))
    return x + mod[5] * ((jax.nn.silu(gate) * val) @ lp['w_down'])


def _context_layer(x, c_ctx, lp):
    mod = _adaln(c_ctx[None, None, :], lp)
    q, k, v, z, xbc, dt_raw, ga, gb = _mixer_inputs(x, mod, lp)
    na = _context_attention(q, k, v)
    h0 = jnp.zeros((x.shape[0], SSD_HEADS, SSD_HEADDIM, D_STATE), x.dtype)
    ssd, h_f, h_b = _ssd_branch(z, xbc, dt_raw, lp, h0, h0)
    return _finish_layer(x, mod, lp, na, ssd, ga, gb), k, v, h_f, h_b


def _latent_layer(x, c, k_ctx, v_ctx, h0_f, h0_b, lp):
    mod = _adaln(c[:, None, :], lp)
    q, k, v, z, xbc, dt_raw, ga, gb = _mixer_inputs(x, mod, lp)
    na = _neighbourhood_attention(q, k, v, k_ctx, v_ctx, lp['rpb'])
    ssd, _, _ = _ssd_branch(z, xbc, dt_raw, lp, h0_f, h0_b)
    return _finish_layer(x, mod, lp, na, ssd, ga, gb)


def setup_inputs(seed: int = 0) -> dict:
    key = jax.random.key(seed)
    ks = jax.random.split(key, 32)
    f32 = jnp.float32

    def nrm(k, shape, s):
        return jax.random.normal(k, shape, f32) * s

    L = DEPTH
    dt0 = jnp.exp(jax.random.uniform(ks[18], (L, 2, SSD_HEADS), f32, math.log(1e-3), math.log(1e-1)))
    return {
        'x_prompt': nrm(ks[0], (BATCH, SEQ, D_MODEL), 1.0),
        'x_sample': nrm(ks[1], (DEC_BATCH, DEC_SEQ, D_MODEL), 1.0),
        'c': nrm(ks[2], (DEC_BATCH, D_MODEL), 1.0),
        'cache_k': nrm(ks[3], (DEC_BATCH, L, PAST_LEN, NA_HEADS, HEAD_DIM), 1.0),
        'cache_v': nrm(ks[4], (DEC_BATCH, L, PAST_LEN, NA_HEADS, HEAD_DIM), 0.5),
        'state_ssd_fwd': nrm(ks[5], (DEC_BATCH, L, SSD_HEADS, SSD_HEADDIM, D_STATE), 0.1),
        'state_ssd_bwd': nrm(ks[6], (DEC_BATCH, L, SSD_HEADS, SSD_HEADDIM, D_STATE), 0.1),
        'c_ctx': nrm(ks[7], (D_MODEL,), 1.0),
        'w_ada': nrm(ks[8], (L, D_MODEL, 6 * D_MODEL), 0.5 * D_MODEL ** -0.5),
        'b_ada': nrm(ks[9], (L, 6 * D_MODEL), 0.02),
        'norm1_g': 1.0 + nrm(ks[10], (L, D_MODEL), 0.01),
        'w_in': nrm(ks[11], (L, D_MODEL, P_IN), D_MODEL ** -0.5),
        'q_norm_g': 1.0 + nrm(ks[12], (L, HEAD_DIM), 0.01),
        'k_norm_g': 1.0 + nrm(ks[13], (L, HEAD_DIM), 0.01),
        'rpb': nrm(ks[14], (L, NA_HEADS, 2 * WIN_R - 1, 2 * WIN_C - 1), 0.1),
        'ssd_conv_w': nrm(ks[15], (L, SSD_CONV, CONV_DIM), SSD_CONV ** -0.5),
        'ssd_conv_b': nrm(ks[16], (L, CONV_DIM), 0.02),
        'a_log': jnp.log(jax.random.uniform(ks[17], (L, 2, SSD_HEADS), f32, 1.0, 16.0)),
        'dt_bias': dt0 + jnp.log(-jnp.expm1(-dt0)),
        'd_skip': 1.0 + nrm(ks[19], (L, SSD_HEADS), 0.1),
        'ssd_norm_g': 1.0 + nrm(ks[20], (L, SSD_INNER), 0.01),
        'w_na_out': nrm(ks[21], (L, NA_WIDTH, D_MODEL), NA_WIDTH ** -0.5),
        'w_ssd_out': nrm(ks[22], (L, SSD_INNER, D_MODEL), SSD_INNER ** -0.5),
        'w_o': nrm(ks[23], (L, D_MODEL, D_MODEL), D_MODEL ** -0.5),
        'norm2_g': 1.0 + nrm(ks[24], (L, D_MODEL), 0.01),
        'w_up': nrm(ks[25], (L, D_MODEL, 2 * D_FF), D_MODEL ** -0.5),
        'ffn_conv_w': nrm(ks[26], (L, FFN_CONV, 2 * D_FF), FFN_CONV ** -0.5),
        'ffn_conv_b': nrm(ks[27], (L, 2 * D_FF), 0.02),
        'w_down': nrm(ks[28], (L, D_FF, D_MODEL), D_FF ** -0.5),
    }


def reference(x_prompt, x_sample, c, cache_k, cache_v, state_ssd_fwd, state_ssd_bwd, c_ctx,
              w_ada, b_ada, norm1_g, w_in, q_norm_g, k_norm_g, rpb, ssd_conv_w, ssd_conv_b,
              a_log, dt_bias, d_skip, ssd_norm_g, w_na_out, w_ssd_out, w_o, norm2_g,
              w_up, ffn_conv_w, ffn_conv_b, w_down):
    xp = x_prompt
    xs = x_sample
    new_k, new_v, new_hf, new_hb = [], [], [], []
    for l in range(DEPTH):
        lp = {
            'w_ada': w_ada[l], 'b_ada': b_ada[l], 'norm1_g': norm1_g[l], 'w_in': w_in[l],
            'q_norm_g': q_norm_g[l], 'k_norm_g': k_norm_g[l], 'rpb': rpb[l],
            'ssd_conv_w': ssd_conv_w[l], 'ssd_conv_b': ssd_conv_b[l], 'a_log': a_log[l],
            'dt_bias': dt_bias[l], 'd_skip': d_skip[l], 'ssd_norm_g': ssd_norm_g[l],
            'w_na_out': w_na_out[l], 'w_ssd_out': w_ssd_out[l], 'w_o': w_o[l], 'norm2_g': norm2_g[l],
            'w_up': w_up[l], 'ffn_conv_w': ffn_conv_w[l], 'ffn_conv_b': ffn_conv_b[l], 'w_down': w_down[l],
        }
        xp, k_l, v_l, hf_l, hb_l = _context_layer(xp, c_ctx, lp)
        new_k.append(k_l)
        new_v.append(v_l)
        new_hf.append(hf_l)
        new_hb.append(hb_l)
        xs = _latent_layer(xs, c, cache_k[:, l], cache_v[:, l], state_ssd_fwd[:, l], state_ssd_bwd[:, l], lp)
    new_cache_k = jnp.stack(new_k, axis=1)
    new_cache_v = jnp.stack(new_v, axis=1)
    new_state_ssd_fwd = jnp.stack(new_hf, axis=1)
    new_state_ssd_bwd = jnp.stack(new_hb, axis=1)
    return (xp, xs, new_cache_k, new_cache_v, new_state_ssd_fwd, new_state_ssd_bwd)
```

```python
import functools

import jax
import jax.numpy as jnp
from jax import lax
from jax.experimental import pallas as pl
from jax.experimental.pallas import tpu as pltpu

D_MODEL = 1024
BATCH = 16
SEQ = 256
DEPTH = 2
DEC_BATCH = 8
DEC_SEQ = 2048
PAST_LEN = 256
GRID_W = 64
NA_HEADS = 16
HEAD_DIM = 64
NA_WIDTH = NA_HEADS * HEAD_DIM
WIN_R = 8
WIN_C = 16
SSD_INNER = 2 * D_MODEL
SSD_HEADDIM = 64
SSD_HEADS = SSD_INNER // SSD_HEADDIM
SSD_GROUPS = 4
D_STATE = 128
SSD_CONV = 4
CHUNK = 128
CONV_DIM = SSD_INNER + 2 * SSD_GROUPS * D_STATE
D_FF = 2816
FFN_CONV = 3
EPS = 1e-6

N_CTX = BATCH * SEQ
N_LAT = DEC_BATCH * DEC_SEQ
N_TOK = N_CTX + N_LAT
HEADS_PER_GROUP = SSD_HEADS // SSD_GROUPS
GROUP_W = HEADS_PER_GROUP * SSD_HEADDIM

LANE = 128
SUBLANE = 8
BF16_ROWS = 16
MXU_N = 256

P_Q, P_K, P_V, P_Z, P_XBC = 0, 1024, 2048, 3072, 5120
P_GA, P_GB, P_W = 8192, 9216, 10240
DT_W = SSD_GROUPS * LANE
IN_TILE_N = MXU_N
N_IN_TILES = (P_W + DT_W) // IN_TILE_N
N_QK_TILES = (P_V - P_Q) // IN_TILE_N
N_QKV_TILES = (P_Z - P_Q) // IN_TILE_N
N_P_TILES = P_W // IN_TILE_N

TM_IN = 1024
TM_MERGE = 512
TM_FFN = 1024
FF_TILE = 256
FFN_HALO = BF16_ROWS
MASK_NEG = -1e30

F32 = jnp.float32
BF16 = jnp.bfloat16
NT = (((1,), (1,)), ((), ()))

VMEM_LIMIT = 56 * 1024 * 1024


def _sigmoid(x):
    return jax.nn.sigmoid(x)


def _dot(a, b):
    return jnp.dot(a, b, preferred_element_type=F32)


def _dot_nt(a, b):
    return lax.dot_general(a, b, NT, preferred_element_type=F32)


def _split_dot(a, b, terms):
    acc = None
    r = a
    for t in range(terms):
        p = r.astype(BF16)
        d = _dot(p, b)
        acc = d if acc is None else acc + d
        if t + 1 < terms:
            r = r - p.astype(F32)
    return acc


def _mod_row(i, tm):
    start = i * tm
    return jnp.where(start < N_CTX, 0, 1 + (start - N_CTX) // DEC_SEQ)


def _mod_kernel(c_ref, w_ref, b_ref, o_ref):
    c = c_ref[...]
    s = (c * _sigmoid(c)).astype(BF16)
    o_ref[...] = _dot(s, w_ref[...].astype(BF16)) + b_ref[...]


def _modulation(cond, w_ada, b_ada):
    tn = 1536
    return pl.pallas_call(
        _mod_kernel,
        out_shape=jax.ShapeDtypeStruct((DEPTH, 16, 6 * D_MODEL), F32),
        grid=(DEPTH, 6 * D_MODEL // tn),
        in_specs=[
            pl.BlockSpec((16, D_MODEL), lambda l, j: (0, 0)),
            pl.BlockSpec((None, D_MODEL, tn), lambda l, j: (l, 0, j)),
            pl.BlockSpec((None, 1, tn), lambda l, j: (l, 0, j)),
        ],
        out_specs=pl.BlockSpec((None, 16, tn), lambda l, j: (l, 0, j)),
        compiler_params=pltpu.CompilerParams(
            dimension_semantics=("parallel", "parallel"), vmem_limit_bytes=VMEM_LIMIT),
        name="adaln_mod",
    )(cond, w_ada, b_ada)


def _inproj_kernel(x_ref, mod_ref, g_ref, w_ref, qkg_ref, bd_ref,
                   p_ref, kv_ref, dt_ref, h_ref, *, n_ctx_tiles):
    i = pl.program_id(0)
    j = pl.program_id(1)

    @pl.when(j == 0)
    def _():
        x = x_ref[...]
        ms = jnp.mean(x * x, axis=-1, keepdims=True)
        y = x * lax.rsqrt(ms + EPS) * g_ref[...]
        shift = mod_ref[:, 0:D_MODEL]
        scale = mod_ref[:, D_MODEL:2 * D_MODEL]
        h_ref[...] = (y * (1.0 + scale) + shift).astype(BF16)

    acc = _dot(h_ref[...], w_ref[...])
    is_ctx = i < n_ctx_tiles

    @pl.when(j < N_QK_TILES)
    def _():
        ss = _split_dot(acc * acc, bd_ref[...], 2)
        yn = acc * lax.rsqrt(ss * (1.0 / HEAD_DIM) + EPS) * qkg_ref[...]
        p_ref[...] = yn.astype(BF16)

        @pl.when(jnp.logical_and(j >= N_QK_TILES // 2, is_ctx))
        def _():
            kv_ref[...] = yn

    @pl.when(jnp.logical_and(j >= N_QK_TILES, j < N_P_TILES))
    def _():
        p_ref[...] = acc.astype(BF16)

        @pl.when(jnp.logical_and(j < N_QKV_TILES, is_ctx))
        def _():
            kv_ref[...] = acc

    @pl.when(j >= N_P_TILES)
    def _():
        dt_ref[...] = acc


def _in_projection(x, mod, g1, w, qkg, bd):
    n_i = N_TOK // TM_IN
    n_ctx_tiles = N_CTX // TM_IN
    kv0 = N_QK_TILES // 2
    n_kv = N_QKV_TILES - kv0

    def kv_map(i, j):
        ii = jnp.minimum(i, n_ctx_tiles - 1)
        jj = jnp.where(i < n_ctx_tiles, jnp.clip(j - kv0, 0, n_kv - 1), n_kv - 1)
        return (ii, jj)

    return pl.pallas_call(
        functools.partial(_inproj_kernel, n_ctx_tiles=n_ctx_tiles),
        out_shape=(
            jax.ShapeDtypeStruct((N_TOK, P_W), BF16),
            jax.ShapeDtypeStruct((N_CTX, 2 * NA_WIDTH), F32),
            jax.ShapeDtypeStruct((N_TOK, DT_W), F32),
        ),
        grid=(n_i, N_IN_TILES),
        in_specs=[
            pl.BlockSpec((TM_IN, D_MODEL), lambda i, j: (i, 0)),
            pl.BlockSpec((None, 1, 6 * D_MODEL), lambda i, j: (_mod_row(i, TM_IN), 0, 0)),
            pl.BlockSpec((1, D_MODEL), lambda i, j: (0, 0)),
            pl.BlockSpec((D_MODEL, IN_TILE_N), lambda i, j: (0, j)),
            pl.BlockSpec((1, IN_TILE_N), lambda i, j: (0, jnp.minimum(j, N_QK_TILES - 1))),
            pl.BlockSpec((IN_TILE_N, IN_TILE_N), lambda i, j: (0, 0)),
        ],
        out_specs=(
            pl.BlockSpec((TM_IN, IN_TILE_N), lambda i, j: (i, jnp.minimum(j, N_P_TILES - 1))),
            pl.BlockSpec((TM_IN, IN_TILE_N), kv_map),
            pl.BlockSpec((TM_IN, IN_TILE_N), lambda i, j: (i, jnp.maximum(j - N_P_TILES, 0))),
        ),
        scratch_shapes=[pltpu.VMEM((TM_IN, D_MODEL), BF16)],
        compiler_params=pltpu.CompilerParams(
            dimension_semantics=("arbitrary", "arbitrary"), vmem_limit_bytes=VMEM_LIMIT),
        name="in_proj",
    )(x, mod, g1, w, qkg, bd)


def _head_masks():
    lane = lax.broadcasted_iota(jnp.int32, (1, LANE), 1)
    return lane < HEAD_DIM


def _ctx_attn_kernel(q_ref, k_ref, v_ref, o_ref):
    q = q_ref[...]
    k = k_ref[...]
    v = v_ref[...]
    lo = _head_masks()
    outs = []
    for hh in range(2):
        m = lo if hh == 0 else jnp.logical_not(lo)
        qm = jnp.where(m, q, jnp.zeros_like(q))
        s = _dot_nt(qm, k)
        mx = jnp.max(s, axis=-1, keepdims=True)
        p = jnp.exp(s - mx)
        l = jnp.sum(p, axis=-1, keepdims=True)
        outs.append(_dot(p.astype(BF16), v) / l)
    o_ref[...] = jnp.where(lo, outs[0], outs[1]).astype(BF16)


def _context_attention(p):
    hp = NA_HEADS // 2
    blk = (SEQ, LANE)
    return pl.pallas_call(
        _ctx_attn_kernel,
        out_shape=jax.ShapeDtypeStruct((N_CTX, NA_WIDTH), BF16),
        grid=(BATCH, hp),
        in_specs=[
            pl.BlockSpec(blk, lambda b, h: (b, P_Q // LANE + h)),
            pl.BlockSpec(blk, lambda b, h: (b, P_K // LANE + h)),
            pl.BlockSpec(blk, lambda b, h: (b, P_V // LANE + h)),
        ],
        out_specs=pl.BlockSpec(blk, lambda b, h: (b, h)),
        compiler_params=pltpu.CompilerParams(
            dimension_semantics=("parallel", "parallel"), vmem_limit_bytes=VMEM_LIMIT),
        name="ctx_attn",
    )(p, p, p)


def _na_kernel(q_ref, k_ref, v_ref, kc_ref, vc_ref, bias_ref, o_ref):
    rows = DEC_SEQ // GRID_W
    kwin = WIN_R * GRID_W
    lo = _head_masks()
    kc = kc_ref[...].astype(BF16)
    vc = vc_ref[...].astype(BF16)

    def body(r, carry):
        rs = jnp.clip(r - WIN_R // 2, 0, rows - WIN_R)
        d0 = rs - r + WIN_R - 1
        q0 = pl.multiple_of(r * GRID_W, GRID_W)
        k0 = pl.multiple_of(rs * GRID_W, GRID_W)
        qr = q_ref[pl.ds(q0, GRID_W), :]
        kw = k_ref[pl.ds(k0, kwin), :]
        vw = v_ref[pl.ds(k0, kwin), :]
        outs = []
        for hh in range(2):
            m = lo if hh == 0 else jnp.logical_not(lo)
            qm = jnp.where(m, qr, jnp.zeros_like(qr))
            sw = _dot_nt(qm, kw) + bias_ref[hh, d0]
            sc = _dot_nt(qm, kc)
            mx = jnp.maximum(jnp.max(sw, axis=-1, keepdims=True),
                             jnp.max(sc, axis=-1, keepdims=True))
            pw = jnp.exp(sw - mx)
            pc = jnp.exp(sc - mx)
            l = jnp.sum(pw, axis=-1, keepdims=True) + jnp.sum(pc, axis=-1, keepdims=True)
            o = _dot(pw.astype(BF16), vw) + _dot(pc.astype(BF16), vc)
            outs.append(o / l)
        o_ref[pl.ds(q0, GRID_W), :] = jnp.where(lo, outs[0], outs[1]).astype(BF16)
        return carry

    lax.fori_loop(0, rows, body, 0)


def _neighbourhood_attention(p, cache_k, cache_v, bias, layer):
    hp = NA_HEADS // 2
    row0 = N_CTX // DEC_SEQ
    blk = (DEC_SEQ, LANE)
    cblk = (None, None, PAST_LEN, LANE)
    return pl.pallas_call(
        _na_kernel,
        out_shape=jax.ShapeDtypeStruct((N_LAT, NA_WIDTH), BF16),
        grid=(hp, DEC_BATCH),
        in_specs=[
            pl.BlockSpec(blk, lambda h, b: (row0 + b, P_Q // LANE + h)),
            pl.BlockSpec(blk, lambda h, b: (row0 + b, P_K // LANE + h)),
            pl.BlockSpec(blk, lambda h, b: (row0 + b, P_V // LANE + h)),
            pl.BlockSpec(cblk, lambda h, b: (b, layer, 0, h)),
            pl.BlockSpec(cblk, lambda h, b: (b, layer, 0, h)),
            pl.BlockSpec((2, WIN_R, GRID_W, WIN_R * GRID_W), lambda h, b: (h, 0, 0, 0)),
        ],
        out_specs=pl.BlockSpec(blk, lambda h, b: (b, h)),
        compiler_params=pltpu.CompilerParams(
            dimension_semantics=("parallel", "parallel"), vmem_limit_bytes=VMEM_LIMIT),
        name="na_attn",
    )(p, p, p, cache_k, cache_v, bias)


def _bias_table(rpb_l):
    cols = jnp.arange(GRID_W)
    cs = jnp.clip(cols - WIN_C // 2, 0, GRID_W - WIN_C)
    mask = (cols[None, :] >= cs[:, None]) & (cols[None, :] < cs[:, None] + WIN_C)
    idx = jnp.clip(cols[None, :] - cols[:, None] + WIN_C - 1, 0, 2 * WIN_C - 2)
    cb = jnp.where(mask, rpb_l[:, :, idx].astype(F32), MASK_NEG)
    d = jnp.arange(WIN_R)[:, None] + jnp.arange(WIN_R)[None, :]
    t = cb[:, d]
    return jnp.transpose(t, (0, 1, 3, 2, 4)).reshape(NA_HEADS, WIN_R, GRID_W, WIN_R * GRID_W)


def _ssd_kernel(*refs, seq_len, has_h0, emit_state):
    it = iter(refs)
    x_ref, b_ref, c_ref, z_ref, dt_ref = (next(it) for _ in range(5))
    cwx_ref, cwb_ref, cwc_ref, cbx_ref, cbb_ref, cbc_ref = (next(it) for _ in range(6))
    dtb_ref, alog_ref, dsk_ref = (next(it) for _ in range(3))
    if has_h0:
        h0f_ref, h0b_ref = next(it), next(it)
    y_ref = next(it)
    if emit_state:
        hf_ref, hb_ref = next(it), next(it)
    raw_s, xc_s, bt_s, c_s, cum_s, dtv_s, y_s, st_s = (next(it) for _ in range(8))

    n_chunks = seq_len // CHUNK
    pad = SUBLANE
    win = CHUNK + 2 * pad
    xw_, bw_ = GROUP_W, D_STATE

    raw_s[0:pad, :] = jnp.zeros((pad, xw_ + 2 * bw_), F32)
    raw_s[seq_len + pad:seq_len + 2 * pad, :] = jnp.zeros((pad, xw_ + 2 * bw_), F32)

    def fill(c, carry):
        r0 = pl.multiple_of(c * CHUNK, CHUNK)
        raw_s[pl.ds(r0 + pad, CHUNK), 0:xw_] = x_ref[pl.ds(r0, CHUNK), :].astype(F32)
        raw_s[pl.ds(r0 + pad, CHUNK), xw_:xw_ + bw_] = b_ref[pl.ds(r0, CHUNK), :].astype(F32)
        raw_s[pl.ds(r0 + pad, CHUNK), xw_ + bw_:xw_ + 2 * bw_] = c_ref[pl.ds(r0, CHUNK), :].astype(F32)
        return carry

    lax.fori_loop(0, n_chunks, fill, 0)

    lane = lax.broadcasted_iota(jnp.int32, (1, LANE), 1)
    ii = lax.broadcasted_iota(jnp.int32, (CHUNK, CHUNK), 0)
    jj = lax.broadcasted_iota(jnp.int32, (CHUNK, CHUNK), 1)
    lower = jj < ii
    diag = jj == ii
    tril = jnp.where(jj <= ii, 1.0, 0.0).astype(BF16)
    triu = jnp.where(jj >= ii, 1.0, 0.0).astype(BF16)
    er = lax.broadcasted_iota(jnp.int32, (LANE, GROUP_W), 0)
    ec = lax.broadcasted_iota(jnp.int32, (LANE, GROUP_W), 1) // SSD_HEADDIM
    exp_f = jnp.where(er == ec, 1.0, 0.0).astype(BF16)
    exp_b = jnp.where(er == ec + HEADS_PER_GROUP, 1.0, 0.0).astype(BF16)
    lo64 = lane < SSD_HEADDIM
    is_fwd = lane < HEADS_PER_GROUP
    a_neg = -jnp.exp(alog_ref[...])
    dtb = dtb_ref[...]
    dsk = dsk_ref[...]

    def conv_silu(v, w_ref, bias_ref):
        w = w_ref[...]
        acc = (w[0:1] * pltpu.roll(v, 2, 0) + w[1:2] * pltpu.roll(v, 1, 0)
               + w[2:3] * v + w[3:4] * pltpu.roll(v, win - 1, 0))
        u = acc[pad:pad + CHUNK] + bias_ref[...]
        return u * _sigmoid(u)

    if has_h0:
        st_s[...] = h0f_ref[...].T
    else:
        st_s[...] = jnp.zeros((D_STATE, GROUP_W), F32)

    def fwd(c, carry):
        r0 = pl.multiple_of(c * CHUNK, CHUNK)
        v = raw_s[pl.ds(r0, win), :]
        xc = conv_silu(v[:, 0:xw_], cwx_ref, cbx_ref)
        bc = conv_silu(v[:, xw_:xw_ + bw_], cwb_ref, cbb_ref)
        cc = conv_silu(v[:, xw_ + bw_:xw_ + 2 * bw_], cwc_ref, cbc_ref)
        xcb = xc.astype(BF16)
        bcb = bc.astype(BF16)
        ccb = cc.astype(BF16)
        btb = bc.T.astype(BF16)
        xc_s[pl.ds(r0, CHUNK), :] = xcb
        bt_s[pl.ds(r0, CHUNK), :] = btb
        c_s[pl.ds(r0, CHUNK), :] = ccb

        raw_dt = dt_ref[pl.ds(r0, CHUNK), :] + dtb
        sp = jnp.maximum(raw_dt, 0.0) + jnp.log1p(jnp.exp(-jnp.abs(raw_dt)))
        dtv = jnp.where(lane < 2 * HEADS_PER_GROUP, sp, 0.0)
        a = dtv * a_neg
        cum = jnp.where(is_fwd, _split_dot_lhs(tril, a), _split_dot_lhs(triu, a))
        cum_s[pl.ds(r0, CHUNK), :] = cum
        dtv_s[pl.ds(r0, CHUNK), :] = dtv
        cum_t = cum.T
        dt_t = dtv.T

        g = _dot_nt(ccb, bcb)
        parts = []
        for pp in range(HEADS_PER_GROUP // 2):
            ws = []
            for hh in (2 * pp, 2 * pp + 1):
                hb = HEADS_PER_GROUP + hh
                seg = jnp.where(lower, cum[:, hh:hh + 1] - cum_t[hh:hh + 1, :],
                                cum[:, hb:hb + 1] - cum_t[hb:hb + 1, :])
                d_f = dt_t[hh:hh + 1, :]
                dtm = jnp.where(lower, d_f, dt_t[hb:hb + 1, :]) + jnp.where(diag, d_f, 0.0)
                ws.append((g * jnp.exp(seg) * dtm).astype(BF16))
            wp = jnp.concatenate(ws, axis=1)
            xp = xcb[:, pp * LANE:(pp + 1) * LANE]
            zero = jnp.zeros_like(xp)
            rhs = jnp.concatenate([jnp.where(lo64, xp, zero), jnp.where(lo64, zero, xp)], axis=0)
            parts.append(_dot(wp, rhs))
        y = jnp.concatenate(parts, axis=1)

        ef_x = _split_dot(jnp.exp(cum), exp_f, 2)
        to_end = jnp.where(is_fwd, cum[CHUNK - 1:CHUNK, :] - cum, 0.0)
        ff_x = _split_dot(jnp.exp(to_end) * dtv, exp_f, 2)
        st = st_s[...]
        y = y + _dot(ccb, st.astype(BF16)) * ef_x + dsk * xc
        st_s[...] = st * ef_x[CHUNK - 1:CHUNK, :] + _dot(btb, (xc * ff_x).astype(BF16))
        y_s[pl.ds(r0, CHUNK), :] = y
        return carry

    lax.fori_loop(0, n_chunks, fwd, 0)

    if emit_state:
        hf_ref[...] = st_s[...].T
    if has_h0:
        st_s[...] = h0b_ref[...].T
    else:
        st_s[...] = jnp.zeros((D_STATE, GROUP_W), F32)

    def bwd(k, carry):
        c = n_chunks - 1 - k
        r0 = pl.multiple_of(c * CHUNK, CHUNK)
        cum = cum_s[pl.ds(r0, CHUNK), :]
        dtv = dtv_s[pl.ds(r0, CHUNK), :]
        eb_x = _split_dot(jnp.exp(cum), exp_b, 2)
        to_start = jnp.where(is_fwd, 0.0, cum[0:1, :] - cum)
        fb_x = _split_dot(jnp.exp(to_start) * dtv, exp_b, 2)
        xc = xc_s[pl.ds(r0, CHUNK), :].astype(F32)
        st = st_s[...]
        y = y_s[pl.ds(r0, CHUNK), :] + _dot(c_s[pl.ds(r0, CHUNK), :], st.astype(BF16)) * eb_x
        st_s[...] = st * eb_x[0:1, :] + _dot(bt_s[pl.ds(r0, CHUNK), :], (xc * fb_x).astype(BF16))
        z = z_ref[pl.ds(r0, CHUNK), :].astype(F32)
        y_ref[pl.ds(r0, CHUNK), :] = (y * (z * _sigmoid(z))).astype(BF16)
        return carry

    lax.fori_loop(0, n_chunks, bwd, 0)

    if emit_state:
        hb_ref[...] = st_s[...].T


def _split_dot_lhs(tri, a):
    a1 = a.astype(BF16)
    r1 = a - a1.astype(F32)
    a2 = r1.astype(BF16)
    a3 = (r1 - a2.astype(F32)).astype(BF16)
    return _dot(tri, a1) + _dot(tri, a2) + _dot(tri, a3)


def _ssd_mixer(p, dt32, conv_w, conv_b, dtb_g, alog_g, dsk, *, context, layer=0, h0f=None, h0b=None):
    seq_len = SEQ if context else DEC_SEQ
    n_seq = BATCH if context else DEC_BATCH
    row0 = 0 if context else N_CTX // DEC_SEQ
    has_h0 = h0f is not None
    emit_state = context
    xw_, bw_ = GROUP_W, D_STATE
    x_cb = P_XBC // xw_
    b_cb = (P_XBC + SSD_INNER) // bw_
    c_cb = b_cb + SSD_GROUPS
    z_cb = P_Z // xw_

    in_specs = [
        pl.BlockSpec((seq_len, xw_), lambda b, g: (row0 + b, x_cb + g)),
        pl.BlockSpec((seq_len, bw_), lambda b, g: (row0 + b, b_cb + g)),
        pl.BlockSpec((seq_len, bw_), lambda b, g: (row0 + b, c_cb + g)),
        pl.BlockSpec((seq_len, xw_), lambda b, g: (row0 + b, z_cb + g)),
        pl.BlockSpec((seq_len, LANE), lambda b, g: (row0 + b, g)),
        pl.BlockSpec((SSD_CONV, xw_), lambda b, g: (0, g)),
        pl.BlockSpec((SSD_CONV, bw_), lambda b, g: (0, SSD_INNER // bw_ + g)),
        pl.BlockSpec((SSD_CONV, bw_), lambda b, g: (0, SSD_INNER // bw_ + SSD_GROUPS + g)),
        pl.BlockSpec((1, xw_), lambda b, g: (0, g)),
        pl.BlockSpec((1, bw_), lambda b, g: (0, SSD_INNER // bw_ + g)),
        pl.BlockSpec((1, bw_), lambda b, g: (0, SSD_INNER // bw_ + SSD_GROUPS + g)),
        pl.BlockSpec((None, 1, LANE), lambda b, g: (g, 0, 0)),
        pl.BlockSpec((None, 1, LANE), lambda b, g: (g, 0, 0)),
        pl.BlockSpec((1, xw_), lambda b, g: (0, g)),
    ]
    args = [p, p, p, p, dt32, conv_w, conv_w, conv_w, conv_b, conv_b, conv_b, dtb_g, alog_g, dsk]
    if has_h0:
        st_spec = pl.BlockSpec((None, None, xw_, D_STATE), lambda b, g: (b, layer, g, 0))
        in_specs += [st_spec, st_spec]
        args += [h0f, h0b]
    out_shape = [jax.ShapeDtypeStruct((n_seq * seq_len, SSD_INNER), BF16)]
    out_specs = [pl.BlockSpec((seq_len, xw_), lambda b, g: (b, g))]
    if emit_state:
        so = pl.BlockSpec((None, xw_, D_STATE), lambda b, g: (b, g, 0))
        out_shape += [jax.ShapeDtypeStruct((n_seq, SSD_INNER, D_STATE), F32)] * 2
        out_specs += [so, so]
    scratch = [
        pltpu.VMEM((seq_len + 2 * SUBLANE, xw_ + 2 * bw_), F32),
        pltpu.VMEM((seq_len, xw_), BF16),
        pltpu.VMEM((seq_len, bw_), BF16),
        pltpu.VMEM((seq_len, bw_), BF16),
        pltpu.VMEM((seq_len, LANE), F32),
        pltpu.VMEM((seq_len, LANE), F32),
        pltpu.VMEM((seq_len, xw_), F32),
        pltpu.VMEM((D_STATE, xw_), F32),
    ]
    return pl.pallas_call(
        functools.partial(_ssd_kernel, seq_len=seq_len, has_h0=has_h0, emit_state=emit_state),
        out_shape=tuple(out_shape),
        grid=(n_seq, SSD_GROUPS),
        in_specs=in_specs,
        out_specs=tuple(out_specs),
        scratch_shapes=scratch,
        compiler_params=pltpu.CompilerParams(
            dimension_semantics=("parallel", "parallel"), vmem_limit_bytes=VMEM_LIMIT),
        name="ssd_ctx" if context else "ssd_lat",
    )(*args)


def _merge_kernel(x_ref, nac_ref, nal_ref, sc_ref, sl_ref, ga_ref, gb_ref, mod_ref, sg_ref,
                  wna_ref, wssd_ref, wo_ref, o_ref, *, n_ctx_tiles):
    is_ctx = pl.program_id(0) < n_ctx_tiles
    na = jnp.where(is_ctx, nac_ref[...], nal_ref[...])
    y = jnp.where(is_ctx, sc_ref[...], sl_ref[...]).astype(F32)
    ms = jnp.mean(y * y, axis=-1, keepdims=True)
    yn = (y * lax.rsqrt(ms + EPS) * sg_ref[...]).astype(BF16)
    a = _dot(na, wna_ref[...])
    s = _dot(yn, wssd_ref[...])
    mix = _sigmoid(ga_ref[...].astype(F32)) * a + _sigmoid(gb_ref[...].astype(F32)) * s
    o = _dot(mix.astype(BF16), wo_ref[...])
    gate = mod_ref[:, 2 * D_MODEL:3 * D_MODEL]
    o_ref[...] = x_ref[...] + gate * o


def _merge(x, na_ctx, na_lat, ssd_ctx, ssd_lat, p, mod, ssd_g, w_na, w_ssd, w_o):
    tm = TM_MERGE
    n_ctx_tiles = N_CTX // tm
    ctx_i = lambda i: (jnp.minimum(i, n_ctx_tiles - 1), 0)
    lat_i = lambda i: (jnp.maximum(i - n_ctx_tiles, 0), 0)
    const = lambda i: (0, 0)
    return pl.pallas_call(
        functools.partial(_merge_kernel, n_ctx_tiles=n_ctx_tiles),
        out_shape=jax.ShapeDtypeStruct((N_TOK, D_MODEL), F32),
        grid=(N_TOK // tm,),
        in_specs=[
            pl.BlockSpec((tm, D_MODEL), lambda i: (i, 0)),
            pl.BlockSpec((tm, NA_WIDTH), ctx_i),
            pl.BlockSpec((tm, NA_WIDTH), lat_i),
            pl.BlockSpec((tm, SSD_INNER), ctx_i),
            pl.BlockSpec((tm, SSD_INNER), lat_i),
            pl.BlockSpec((tm, D_MODEL), lambda i: (i, P_GA // D_MODEL)),
            pl.BlockSpec((tm, D_MODEL), lambda i: (i, P_GB // D_MODEL)),
            pl.BlockSpec((None, 1, 6 * D_MODEL), lambda i: (_mod_row(i, tm), 0, 0)),
            pl.BlockSpec((1, SSD_INNER), const),
            pl.BlockSpec((NA_WIDTH, D_MODEL), const),
            pl.BlockSpec((SSD_INNER, D_MODEL), const),
            pl.BlockSpec((D_MODEL, D_MODEL), const),
        ],
        out_specs=pl.BlockSpec((tm, D_MODEL), lambda i: (i, 0)),
        compiler_params=pltpu.CompilerParams(
            dimension_semantics=("parallel",), vmem_limit_bytes=VMEM_LIMIT),
        name="merge",
    )(x, na_ctx, na_lat, ssd_ctx, ssd_lat, p, p, mod, ssd_g, w_na, w_ssd, w_o)


def _ffn_kernel(x_ref, xp_ref, xn_ref, mod_ref, g_ref, wv_ref, wg_ref, cwv_ref, cwg_ref,
                cbv_ref, cbg_ref, wd_ref, o_ref, h_ref, acc_ref, *, tm):
    i = pl.program_id(0)
    j = pl.program_id(1)
    halo = FFN_HALO
    rows = tm + 2 * halo

    @pl.when(j == 0)
    def _():
        shift = mod_ref[:, 3 * D_MODEL:4 * D_MODEL]
        scale = mod_ref[:, 4 * D_MODEL:5 * D_MODEL]
        g = g_ref[...]

        def norm_mod(xv):
            ms = jnp.mean(xv * xv, axis=-1, keepdims=True)
            return (xv * lax.rsqrt(ms + EPS) * g * (1.0 + scale) + shift).astype(BF16)

        h_ref[0:halo, :] = norm_mod(xp_ref[...])
        h_ref[halo:halo + tm, :] = norm_mod(x_ref[...])
        h_ref[halo + tm:rows, :] = norm_mod(xn_ref[...])
        acc_ref[...] = jnp.zeros((tm, D_MODEL), F32)

    seq = jnp.where(i * tm < N_CTX, SEQ, DEC_SEQ)
    pos = (i * tm + lax.broadcasted_iota(jnp.int32, (tm, 1), 0)) & (seq - 1)
    has_prev = pos != 0
    has_next = pos != seq - 1
    h = h_ref[...]

    def conv(w_ref, cw_ref, cb_ref):
        pre = _dot(h, w_ref[...])
        cw = cw_ref[...]
        left = pltpu.roll(pre, 1, 0)[halo:halo + tm]
        right = pltpu.roll(pre, rows - 1, 0)[halo:halo + tm]
        return (cw[0:1] * jnp.where(has_prev, left, 0.0) + cw[1:2] * pre[halo:halo + tm]
                + cw[2:3] * jnp.where(has_next, right, 0.0) + cb_ref[...])

    val = conv(wv_ref, cwv_ref, cbv_ref)
    gate = conv(wg_ref, cwg_ref, cbg_ref)
    act = (gate * _sigmoid(gate) * val).astype(BF16)
    acc_ref[...] += _dot(act, wd_ref[...])

    @pl.when(j == pl.num_programs(1) - 1)
    def _():
        g2 = mod_ref[:, 5 * D_MODEL:6 * D_MODEL]
        o_ref[...] = x_ref[...] + g2 * acc_ref[...]


def _conv_ffn(x, mod, g2, w_up, conv_w, conv_b, w_down):
    tm = TM_FFN
    halo = FFN_HALO
    n_ff = D_FF // FF_TILE
    hb = tm // halo
    last_h = N_TOK // halo - 1
    return pl.pallas_call(
        functools.partial(_ffn_kernel, tm=tm),
        out_shape=jax.ShapeDtypeStruct((N_TOK, D_MODEL), F32),
        grid=(N_TOK // tm, n_ff),
        in_specs=[
            pl.BlockSpec((tm, D_MODEL), lambda i, j: (i, 0)),
            pl.BlockSpec((halo, D_MODEL), lambda i, j: (jnp.maximum(i * hb - 1, 0), 0)),
            pl.BlockSpec((halo, D_MODEL), lambda i, j: (jnp.minimum((i + 1) * hb, last_h), 0)),
            pl.BlockSpec((None, 1, 6 * D_MODEL), lambda i, j: (_mod_row(i, tm), 0, 0)),
            pl.BlockSpec((1, D_MODEL), lambda i, j: (0, 0)),
            pl.BlockSpec((D_MODEL, FF_TILE), lambda i, j: (0, j)),
            pl.BlockSpec((D_MODEL, FF_TILE), lambda i, j: (0, n_ff + j)),
            pl.BlockSpec((FFN_CONV, FF_TILE), lambda i, j: (0, j)),
            pl.BlockSpec((FFN_CONV, FF_TILE), lambda i, j: (0, n_ff + j)),
            pl.BlockSpec((1, FF_TILE), lambda i, j: (0, j)),
            pl.BlockSpec((1, FF_TILE), lambda i, j: (0, n_ff + j)),
            pl.BlockSpec((FF_TILE, D_MODEL), lambda i, j: (j, 0)),
        ],
        out_specs=pl.BlockSpec((tm, D_MODEL), lambda i, j: (i, 0)),
        scratch_shapes=[
            pltpu.VMEM((tm + 2 * halo, D_MODEL), BF16),
            pltpu.VMEM((tm, D_MODEL), F32),
        ],
        compiler_params=pltpu.CompilerParams(
            dimension_semantics=("parallel", "arbitrary"), vmem_limit_bytes=VMEM_LIMIT),
        name="conv_ffn",
    )(x, x, x, mod, g2, w_up, w_up, conv_w, conv_w, conv_b, conv_b, w_down)


def _reorder_w_in(w):
    dt0 = P_XBC + CONV_DIM
    ga0 = dt0 + 2 * SSD_HEADS
    dt = w[:, dt0:ga0]
    hg = HEADS_PER_GROUP
    blocks = []
    for g in range(SSD_GROUPS):
        blocks += [dt[:, g * hg:(g + 1) * hg], dt[:, SSD_HEADS + g * hg:SSD_HEADS + (g + 1) * hg],
                   jnp.zeros((D_MODEL, LANE - 2 * hg), w.dtype)]
    return jnp.concatenate([w[:, :dt0], w[:, ga0:]] + blocks, axis=1).astype(BF16)


def _by_group(v):
    hg = HEADS_PER_GROUP
    v = v.astype(F32).reshape(2, SSD_GROUPS, hg)
    both = jnp.concatenate([v[0], v[1]], axis=1)
    return jnp.pad(both, ((0, 0), (0, LANE - 2 * hg))).reshape(SSD_GROUPS, 1, LANE)


def kernel(x_prompt, x_sample, c, cache_k, cache_v, state_ssd_fwd, state_ssd_bwd, c_ctx, w_ada, b_ada, norm1_g, w_in, q_norm_g, k_norm_g, rpb, ssd_conv_w, ssd_conv_b, a_log, dt_bias, d_skip, ssd_norm_g, w_na_out, w_ssd_out, w_o, norm2_g, w_up, ffn_conv_w, ffn_conv_b, w_down):
    x = jnp.concatenate([x_prompt.reshape(N_CTX, D_MODEL), x_sample.reshape(N_LAT, D_MODEL)], axis=0)
    cond = jnp.concatenate([c_ctx[None, :], c, jnp.zeros((16 - 1 - DEC_BATCH, D_MODEL), F32)], axis=0)
    mod = _modulation(cond, w_ada, b_ada.reshape(DEPTH, 1, 6 * D_MODEL))
    mod = mod.reshape(DEPTH, 16, 1, 6 * D_MODEL)

    ck = cache_k.reshape(DEC_BATCH, DEPTH, PAST_LEN, NA_WIDTH)
    cv = cache_v.reshape(DEC_BATCH, DEPTH, PAST_LEN, NA_WIDTH)
    h0f = state_ssd_fwd.reshape(DEC_BATCH, DEPTH, SSD_INNER, D_STATE)
    h0b = state_ssd_bwd.reshape(DEC_BATCH, DEPTH, SSD_INNER, D_STATE)
    bd =(jnp.arange(IN_TILE_N)[:, None] // HEAD_DIM == jnp.arange(IN_TILE_N)[None, :] // HEAD_DIM).astype(BF16)

    new_k, new_v, new_hf, new_hb = [], [], [], []
    for l in range(DEPTH):
        qkg = jnp.concatenate([jnp.tile(q_norm_g[l] * HEAD_DIM ** -0.5, NA_HEADS),
                               jnp.tile(k_norm_g[l], NA_HEADS)])[None, :]
        p, kv32, dt32 = _in_projection(x, mod[l], norm1_g[l][None, :], _reorder_w_in(w_in[l]), qkg, bd)
        new_k.append(kv32[:, :NA_WIDTH].reshape(BATCH, SEQ, NA_HEADS, HEAD_DIM))
        new_v.append(kv32[:, NA_WIDTH:].reshape(BATCH, SEQ, NA_HEADS, HEAD_DIM))

        na_ctx = _context_attention(p)
        na_lat = _neighbourhood_attention(p, ck, cv, _bias_table(rpb[l]), l)

        conv_b = ssd_conv_b[l][None, :]
        dtb_g = _by_group(dt_bias[l])
        alog_g = _by_group(a_log[l])
        dsk = jnp.repeat(d_skip[l], SSD_HEADDIM)[None, :]
        ssd_ctx, hf, hb = _ssd_mixer(p, dt32, ssd_conv_w[l], conv_b, dtb_g, alog_g, dsk, context=True)
        (ssd_lat,) = _ssd_mixer(p, dt32, ssd_conv_w[l], conv_b, dtb_g, alog_g, dsk, context=False,
                                layer=l, h0f=h0f, h0b=h0b)
        new_hf.append(hf.reshape(BATCH, SSD_HEADS, SSD_HEADDIM, D_STATE))
        new_hb.append(hb.reshape(BATCH, SSD_HEADS, SSD_HEADDIM, D_STATE))

        x = _merge(x, na_ctx, na_lat, ssd_ctx, ssd_lat, p, mod[l], ssd_norm_g[l][None, :],
                   w_na_out[l].astype(BF16), w_ssd_out[l].astype(BF16), w_o[l].astype(BF16))
        x = _conv_ffn(x, mod[l], norm2_g[l][None, :], w_up[l].astype(BF16), ffn_conv_w[l],
                      ffn_conv_b[l][None, :], w_down[l].astype(BF16))

    y_prompt = x[:N_CTX].reshape(BATCH, SEQ, D_MODEL)
    y_sample = x[N_CTX:].reshape(DEC_BATCH, DEC_SEQ, D_MODEL)
    return (y_prompt, y_sample, jnp.stack(new_k, axis=1), jnp.stack(new_v, axis=1),
            jnp.stack(new_hf, axis=1), jnp.stack(new_hb, axis=1))
```

```python
import functools

import jax
import jax.numpy as jnp
from jax import lax
from jax.experimental import pallas as pl
from jax.experimental.pallas import tpu as pltpu

D_MODEL = 1024
BATCH = 16
SEQ = 256
DEPTH = 2
DEC_BATCH = 8
DEC_SEQ = 2048
PAST_LEN = 256
GRID_W = 64
NA_HEADS = 16
HEAD_DIM = 64
NA_WIDTH = NA_HEADS * HEAD_DIM
WIN_R = 8
WIN_C = 16
SSD_INNER = 2 * D_MODEL
SSD_HEADDIM = 64
SSD_HEADS = SSD_INNER // SSD_HEADDIM
SSD_GROUPS = 4
D_STATE = 128
SSD_CONV = 4
CHUNK = 128
CONV_DIM = SSD_INNER + 2 * SSD_GROUPS * D_STATE
D_FF = 2816
FFN_CONV = 3
EPS = 1e-6

N_CTX = BATCH * SEQ
N_LAT = DEC_BATCH * DEC_SEQ
N_TOK = N_CTX + N_LAT
GRID_ROWS = DEC_SEQ // GRID_W
HEADS_PER_GROUP = SSD_HEADS // SSD_GROUPS
GROUP_W = HEADS_PER_GROUP * SSD_HEADDIM

LANE = 128
SUBLANE = 8
BF16_ROWS = 16
MXU_N = 256

P_Q, P_K, P_V, P_Z, P_XBC = 0, 1024, 2048, 3072, 5120
P_GA, P_GB, P_DT = 8192, 9216, 10240
DT_W = SSD_GROUPS * LANE
P_W = P_DT + DT_W
IN_TILE_N = 2 * MXU_N
N_IN_TILES = P_W // IN_TILE_N
N_QK_TILES = (P_V - P_Q) // IN_TILE_N
N_QKV_TILES = (P_Z - P_Q) // IN_TILE_N
DT_TILE = P_DT // IN_TILE_N

TM_IN = 1024
TM_MERGE = 512
TM_FFN = 1024
FF_TILE = 256
FFN_HALO = BF16_ROWS
MASK_NEG = -1e30

NA_QROWS = 4
NA_KROWS = 12
NA_Q = NA_QROWS * GRID_W
NA_KW = NA_KROWS * GRID_W
NA_KBLK = 256
NA_STEPS = GRID_ROWS // NA_QROWS
NA_VARIANTS = 3

F32 = jnp.float32
BF16 = jnp.bfloat16
NT = (((1,), (1,)), ((), ()))

VMEM_LIMIT = 56 * 1024 * 1024


def _sigmoid(x):
    return jax.nn.sigmoid(x)


def _dot(a, b):
    return jnp.dot(a, b, preferred_element_type=F32)


def _dot_nt(a, b):
    return lax.dot_general(a, b, NT, preferred_element_type=F32)


def _split_dot(a, b, terms):
    acc = None
    r = a
    for t in range(terms):
        p = r.astype(BF16)
        d = _dot(p, b)
        acc = d if acc is None else acc + d
        if t + 1 < terms:
            r = r - p.astype(F32)
    return acc


def _mod_row(i, tm):
    start = i * tm
    return jnp.where(start < N_CTX, 0, 1 + (start - N_CTX) // DEC_SEQ)


def _mod_kernel(c_ref, w_ref, b_ref, o_ref):
    c = c_ref[...]
    s = (c * _sigmoid(c)).astype(BF16)
    o_ref[...] = _dot(s, w_ref[...].astype(BF16)) + b_ref[...]


def _modulation(cond, w_ada, b_ada):
    tn = 1536
    return pl.pallas_call(
        _mod_kernel,
        out_shape=jax.ShapeDtypeStruct((DEPTH, 16, 6 * D_MODEL), F32),
        grid=(DEPTH, 6 * D_MODEL // tn),
        in_specs=[
            pl.BlockSpec((16, D_MODEL), lambda l, j: (0, 0)),
            pl.BlockSpec((None, D_MODEL, tn), lambda l, j: (l, 0, j)),
            pl.BlockSpec((None, 1, tn), lambda l, j: (l, 0, j)),
        ],
        out_specs=pl.BlockSpec((None, 16, tn), lambda l, j: (l, 0, j)),
        compiler_params=pltpu.CompilerParams(
            dimension_semantics=("parallel", "parallel"), vmem_limit_bytes=VMEM_LIMIT),
        name="adaln_mod",
    )(cond, w_ada, b_ada)


def _inproj_kernel(x_ref, mod_ref, g_ref, w_ref, qkg_ref, bd_ref,
                   p_ref, kv_ref, dt_ref, h_ref, *, n_ctx_tiles):
    i = pl.program_id(0)
    j = pl.program_id(1)

    @pl.when(j == 0)
    def _():
        x = x_ref[...]
        ms = jnp.mean(x * x, axis=-1, keepdims=True)
        y = x * lax.rsqrt(ms + EPS) * g_ref[...]
        shift = mod_ref[:, 0:D_MODEL]
        scale = mod_ref[:, D_MODEL:2 * D_MODEL]
        h_ref[...] = (y * (1.0 + scale) + shift).astype(BF16)

    acc = _dot(h_ref[...], w_ref[...])
    p_ref[...] = acc.astype(BF16)
    is_ctx = i < n_ctx_tiles

    @pl.when(j < N_QK_TILES)
    def _():
        sq = acc * acc
        ss = jnp.concatenate(
            [_split_dot(sq[:, t * MXU_N:(t + 1) * MXU_N], bd_ref[...], 2) for t in range(IN_TILE_N // MXU_N)],
            axis=1)
        yn = acc * lax.rsqrt(ss * (1.0 / HEAD_DIM) + EPS) * qkg_ref[...]
        p_ref[...] = yn.astype(BF16)

        @pl.when(jnp.logical_and(j >= N_QK_TILES // 2, is_ctx))
        def _():
            kv_ref[...] = yn

    @pl.when(jnp.logical_and(jnp.logical_and(j >= N_QK_TILES, j < N_QKV_TILES), is_ctx))
    def _():
        kv_ref[...] = acc

    @pl.when(j == DT_TILE)
    def _():
        dt_ref[...] = acc


def _in_projection(x, mod, g1, w, qkg, bd):
    n_i = N_TOK // TM_IN
    n_ctx_tiles = N_CTX // TM_IN
    kv0 = N_QK_TILES // 2
    n_kv = N_QKV_TILES - kv0

    def kv_map(i, j):
        ii = jnp.minimum(i, n_ctx_tiles - 1)
        jj = jnp.where(i < n_ctx_tiles, jnp.clip(j - kv0, 0, n_kv - 1), n_kv - 1)
        return (ii, jj)

    return pl.pallas_call(
        functools.partial(_inproj_kernel, n_ctx_tiles=n_ctx_tiles),
        out_shape=(
            jax.ShapeDtypeStruct((N_TOK, P_W), BF16),
            jax.ShapeDtypeStruct((N_CTX, 2 * NA_WIDTH), F32),
            jax.ShapeDtypeStruct((N_TOK, DT_W), F32),
        ),
        grid=(n_i, N_IN_TILES),
        in_specs=[
            pl.BlockSpec((TM_IN, D_MODEL), lambda i, j: (i, 0)),
            pl.BlockSpec((None, 1, 6 * D_MODEL), lambda i, j: (_mod_row(i, TM_IN), 0, 0)),
            pl.BlockSpec((1, D_MODEL), lambda i, j: (0, 0)),
            pl.BlockSpec((D_MODEL, IN_TILE_N), lambda i, j: (0, j)),
            pl.BlockSpec((1, IN_TILE_N), lambda i, j: (0, jnp.minimum(j, N_QK_TILES - 1))),
            pl.BlockSpec((MXU_N, MXU_N), lambda i, j: (0, 0)),
        ],
        out_specs=(
            pl.BlockSpec((TM_IN, IN_TILE_N), lambda i, j: (i, j)),
            pl.BlockSpec((TM_IN, IN_TILE_N), kv_map),
            pl.BlockSpec((TM_IN, DT_W), lambda i, j: (i, 0)),
        ),
        scratch_shapes=[pltpu.VMEM((TM_IN, D_MODEL), BF16)],
        compiler_params=pltpu.CompilerParams(
            dimension_semantics=("arbitrary", "arbitrary"), vmem_limit_bytes=VMEM_LIMIT),
        name="in_proj",
    )(x, mod, g1, w, qkg, bd)


def _head_masks():
    lane = lax.broadcasted_iota(jnp.int32, (1, LANE), 1)
    return lane < HEAD_DIM


def _ctx_attn_kernel(q_ref, k_ref, v_ref, o_ref):
    q = q_ref[...]
    k = k_ref[...]
    v = v_ref[...]
    lo = _head_masks()
    outs = []
    for hh in range(2):
        m = lo if hh == 0 else jnp.logical_not(lo)
        qm = jnp.where(m, q, jnp.zeros_like(q))
        s = _dot_nt(qm, k)
        mx = jnp.max(s, axis=-1, keepdims=True)
        p = jnp.exp(s - mx)
        l = jnp.sum(p, axis=-1, keepdims=True)
        outs.append(_dot(p.astype(BF16), v) / l)
    o_ref[...] = jnp.where(lo, outs[0], outs[1]).astype(BF16)


def _context_attention(p):
    hp = NA_HEADS // 2
    blk = (SEQ, LANE)
    return pl.pallas_call(
        _ctx_attn_kernel,
        out_shape=jax.ShapeDtypeStruct((N_CTX, NA_WIDTH), BF16),
        grid=(BATCH, hp),
        in_specs=[
            pl.BlockSpec(blk, lambda b, h: (b, P_Q // LANE + h)),
            pl.BlockSpec(blk, lambda b, h: (b, P_K // LANE + h)),
            pl.BlockSpec(blk, lambda b, h: (b, P_V // LANE + h)),
        ],
        out_specs=pl.BlockSpec(blk, lambda b, h: (b, h)),
        compiler_params=pltpu.CompilerParams(
            dimension_semantics=("parallel", "parallel"), vmem_limit_bytes=VMEM_LIMIT),
        name="ctx_attn",
    )(p, p, p)


def _na_bias_kernel(r_ref, o_ref):
    ck = lax.broadcasted_iota(jnp.int32, (GRID_W, LANE), 0)
    ln = lax.broadcasted_iota(jnp.int32, (GRID_W, LANE), 1)
    cq = ln & (GRID_W - 1)
    cs = jnp.clip(cq - WIN_C // 2, 0, GRID_W - WIN_C)
    col_ok = jnp.logical_and(ck >= cs, ck < cs + WIN_C)
    left = ln < GRID_W
    n_dr = 2 * WIN_R - 1
    toep = []
    for dr in range(n_dr):
        base = jnp.broadcast_to(r_ref[dr:dr + 1, :], (GRID_W, LANE))
        toep.append([pltpu.roll(base, s * GRID_W, 1, stride=1, stride_axis=0) for s in range(2)])
    neg = jnp.full((GRID_W, LANE), MASK_NEG, F32)

    def rel(v, j, q4):
        if v == 0:
            return j - q4 + WIN_R - 1, j < WIN_R
        if v == 1:
            return j - q4 + WIN_R // 2 - 1, q4 <= j < q4 + WIN_R
        return j - q4 - 1, j >= NA_KROWS - WIN_R

    for v in range(NA_VARIANTS):
        for j in range(NA_KROWS):
            for lb in range(NA_QROWS // 2):
                halves = []
                for s in range(2):
                    dr, ok = rel(v, j, 2 * lb + s)
                    halves.append(jnp.where(col_ok, toep[dr][s], neg) if ok else neg)
                o_ref[v, j * GRID_W:(j + 1) * GRID_W, lb * LANE:(lb + 1) * LANE] = (
                    jnp.where(left, halves[0], halves[1]))


def _na_bias(rrev):
    return pl.pallas_call(
        _na_bias_kernel,
        out_shape=jax.ShapeDtypeStruct((NA_HEADS // 2, NA_VARIANTS, NA_KW, 2 * NA_Q), F32),
        grid=(NA_HEADS,),
        in_specs=[pl.BlockSpec((None, 2 * WIN_R, LANE), lambda h: (h, 0, 0))],
        out_specs=pl.BlockSpec((None, NA_VARIANTS, NA_KW, NA_Q), lambda h: (h // 2, 0, 0, h % 2)),
        compiler_params=pltpu.CompilerParams(
            dimension_semantics=("parallel",), vmem_limit_bytes=VMEM_LIMIT),
        name="na_bias",
    )(rrev)


def _na_kernel(q_ref, k_ref, v_ref, kc_ref, vc_ref, bias_ref, o_ref, vt_ref):
    lo = _head_masks()
    kc = kc_ref[...].astype(BF16)
    vct = vc_ref[...].T.astype(BF16)
    for t in range(DEC_SEQ // NA_KBLK):
        vt_ref[t] = v_ref[t * NA_KBLK:(t + 1) * NA_KBLK, :].astype(F32).T.astype(BF16)
    n_wblk = NA_KW // NA_KBLK

    def body(m, carry):
        blk0 = jnp.clip(m - 1, 0, DEC_SEQ // NA_KBLK - n_wblk)
        var = jnp.where(m == 0, 0, jnp.where(m == NA_STEPS - 1, 2, 1))
        q0 = pl.multiple_of(m * NA_Q, NA_Q)
        k0 = pl.multiple_of(blk0 * NA_KBLK, NA_KBLK)
        qg = q_ref[pl.ds(q0, NA_Q), :]
        kw = k_ref[pl.ds(k0, NA_KW), :]
        zq = jnp.zeros_like(qg)
        qcat = jnp.concatenate([jnp.where(lo, qg, zq), jnp.where(lo, zq, qg)], axis=0)
        sw = _dot_nt(kw, qcat) + bias_ref[var]
        sc = _dot_nt(kc, qcat)
        mx = jnp.maximum(jnp.max(sw, axis=0, keepdims=True), jnp.max(sc, axis=0, keepdims=True))
        pw = jnp.exp(sw - mx)
        pc = jnp.exp(sc - mx)
        l = jnp.sum(pw, axis=0, keepdims=True) + jnp.sum(pc, axis=0, keepdims=True)
        pwb = pw.astype(BF16)
        o = _dot(vct, pc.astype(BF16))
        for t in range(n_wblk):
            o = o + _dot(vt_ref[blk0 + t], pwb[t * NA_KBLK:(t + 1) * NA_KBLK, :])
        o = o / l
        both = jnp.concatenate([o[0:HEAD_DIM, 0:NA_Q], o[HEAD_DIM:2 * HEAD_DIM, NA_Q:2 * NA_Q]], axis=0)
        o_ref[pl.ds(q0, NA_Q), :] = both.T.astype(BF16)
        return carry

    lax.fori_loop(0, NA_STEPS, body, 0, unroll=2)


def _neighbourhood_attention(p, cache_k, cache_v, bias, layer):
    hp = NA_HEADS // 2
    row0 = N_CTX // DEC_SEQ
    blk = (DEC_SEQ, LANE)
    cblk = (None, None, PAST_LEN, LANE)
    return pl.pallas_call(
        _na_kernel,
        out_shape=jax.ShapeDtypeStruct((N_LAT, NA_WIDTH), BF16),
        grid=(hp, DEC_BATCH),
        in_specs=[
            pl.BlockSpec(blk, lambda h, b: (row0 + b, P_Q // LANE + h)),
            pl.BlockSpec(blk, lambda h, b: (row0 + b, P_K // LANE + h)),
            pl.BlockSpec(blk, lambda h, b: (row0 + b, P_V // LANE + h)),
            pl.BlockSpec(cblk, lambda h, b: (b, layer, 0, h)),
            pl.BlockSpec(cblk, lambda h, b: (b, layer, 0, h)),
            pl.BlockSpec((None, NA_VARIANTS, NA_KW, 2 * NA_Q), lambda h, b: (h, 0, 0, 0)),
        ],
        out_specs=pl.BlockSpec(blk, lambda h, b: (b, h)),
        scratch_shapes=[pltpu.VMEM((DEC_SEQ // NA_KBLK, LANE, NA_KBLK), BF16)],
        compiler_params=pltpu.CompilerParams(
            dimension_semantics=("parallel", "parallel"), vmem_limit_bytes=VMEM_LIMIT),
        name="na_attn",
    )(p, p, p, cache_k, cache_v, bias)


def _rpb_reversed(rpb_l):
    n = WIN_C - 1
    pos = rpb_l[:, :, n::-1]
    neg = rpb_l[:, :, :n:-1]
    z = jnp.zeros(rpb_l.shape[:2] + (LANE - 2 * n - 1,), F32)
    r = jnp.concatenate([pos.astype(F32), z, neg.astype(F32)], axis=-1)
    return jnp.pad(r, ((0, 0), (0, 1), (0, 0)))


def _ssd_kernel(*refs, seq_len, has_h0, emit_state):
    it = iter(refs)
    x_ref, b_ref, c_ref, z_ref, dt_ref = (next(it) for _ in range(5))
    cwx_ref, cwb_ref, cwc_ref, cbx_ref, cbb_ref, cbc_ref = (next(it) for _ in range(6))
    dtb_ref, alog_ref, dsk_ref = (next(it) for _ in range(3))
    if has_h0:
        h0f_ref, h0b_ref = next(it), next(it)
    y_ref = next(it)
    if emit_state:
        hf_ref, hb_ref = next(it), next(it)
    raw_s, xc_s, bt_s, c_s, cum_s, dtv_s, y_s, st_s = (next(it) for _ in range(8))

    n_chunks = seq_len // CHUNK
    pad = SUBLANE
    xw_, bw_ = GROUP_W, D_STATE

    raw_s[0:pad, :] = jnp.zeros((pad, xw_ + 2 * bw_), F32)
    raw_s[seq_len + pad:seq_len + 2 * pad, :] = jnp.zeros((pad, xw_ + 2 * bw_), F32)

    def fill(c, carry):
        r0 = pl.multiple_of(c * CHUNK, CHUNK)
        raw_s[pl.ds(r0 + pad, CHUNK), 0:xw_] = x_ref[pl.ds(r0, CHUNK), :].astype(F32)
        raw_s[pl.ds(r0 + pad, CHUNK), xw_:xw_ + bw_] = b_ref[pl.ds(r0, CHUNK), :].astype(F32)
        raw_s[pl.ds(r0 + pad, CHUNK), xw_ + bw_:xw_ + 2 * bw_] = c_ref[pl.ds(r0, CHUNK), :].astype(F32)
        return carry

    lax.fori_loop(0, n_chunks, fill, 0)

    lane = lax.broadcasted_iota(jnp.int32, (1, LANE), 1)
    ii = lax.broadcasted_iota(jnp.int32, (CHUNK, CHUNK), 0)
    jj = lax.broadcasted_iota(jnp.int32, (CHUNK, CHUNK), 1)
    lower = jj < ii
    diag = jj == ii
    tril = jnp.where(jj <= ii, 1.0, 0.0).astype(BF16)
    triu = jnp.where(jj >= ii, 1.0, 0.0).astype(BF16)
    er = lax.broadcasted_iota(jnp.int32, (LANE, GROUP_W), 0)
    ec = lax.broadcasted_iota(jnp.int32, (LANE, GROUP_W), 1) // SSD_HEADDIM
    exp_f = jnp.where(er == ec, 1.0, 0.0).astype(BF16)
    exp_b = jnp.where(er == ec + HEADS_PER_GROUP, 1.0, 0.0).astype(BF16)
    lo64 = lane < SSD_HEADDIM
    is_fwd = lane < HEADS_PER_GROUP
    a_neg = -jnp.exp(alog_ref[...])
    dtb = dtb_ref[...]
    dsk = dsk_ref[...]

    win = CHUNK + 2 * pad

    def conv_silu(r0, c0, width, w_ref, bias_ref):
        v = raw_s[pl.ds(r0, win), c0:c0 + width]
        w = w_ref[...]
        acc = (w[0:1] * pltpu.roll(v, 2, 0) + w[1:2] * pltpu.roll(v, 1, 0)
               + w[2:3] * v + w[3:4] * pltpu.roll(v, win - 1, 0))
        u = acc[pad:pad + CHUNK] + bias_ref[...]
        return u * _sigmoid(u)

    if has_h0:
        st_s[...] = h0f_ref[...].T
    else:
        st_s[...] = jnp.zeros((D_STATE, GROUP_W), F32)

    def fwd(c, carry):
        r0 = pl.multiple_of(c * CHUNK, CHUNK)
        xc = conv_silu(r0, 0, xw_, cwx_ref, cbx_ref)
        bc = conv_silu(r0, xw_, bw_, cwb_ref, cbb_ref)
        cc = conv_silu(r0, xw_ + bw_, bw_, cwc_ref, cbc_ref)
        xcb = xc.astype(BF16)
        bcb = bc.astype(BF16)
        ccb = cc.astype(BF16)
        btb = bc.T.astype(BF16)
        xc_s[pl.ds(r0, CHUNK), :] = xcb
        bt_s[pl.ds(r0, CHUNK), :] = btb
        c_s[pl.ds(r0, CHUNK), :] = ccb

        raw_dt = dt_ref[pl.ds(r0, CHUNK), :] + dtb
        e = jnp.exp(-jnp.abs(raw_dt))
        log1p_e = jnp.where(e < 1e-4, e * (1.0 - 0.5 * e), jnp.log(1.0 + e))
        dtv = jnp.where(lane < 2 * HEADS_PER_GROUP, jnp.maximum(raw_dt, 0.0) + log1p_e, 0.0)
        a = dtv * a_neg
        cum = jnp.where(is_fwd, _split_dot_lhs(tril, a), _split_dot_lhs(triu, a))
        cum_s[pl.ds(r0, CHUNK), :] = cum
        dtv_s[pl.ds(r0, CHUNK), :] = dtv
        cum_t = cum.T
        dt_t = dtv.T

        g = _dot_nt(ccb, bcb)
        parts = []
        for pp in range(HEADS_PER_GROUP // 2):
            ws = []
            for hh in (2 * pp, 2 * pp + 1):
                hb = HEADS_PER_GROUP + hh
                seg = jnp.where(lower, cum[:, hh:hh + 1] - cum_t[hh:hh + 1, :],
                                cum[:, hb:hb + 1] - cum_t[hb:hb + 1, :])
                d_f = dt_t[hh:hh + 1, :]
                dtm = jnp.where(lower, d_f, dt_t[hb:hb + 1, :]) + jnp.where(diag, d_f, 0.0)
                ws.append((g * jnp.exp(seg) * dtm).astype(BF16))
            wp = jnp.concatenate(ws, axis=1)
            xp = xcb[:, pp * LANE:(pp + 1) * LANE]
            zero = jnp.zeros_like(xp)
            rhs = jnp.concatenate([jnp.where(lo64, xp, zero), jnp.where(lo64, zero, xp)], axis=0)
            parts.append(_dot(wp, rhs))
        y = jnp.concatenate(parts, axis=1)

        ef_x = _split_dot(jnp.exp(cum), exp_f, 2)
        to_end = jnp.where(is_fwd, cum[CHUNK - 1:CHUNK, :] - cum, 0.0)
        ff_x = _split_dot(jnp.exp(to_end) * dtv, exp_f, 2)
        st = st_s[...]
        y = y + _dot(ccb, st.astype(BF16)) * ef_x + dsk * xc
        st_s[...] = st * ef_x[CHUNK - 1:CHUNK, :] + _dot(btb, (xc * ff_x).astype(BF16))
        y_s[pl.ds(r0, CHUNK), :] = y
        return carry

    lax.fori_loop(0, n_chunks, fwd, 0, unroll=2)

    if emit_state:
        hf_ref[...] = st_s[...].T
    if has_h0:
        st_s[...] = h0b_ref[...].T
    else:
        st_s[...] = jnp.zeros((D_STATE, GROUP_W), F32)

    def bwd(k, carry):
        c = n_chunks - 1 - k
        r0 = pl.multiple_of(c * CHUNK, CHUNK)
        cum = cum_s[pl.ds(r0, CHUNK), :]
        dtv = dtv_s[pl.ds(r0, CHUNK), :]
        eb_x = _split_dot(jnp.exp(cum), exp_b, 2)
        to_start = jnp.where(is_fwd, 0.0, cum[0:1, :] - cum)
        fb_x = _split_dot(jnp.exp(to_start) * dtv, exp_b, 2)
        xc = xc_s[pl.ds(r0, CHUNK), :].astype(F32)
        st = st_s[...]
        y = y_s[pl.ds(r0, CHUNK), :] + _dot(c_s[pl.ds(r0, CHUNK), :], st.astype(BF16)) * eb_x
        st_s[...] = st * eb_x[0:1, :] + _dot(bt_s[pl.ds(r0, CHUNK), :], (xc * fb_x).astype(BF16))
        z = z_ref[pl.ds(r0, CHUNK), :].astype(F32)
        y_ref[pl.ds(r0, CHUNK), :] = (y * (z * _sigmoid(z))).astype(BF16)
        return carry

    lax.fori_loop(0, n_chunks, bwd, 0, unroll=2)

    if emit_state:
        hb_ref[...] = st_s[...].T


def _split_dot_lhs(tri, a):
    a1 = a.astype(BF16)
    r1 = a - a1.astype(F32)
    a2 = r1.astype(BF16)
    a3 = (r1 - a2.astype(F32)).astype(BF16)
    return _dot(tri, a1) + _dot(tri, a2) + _dot(tri, a3)


def _ssd_mixer(p, dt32, conv_w, conv_b, dtb_g, alog_g, dsk, *, context, layer=0, h0f=None, h0b=None):
    seq_len = SEQ if context else DEC_SEQ
    n_seq = BATCH if context else DEC_BATCH
    row0 = 0 if context else N_CTX // DEC_SEQ
    has_h0 = h0f is not None
    emit_state = context
    xw_, bw_ = GROUP_W, D_STATE
    x_cb = P_XBC // xw_
    b_cb = (P_XBC + SSD_INNER) // bw_
    c_cb = b_cb + SSD_GROUPS
    z_cb = P_Z // xw_

    in_specs = [
        pl.BlockSpec((seq_len, xw_), lambda b, g: (row0 + b, x_cb + g)),
        pl.BlockSpec((seq_len, bw_), lambda b, g: (row0 + b, b_cb + g)),
        pl.BlockSpec((seq_len, bw_), lambda b, g: (row0 + b, c_cb + g)),
        pl.BlockSpec((seq_len, xw_), lambda b, g: (row0 + b, z_cb + g)),
        pl.BlockSpec((seq_len, LANE), lambda b, g: (row0 + b, g)),
        pl.BlockSpec((SSD_CONV, xw_), lambda b, g: (0, g)),
        pl.BlockSpec((SSD_CONV, bw_), lambda b, g: (0, SSD_INNER // bw_ + g)),
        pl.BlockSpec((SSD_CONV, bw_), lambda b, g: (0, SSD_INNER // bw_ + SSD_GROUPS + g)),
        pl.BlockSpec((1, xw_), lambda b, g: (0, g)),
        pl.BlockSpec((1, bw_), lambda b, g: (0, SSD_INNER // bw_ + g)),
        pl.BlockSpec((1, bw_), lambda b, g: (0, SSD_INNER // bw_ + SSD_GROUPS + g)),
        pl.BlockSpec((None, 1, LANE), lambda b, g: (g, 0, 0)),
        pl.BlockSpec((None, 1, LANE), lambda b, g: (g, 0, 0)),
        pl.BlockSpec((1, xw_), lambda b, g: (0, g)),
    ]
    args = [p, p, p, p, dt32, conv_w, conv_w, conv_w, conv_b, conv_b, conv_b, dtb_g, alog_g, dsk]
    if has_h0:
        st_spec = pl.BlockSpec((None, None, xw_, D_STATE), lambda b, g: (b, layer, g, 0))
        in_specs += [st_spec, st_spec]
        args += [h0f, h0b]
    out_shape = [jax.ShapeDtypeStruct((n_seq * seq_len, SSD_INNER), BF16)]
    out_specs = [pl.BlockSpec((seq_len, xw_), lambda b, g: (b, g))]
    if emit_state:
        so = pl.BlockSpec((None, xw_, D_STATE), lambda b, g: (b, g, 0))
        out_shape += [jax.ShapeDtypeStruct((n_seq, SSD_INNER, D_STATE), F32)] * 2
        out_specs += [so, so]
    scratch = [
        pltpu.VMEM((seq_len + 2 * SUBLANE, xw_ + 2 * bw_), F32),
        pltpu.VMEM((seq_len, xw_), BF16),
        pltpu.VMEM((seq_len, bw_), BF16),
        pltpu.VMEM((seq_len, bw_), BF16),
        pltpu.VMEM((seq_len, LANE), F32),
        pltpu.VMEM((seq_len, LANE), F32),
        pltpu.VMEM((seq_len, xw_), F32),
        pltpu.VMEM((D_STATE, xw_), F32),
    ]
    return pl.pallas_call(
        functools.partial(_ssd_kernel, seq_len=seq_len, has_h0=has_h0, emit_state=emit_state),
        out_shape=tuple(out_shape),
        grid=(n_seq, SSD_GROUPS),
        in_specs=in_specs,
        out_specs=tuple(out_specs),
        scratch_shapes=scratch,
        compiler_params=pltpu.CompilerParams(
            dimension_semantics=("parallel", "parallel"), vmem_limit_bytes=VMEM_LIMIT),
        name="ssd_ctx" if context else "ssd_lat",
    )(*args)


def _merge_kernel(x_ref, nac_ref, nal_ref, sc_ref, sl_ref, ga_ref, gb_ref, mod_ref, sg_ref,
                  wna_ref, wssd_ref, wo_ref, o_ref, *, n_ctx_tiles):
    is_ctx = pl.program_id(0) < n_ctx_tiles
    na = jnp.where(is_ctx, nac_ref[...], nal_ref[...])
    y = jnp.where(is_ctx, sc_ref[...], sl_ref[...]).astype(F32)
    ms = jnp.mean(y * y, axis=-1, keepdims=True)
    yn = (y * lax.rsqrt(ms + EPS) * sg_ref[...]).astype(BF16)
    a = _dot(na, wna_ref[...])
    s = _dot(yn, wssd_ref[...])
    mix = _sigmoid(ga_ref[...].astype(F32)) * a + _sigmoid(gb_ref[...].astype(F32)) * s
    o = _dot(mix.astype(BF16), wo_ref[...])
    gate = mod_ref[:, 2 * D_MODEL:3 * D_MODEL]
    o_ref[...] = x_ref[...] + gate * o


def _merge(x, na_ctx, na_lat, ssd_ctx, ssd_lat, p, mod, ssd_g, w_na, w_ssd, w_o):
    tm = TM_MERGE
    n_ctx_tiles = N_CTX // tm
    ctx_i = lambda i: (jnp.minimum(i, n_ctx_tiles - 1), 0)
    lat_i = lambda i: (jnp.maximum(i - n_ctx_tiles, 0), 0)
    const = lambda i: (0, 0)
    return pl.pallas_call(
        functools.partial(_merge_kernel, n_ctx_tiles=n_ctx_tiles),
        out_shape=jax.ShapeDtypeStruct((N_TOK, D_MODEL), F32),
        grid=(N_TOK // tm,),
        in_specs=[
            pl.BlockSpec((tm, D_MODEL), lambda i: (i, 0)),
            pl.BlockSpec((tm, NA_WIDTH), ctx_i),
            pl.BlockSpec((tm, NA_WIDTH), lat_i),
            pl.BlockSpec((tm, SSD_INNER), ctx_i),
            pl.BlockSpec((tm, SSD_INNER), lat_i),
            pl.BlockSpec((tm, D_MODEL), lambda i: (i, P_GA // D_MODEL)),
            pl.BlockSpec((tm, D_MODEL), lambda i: (i, P_GB // D_MODEL)),
            pl.BlockSpec((None, 1, 6 * D_MODEL), lambda i: (_mod_row(i, tm), 0, 0)),
            pl.BlockSpec((1, SSD_INNER), const),
            pl.BlockSpec((NA_WIDTH, D_MODEL), const),
            pl.BlockSpec((SSD_INNER, D_MODEL), const),
            pl.BlockSpec((D_MODEL, D_MODEL), const),
        ],
        out_specs=pl.BlockSpec((tm, D_MODEL), lambda i: (i, 0)),
        compiler_params=pltpu.CompilerParams(
            dimension_semantics=("parallel",), vmem_limit_bytes=VMEM_LIMIT),
        name="merge",
    )(x, na_ctx, na_lat, ssd_ctx, ssd_lat, p, p, mod, ssd_g, w_na, w_ssd, w_o)


def _ffn_kernel(x_ref, xp_ref, xn_ref, mod_ref, g_ref, wv_ref, wg_ref, cwv_ref, cwg_ref,
                cbv_ref, cbg_ref, wd_ref, o_ref, h_ref, acc_ref, *, tm):
    i = pl.program_id(0)
    j = pl.program_id(1)
    halo = FFN_HALO
    rows = tm + 2 * halo

    @pl.when(j == 0)
    def _():
        shift = mod_ref[:, 3 * D_MODEL:4 * D_MODEL]
        scale = mod_ref[:, 4 * D_MODEL:5 * D_MODEL]
        g = g_ref[...]

        def norm_mod(xv):
            ms = jnp.mean(xv * xv, axis=-1, keepdims=True)
            return (xv * lax.rsqrt(ms + EPS) * g * (1.0 + scale) + shift).astype(BF16)

        h_ref[0:halo, :] = norm_mod(xp_ref[...])
        h_ref[halo:halo + tm, :] = norm_mod(x_ref[...])
        h_ref[halo + tm:rows, :] = norm_mod(xn_ref[...])
        acc_ref[...] = jnp.zeros((tm, D_MODEL), F32)

    seq = jnp.where(i * tm < N_CTX, SEQ, DEC_SEQ)
    pos = (i * tm + lax.broadcasted_iota(jnp.int32, (tm, 1), 0)) & (seq - 1)
    has_prev = pos != 0
    has_next = pos != seq - 1
    h = h_ref[...]

    def conv(w_ref, cw_ref, cb_ref):
        pre = _dot(h, w_ref[...])
        cw = cw_ref[...]
        left = pltpu.roll(pre, 1, 0)[halo:halo + tm]
        right = pltpu.roll(pre, rows - 1, 0)[halo:halo + tm]
        return (cw[0:1] * jnp.where(has_prev, left, 0.0) + cw[1:2] * pre[halo:halo + tm]
                + cw[2:3] * jnp.where(has_next, right, 0.0) + cb_ref[...])

    val = conv(wv_ref, cwv_ref, cbv_ref)
    gate = conv(wg_ref, cwg_ref, cbg_ref)
    act = (gate * _sigmoid(gate) * val).astype(BF16)
    acc_ref[...] += _dot(act, wd_ref[...])

    @pl.when(j == pl.num_programs(1) - 1)
    def _():
        g2 = mod_ref[:, 5 * D_MODEL:6 * D_MODEL]
        o_ref[...] = x_ref[...] + g2 * acc_ref[...]


def _conv_ffn(x, mod, g2, w_up, conv_w, conv_b, w_down):
    tm = TM_FFN
    halo = FFN_HALO
    n_ff = D_FF // FF_TILE
    hb = tm // halo
    last_h = N_TOK // halo - 1
    return pl.pallas_call(
        functools.partial(_ffn_kernel, tm=tm),
        out_shape=jax.ShapeDtypeStruct((N_TOK, D_MODEL), F32),
        grid=(N_TOK // tm, n_ff),
        in_specs=[
            pl.BlockSpec((tm, D_MODEL), lambda i, j: (i, 0)),
            pl.BlockSpec((halo, D_MODEL), lambda i, j: (jnp.maximum(i * hb - 1, 0), 0)),
            pl.BlockSpec((halo, D_MODEL), lambda i, j: (jnp.minimum((i + 1) * hb, last_h), 0)),
            pl.BlockSpec((None, 1, 6 * D_MODEL), lambda i, j: (_mod_row(i, tm), 0, 0)),
            pl.BlockSpec((1, D_MODEL), lambda i, j: (0, 0)),
            pl.BlockSpec((D_MODEL, FF_TILE), lambda i, j: (0, j)),
            pl.BlockSpec((D_MODEL, FF_TILE), lambda i, j: (0, n_ff + j)),
            pl.BlockSpec((FFN_CONV, FF_TILE), lambda i, j: (0, j)),
            pl.BlockSpec((FFN_CONV, FF_TILE), lambda i, j: (0, n_ff + j)),
            pl.BlockSpec((1, FF_TILE), lambda i, j: (0, j)),
            pl.BlockSpec((1, FF_TILE), lambda i, j: (0, n_ff + j)),
            pl.BlockSpec((FF_TILE, D_MODEL), lambda i, j: (j, 0)),
        ],
        out_specs=pl.BlockSpec((tm, D_MODEL), lambda i, j: (i, 0)),
        scratch_shapes=[
            pltpu.VMEM((tm + 2 * halo, D_MODEL), BF16),
            pltpu.VMEM((tm, D_MODEL), F32),
        ],
        compiler_params=pltpu.CompilerParams(
            dimension_semantics=("parallel", "arbitrary"), vmem_limit_bytes=VMEM_LIMIT),
        name="conv_ffn",
    )(x, x, x, mod, g2, w_up, w_up, conv_w, conv_w, conv_b, conv_b, w_down)


def _reorder_w_in(w):
    dt0 = P_XBC + CONV_DIM
    ga0 = dt0 + 2 * SSD_HEADS
    dt = w[:, dt0:ga0]
    hg = HEADS_PER_GROUP
    blocks = []
    for g in range(SSD_GROUPS):
        blocks += [dt[:, g * hg:(g + 1) * hg], dt[:, SSD_HEADS + g * hg:SSD_HEADS + (g + 1) * hg],
                   jnp.zeros((D_MODEL, LANE - 2 * hg), w.dtype)]
    return jnp.concatenate([w[:, :dt0], w[:, ga0:]] + blocks, axis=1).astype(BF16)


def _by_group(v):
    hg = HEADS_PER_GROUP
    v = v.astype(F32).reshape(2, SSD_GROUPS, hg)
    both = jnp.concatenate([v[0], v[1]], axis=1)
    return jnp.pad(both, ((0, 0), (0, LANE - 2 * hg))).reshape(SSD_GROUPS, 1, LANE)


def kernel(x_prompt, x_sample, c, cache_k, cache_v, state_ssd_fwd, state_ssd_bwd, c_ctx, w_ada, b_ada, norm1_g, w_in, q_norm_g, k_norm_g, rpb, ssd_conv_w, ssd_conv_b, a_log, dt_bias, d_skip, ssd_norm_g, w_na_out, w_ssd_out, w_o, norm2_g, w_up, ffn_conv_w, ffn_conv_b, w_down):
    x = jnp.concatenate([x_prompt.reshape(N_CTX, D_MODEL), x_sample.reshape(N_LAT, D_MODEL)], axis=0)
    cond = jnp.concatenate([c_ctx[None, :], c, jnp.zeros((16 - 1 - DEC_BATCH, D_MODEL), F32)], axis=0)
    mod = _modulation(cond, w_ada, b_ada.reshape(DEPTH, 1, 6 * D_MODEL))
    mod = mod.reshape(DEPTH, 16, 1, 6 * D_MODEL)

    ck = cache_k.reshape(DEC_BATCH, DEPTH, PAST_LEN, NA_WIDTH)
    cv = cache_v.reshape(DEC_BATCH, DEPTH, PAST_LEN, NA_WIDTH)
    h0f = state_ssd_fwd.reshape(DEC_BATCH, DEPTH, SSD_INNER, D_STATE)
    h0b = state_ssd_bwd.reshape(DEC_BATCH, DEPTH, SSD_INNER, D_STATE)
    bd = (jnp.arange(MXU_N)[:, None] // HEAD_DIM == jnp.arange(MXU_N)[None, :] // HEAD_DIM).astype(BF16)

    new_k, new_v, new_hf, new_hb = [], [], [], []
    for l in range(DEPTH):
        qkg = jnp.concatenate([jnp.tile(q_norm_g[l] * HEAD_DIM ** -0.5, NA_HEADS),
                               jnp.tile(k_norm_g[l], NA_HEADS)])[None, :]
        p, kv32, dt32 = _in_projection(x, mod[l], norm1_g[l][None, :], _reorder_w_in(w_in[l]), qkg, bd)
        new_k.append(kv32[:, :NA_WIDTH].reshape(BATCH, SEQ, NA_HEADS, HEAD_DIM))
        new_v.append(kv32[:, NA_WIDTH:].reshape(BATCH, SEQ, NA_HEADS, HEAD_DIM))

        na_ctx = _context_attention(p)
        na_lat = _neighbourhood_attention(p, ck, cv, _na_bias(_rpb_reversed(rpb[l])), l)

        conv_b = ssd_conv_b[l][None, :]
        dtb_g = _by_group(dt_bias[l])
        alog_g = _by_group(a_log[l])
        dsk = jnp.repeat(d_skip[l], SSD_HEADDIM)[None, :]
        ssd_ctx, hf, hb = _ssd_mixer(p, dt32, ssd_conv_w[l], conv_b, dtb_g, alog_g, dsk, context=True)
        (ssd_lat,) = _ssd_mixer(p, dt32, ssd_conv_w[l], conv_b, dtb_g, alog_g, dsk, context=False,
                                layer=l, h0f=h0f, h0b=h0b)
        new_hf.append(hf.reshape(BATCH, SSD_HEADS, SSD_HEADDIM, D_STATE))
        new_hb.append(hb.reshape(BATCH, SSD_HEADS, SSD_HEADDIM, D_STATE))

        x = _merge(x, na_ctx, na_lat, ssd_ctx, ssd_lat, p, mod[l], ssd_norm_g[l][None, :],
                   w_na_out[l].astype(BF16), w_ssd_out[l].astype(BF16), w_o[l].astype(BF16))
        x = _conv_ffn(x, mod[l], norm2_g[l][None, :], w_up[l].astype(BF16), ffn_conv_w[l],
                      ffn_conv_b[l][None, :], w_down[l].astype(BF16))

    y_prompt = x[:N_CTX].reshape(BATCH, SEQ, D_MODEL)
    y_sample = x[N_CTX:].reshape(DEC_BATCH, DEC_SEQ, D_MODEL)
    return (y_prompt, y_sample, jnp.stack(new_k, axis=1), jnp.stack(new_v, axis=1),
            jnp.stack(new_hf, axis=1), jnp.stack(new_hb, axis=1))
```

```python
import functools

import jax
import jax.numpy as jnp
from jax import lax
from jax.experimental import pallas as pl
from jax.experimental.pallas import tpu as pltpu

D_MODEL = 1024
BATCH = 16
SEQ = 256
DEPTH = 2
DEC_BATCH = 8
DEC_SEQ = 2048
PAST_LEN = 256
GRID_W = 64
NA_HEADS = 16
HEAD_DIM = 64
NA_WIDTH = NA_HEADS * HEAD_DIM
WIN_R = 8
WIN_C = 16
SSD_INNER = 2 * D_MODEL
SSD_HEADDIM = 64
SSD_HEADS = SSD_INNER // SSD_HEADDIM
SSD_GROUPS = 4
D_STATE = 128
SSD_CONV = 4
CHUNK = 128
CONV_DIM = SSD_INNER + 2 * SSD_GROUPS * D_STATE
D_FF = 2816
FFN_CONV = 3
EPS = 1e-6

N_CTX = BATCH * SEQ
N_LAT = DEC_BATCH * DEC_SEQ
N_TOK = N_CTX + N_LAT
GRID_ROWS = DEC_SEQ // GRID_W
HEADS_PER_GROUP = SSD_HEADS // SSD_GROUPS
GROUP_W = HEADS_PER_GROUP * SSD_HEADDIM

LANE = 128
SUBLANE = 8
BF16_ROWS = 16
MXU_N = 256

P_Q, P_K, P_V, P_Z, P_XBC = 0, 1024, 2048, 3072, 5120
P_GA, P_GB, P_DT = 8192, 9216, 10240
DT_W = SSD_GROUPS * LANE
P_W = P_DT + DT_W
IN_TILE_N = 2 * MXU_N
N_IN_TILES = P_W // IN_TILE_N
N_QK_TILES = (P_V - P_Q) // IN_TILE_N
N_QKV_TILES = (P_Z - P_Q) // IN_TILE_N
DT_TILE = P_DT // IN_TILE_N

TM_IN = 1024
TM_MERGE = 512
TM_FFN = 512
FF_TILE = 256
MASK_NEG = -1e30
EXPAND_TERMS = 1

NA_QROWS = 4
NA_KROWS = 12
NA_Q = NA_QROWS * GRID_W
NA_KW = NA_KROWS * GRID_W
NA_KBLK = 256
NA_STEPS = GRID_ROWS // NA_QROWS
NA_VARIANTS = 3

F32 = jnp.float32
BF16 = jnp.bfloat16
NT = (((1,), (1,)), ((), ()))

VMEM_LIMIT = 56 * 1024 * 1024


def _sigmoid(x):
    return 0.5 + 0.5 * jnp.tanh(0.5 * x)


def _silu(x):
    h = 0.5 * x
    return h + h * jnp.tanh(h)


def _dot(a, b):
    return jnp.dot(a, b, preferred_element_type=F32)


def _dot_nt(a, b):
    return lax.dot_general(a, b, NT, preferred_element_type=F32)


def _split_dot(a, b, terms):
    acc = None
    r = a
    for t in range(terms):
        p = r.astype(BF16)
        d = _dot(p, b)
        acc = d if acc is None else acc + d
        if t + 1 < terms:
            r = r - p.astype(F32)
    return acc


def _mod_row(i, tm):
    start = i * tm
    return jnp.where(start < N_CTX, 0, 1 + (start - N_CTX) // DEC_SEQ)


def _mod_kernel(c_ref, w_ref, b_ref, o_ref):
    c = c_ref[...]
    s = _silu(c).astype(BF16)
    o_ref[...] = _dot(s, w_ref[...].astype(BF16)) + b_ref[...]


def _modulation(cond, w_ada, b_ada):
    tn = 1536
    return pl.pallas_call(
        _mod_kernel,
        out_shape=jax.ShapeDtypeStruct((DEPTH, 16, 6 * D_MODEL), F32),
        grid=(DEPTH, 6 * D_MODEL // tn),
        in_specs=[
            pl.BlockSpec((16, D_MODEL), lambda l, j: (0, 0)),
            pl.BlockSpec((None, D_MODEL, tn), lambda l, j: (l, 0, j)),
            pl.BlockSpec((None, 1, tn), lambda l, j: (l, 0, j)),
        ],
        out_specs=pl.BlockSpec((None, 16, tn), lambda l, j: (l, 0, j)),
        compiler_params=pltpu.CompilerParams(
            dimension_semantics=("parallel", "parallel"), vmem_limit_bytes=VMEM_LIMIT),
        name="adaln_mod",
    )(cond, w_ada, b_ada)


def _inproj_kernel(x_ref, mod_ref, g_ref, w_ref, qkg_ref, bd_ref,
                   p_ref, kv_ref, dt_ref, h_ref, *, n_ctx_tiles):
    i = pl.program_id(0)
    j = pl.program_id(1)

    @pl.when(j == 0)
    def _():
        x = x_ref[...]
        ms = jnp.mean(x * x, axis=-1, keepdims=True)
        y = x * lax.rsqrt(ms + EPS) * g_ref[...]
        shift = mod_ref[:, 0:D_MODEL]
        scale = mod_ref[:, D_MODEL:2 * D_MODEL]
        h_ref[...] = (y * (1.0 + scale) + shift).astype(BF16)

    is_ctx = i < n_ctx_tiles
    is_qk = j < N_QK_TILES
    is_dt = j == DT_TILE
    is_ctx_v = jnp.logical_and(jnp.logical_and(j >= N_QK_TILES, j < N_QKV_TILES), is_ctx)

    @pl.when(is_qk)
    def _():
        acc = _dot(h_ref[...], w_ref[...])
        sq = acc * acc
        ss = jnp.concatenate(
            [_split_dot(sq[:, t * MXU_N:(t + 1) * MXU_N], bd_ref[...], 2) for t in range(IN_TILE_N // MXU_N)],
            axis=1)
        yn = acc * lax.rsqrt(ss * (1.0 / HEAD_DIM) + EPS) * qkg_ref[...]
        p_ref[...] = yn.astype(BF16)

        @pl.when(jnp.logical_and(j >= N_QK_TILES // 2, is_ctx))
        def _():
            kv_ref[...] = yn

    @pl.when(is_ctx_v)
    def _():
        acc = _dot(h_ref[...], w_ref[...])
        p_ref[...] = acc.astype(BF16)
        kv_ref[...] = acc

    @pl.when(is_dt)
    def _():
        acc = _dot(h_ref[...], w_ref[...])
        p_ref[...] = acc.astype(BF16)
        dt_ref[...] = acc

    @pl.when(jnp.logical_not(jnp.logical_or(jnp.logical_or(is_qk, is_ctx_v), is_dt)))
    def _():
        p_ref[...] = _dot(h_ref[...], w_ref[...]).astype(BF16)


def _in_projection(x, mod, g1, w, qkg, bd):
    n_i = N_TOK // TM_IN
    n_ctx_tiles = N_CTX // TM_IN
    kv0 = N_QK_TILES // 2
    n_kv = N_QKV_TILES - kv0

    def kv_map(i, j):
        ii = jnp.minimum(i, n_ctx_tiles - 1)
        jj = jnp.where(i < n_ctx_tiles, jnp.clip(j - kv0, 0, n_kv - 1), n_kv - 1)
        return (ii, jj)

    return pl.pallas_call(
        functools.partial(_inproj_kernel, n_ctx_tiles=n_ctx_tiles),
        out_shape=(
            jax.ShapeDtypeStruct((N_TOK, P_W), BF16),
            jax.ShapeDtypeStruct((N_CTX, 2 * NA_WIDTH), F32),
            jax.ShapeDtypeStruct((N_TOK, DT_W), F32),
        ),
        grid=(n_i, N_IN_TILES),
        in_specs=[
            pl.BlockSpec((TM_IN, D_MODEL), lambda i, j: (i, 0)),
            pl.BlockSpec((None, 1, 6 * D_MODEL), lambda i, j: (_mod_row(i, TM_IN), 0, 0)),
            pl.BlockSpec((1, D_MODEL), lambda i, j: (0, 0)),
            pl.BlockSpec((D_MODEL, IN_TILE_N), lambda i, j: (0, j)),
            pl.BlockSpec((1, IN_TILE_N), lambda i, j: (0, jnp.minimum(j, N_QK_TILES - 1))),
            pl.BlockSpec((MXU_N, MXU_N), lambda i, j: (0, 0)),
        ],
        out_specs=(
            pl.BlockSpec((TM_IN, IN_TILE_N), lambda i, j: (i, j)),
            pl.BlockSpec((TM_IN, IN_TILE_N), kv_map),
            pl.BlockSpec((TM_IN, DT_W), lambda i, j: (i, 0)),
        ),
        scratch_shapes=[pltpu.VMEM((TM_IN, D_MODEL), BF16)],
        compiler_params=pltpu.CompilerParams(
            dimension_semantics=("arbitrary", "arbitrary"), vmem_limit_bytes=VMEM_LIMIT),
        name="in_proj",
    )(x, mod, g1, w, qkg, bd)


def _head_masks():
    lane = lax.broadcasted_iota(jnp.int32, (1, LANE), 1)
    return lane < HEAD_DIM


def _ctx_attn_kernel(q_ref, k_ref, v_ref, o_ref):
    q = q_ref[...]
    k = k_ref[...]
    v = v_ref[...]
    lo = _head_masks()
    outs = []
    for hh in range(2):
        m = lo if hh == 0 else jnp.logical_not(lo)
        qm = jnp.where(m, q, jnp.zeros_like(q))
        s = _dot_nt(qm, k)
        mx = jnp.max(s, axis=-1, keepdims=True)
        p = jnp.exp(s - mx)
        l = jnp.sum(p, axis=-1, keepdims=True)
        outs.append(_dot(p.astype(BF16), v) / l)
    o_ref[...] = jnp.where(lo, outs[0], outs[1]).astype(BF16)


def _context_attention(p):
    hp = NA_HEADS // 2
    blk = (SEQ, LANE)
    return pl.pallas_call(
        _ctx_attn_kernel,
        out_shape=jax.ShapeDtypeStruct((N_CTX, NA_WIDTH), BF16),
        grid=(BATCH, hp),
        in_specs=[
            pl.BlockSpec(blk, lambda b, h: (b, P_Q // LANE + h)),
            pl.BlockSpec(blk, lambda b, h: (b, P_K // LANE + h)),
            pl.BlockSpec(blk, lambda b, h: (b, P_V // LANE + h)),
        ],
        out_specs=pl.BlockSpec(blk, lambda b, h: (b, h)),
        compiler_params=pltpu.CompilerParams(
            dimension_semantics=("parallel", "parallel"), vmem_limit_bytes=VMEM_LIMIT),
        name="ctx_attn",
    )(p, p, p)


def _na_bias_kernel(r_ref, o_ref):
    ck = lax.broadcasted_iota(jnp.int32, (GRID_W, LANE), 0)
    ln = lax.broadcasted_iota(jnp.int32, (GRID_W, LANE), 1)
    cq = ln & (GRID_W - 1)
    cs = jnp.clip(cq - WIN_C // 2, 0, GRID_W - WIN_C)
    col_ok = jnp.logical_and(ck >= cs, ck < cs + WIN_C)
    left = ln < GRID_W
    n_dr = 2 * WIN_R - 1
    toep = []
    for dr in range(n_dr):
        base = jnp.broadcast_to(r_ref[dr:dr + 1, :], (GRID_W, LANE))
        toep.append([pltpu.roll(base, s * GRID_W, 1, stride=1, stride_axis=0) for s in range(2)])
    neg = jnp.full((GRID_W, LANE), MASK_NEG, F32)

    def rel(v, j, q4):
        if v == 0:
            return j - q4 + WIN_R - 1, j < WIN_R
        if v == 1:
            return j - q4 + WIN_R // 2 - 1, q4 <= j < q4 + WIN_R
        return j - q4 - 1, j >= NA_KROWS - WIN_R

    for v in range(NA_VARIANTS):
        for j in range(NA_KROWS):
            for lb in range(NA_QROWS // 2):
                halves = []
                for s in range(2):
                    dr, ok = rel(v, j, 2 * lb + s)
                    halves.append(jnp.where(col_ok, toep[dr][s], neg) if ok else neg)
                o_ref[v, j * GRID_W:(j + 1) * GRID_W, lb * LANE:(lb + 1) * LANE] = (
                    jnp.where(left, halves[0], halves[1]))


def _na_bias(rrev):
    return pl.pallas_call(
        _na_bias_kernel,
        out_shape=jax.ShapeDtypeStruct((NA_HEADS // 2, NA_VARIANTS, NA_KW, 2 * NA_Q), F32),
        grid=(NA_HEADS,),
        in_specs=[pl.BlockSpec((None, 2 * WIN_R, LANE), lambda h: (h, 0, 0))],
        out_specs=pl.BlockSpec((None, NA_VARIANTS, NA_KW, NA_Q), lambda h: (h // 2, 0, 0, h % 2)),
        compiler_params=pltpu.CompilerParams(
            dimension_semantics=("parallel",), vmem_limit_bytes=VMEM_LIMIT),
        name="na_bias",
    )(rrev)


def _na_kernel(q_ref, k_ref, v_ref, kc_ref, vc_ref, bias_ref, o_ref, vt_ref):
    lo = _head_masks()
    kc = kc_ref[...].astype(BF16)
    vct = vc_ref[...].T.astype(BF16)
    for t in range(DEC_SEQ // NA_KBLK):
        vt_ref[t] = v_ref[t * NA_KBLK:(t + 1) * NA_KBLK, :].astype(F32).T.astype(BF16)
    n_wblk = NA_KW // NA_KBLK

    for m in range(NA_STEPS):
        blk0 = min(max(m - 1, 0), DEC_SEQ // NA_KBLK - n_wblk)
        var = 0 if m == 0 else (2 if m == NA_STEPS - 1 else 1)
        q0 = m * NA_Q
        k0 = blk0 * NA_KBLK
        qg = q_ref[q0:q0 + NA_Q, :]
        kw = k_ref[k0:k0 + NA_KW, :]
        zq = jnp.zeros_like(qg)
        qcat = jnp.concatenate([jnp.where(lo, qg, zq), jnp.where(lo, zq, qg)], axis=0)
        sw = _dot_nt(kw, qcat) + bias_ref[var]
        sc = _dot_nt(kc, qcat)
        mx = jnp.maximum(jnp.max(sw, axis=0, keepdims=True), jnp.max(sc, axis=0, keepdims=True))
        pw = jnp.exp(sw - mx)
        pc = jnp.exp(sc - mx)
        l = jnp.sum(pw, axis=0, keepdims=True) + jnp.sum(pc, axis=0, keepdims=True)
        pwb = pw.astype(BF16)
        o = _dot(vct, pc.astype(BF16))
        for t in range(n_wblk):
            o = o + _dot(vt_ref[blk0 + t], pwb[t * NA_KBLK:(t + 1) * NA_KBLK, :])
        o = o / l
        both = jnp.concatenate([o[0:HEAD_DIM, 0:NA_Q], o[HEAD_DIM:2 * HEAD_DIM, NA_Q:2 * NA_Q]], axis=0)
        o_ref[q0:q0 + NA_Q, :] = both.T.astype(BF16)


def _neighbourhood_attention(p, cache_k, cache_v, bias, layer):
    hp = NA_HEADS // 2
    row0 = N_CTX // DEC_SEQ
    blk = (DEC_SEQ, LANE)
    cblk = (None, None, PAST_LEN, LANE)
    return pl.pallas_call(
        _na_kernel,
        out_shape=jax.ShapeDtypeStruct((N_LAT, NA_WIDTH), BF16),
        grid=(hp, DEC_BATCH),
        in_specs=[
            pl.BlockSpec(blk, lambda h, b: (row0 + b, P_Q // LANE + h)),
            pl.BlockSpec(blk, lambda h, b: (row0 + b, P_K // LANE + h)),
            pl.BlockSpec(blk, lambda h, b: (row0 + b, P_V // LANE + h)),
            pl.BlockSpec(cblk, lambda h, b: (b, layer, 0, h)),
            pl.BlockSpec(cblk, lambda h, b: (b, layer, 0, h)),
            pl.BlockSpec((None, NA_VARIANTS, NA_KW, 2 * NA_Q), lambda h, b: (h, 0, 0, 0)),
        ],
        out_specs=pl.BlockSpec(blk, lambda h, b: (b, h)),
        scratch_shapes=[pltpu.VMEM((DEC_SEQ // NA_KBLK, LANE, NA_KBLK), BF16)],
        compiler_params=pltpu.CompilerParams(
            dimension_semantics=("parallel", "parallel"), vmem_limit_bytes=VMEM_LIMIT),
        name="na_attn",
    )(p, p, p, cache_k, cache_v, bias)


def _rpb_reversed(rpb_l):
    n = WIN_C - 1
    pos = rpb_l[:, :, n::-1]
    neg = rpb_l[:, :, :n:-1]
    z = jnp.zeros(rpb_l.shape[:2] + (LANE - 2 * n - 1,), F32)
    r = jnp.concatenate([pos.astype(F32), z, neg.astype(F32)], axis=-1)
    return jnp.pad(r, ((0, 0), (0, 1), (0, 0)))


def _ssd_kernel(*refs, seq_len, has_h0, emit_state):
    it = iter(refs)
    x_ref, b_ref, c_ref, z_ref, dt_ref = (next(it) for _ in range(5))
    cwx_ref, cwb_ref, cwc_ref, cbx_ref, cbb_ref, cbc_ref = (next(it) for _ in range(6))
    dtb_ref, alog_ref, dsk_ref = (next(it) for _ in range(3))
    if has_h0:
        h0f_ref, h0b_ref = next(it), next(it)
    y_ref = next(it)
    if emit_state:
        hf_ref, hb_ref = next(it), next(it)
    xc_s, bt_s, c_s, cum_s, dtv_s, y_s, st_s = (next(it) for _ in range(7))

    n_chunks = seq_len // CHUNK
    n_dir = 2 * HEADS_PER_GROUP
    win = CHUNK + 2 * BF16_ROWS

    lane = lax.broadcasted_iota(jnp.int32, (1, LANE), 1)
    ii = lax.broadcasted_iota(jnp.int32, (CHUNK, CHUNK), 0)
    jj = lax.broadcasted_iota(jnp.int32, (CHUNK, CHUNK), 1)
    lower = jj < ii
    diag = jj == ii
    tril = jnp.where(jj <= ii, 1.0, 0.0).astype(BF16)
    triu = jnp.where(jj >= ii, 1.0, 0.0).astype(BF16)
    er = lax.broadcasted_iota(jnp.int32, (LANE, GROUP_W), 0)
    ec = lax.broadcasted_iota(jnp.int32, (LANE, GROUP_W), 1) // SSD_HEADDIM
    exp_f = jnp.where(er == ec, 1.0, 0.0).astype(BF16)
    exp_b = jnp.where(er == ec + HEADS_PER_GROUP, 1.0, 0.0).astype(BF16)
    lo64 = lane < SSD_HEADDIM
    is_fwd = lane < HEADS_PER_GROUP
    a_neg = -jnp.exp(alog_ref[...])
    dtb = dtb_ref[...]
    dsk = dsk_ref[...]
    row_fwd = lax.broadcasted_iota(jnp.int32, (n_dir, 1), 0) < HEADS_PER_GROUP
    pad_rows = jnp.zeros((LANE - n_dir, CHUNK), F32)

    tok = lax.broadcasted_iota(jnp.int32, (CHUNK, 1), 0)

    def conv_silu(src_ref, r0, w_ref, bias_ref):
        w0 = min(max(r0 - BF16_ROWS, 0), seq_len - win)
        off = r0 - w0
        v = src_ref[w0:w0 + win, :].astype(F32)
        back2 = pltpu.roll(v, 2, 0)[off:off + CHUNK]
        back1 = pltpu.roll(v, 1, 0)[off:off + CHUNK]
        ahead = pltpu.roll(v, win - 1, 0)[off:off + CHUNK]
        if r0 == 0:
            back2 = jnp.where(tok >= 2, back2, 0.0)
            back1 = jnp.where(tok >= 1, back1, 0.0)
        if r0 == seq_len - CHUNK:
            ahead = jnp.where(tok < CHUNK - 1, ahead, 0.0)
        w = w_ref[...]
        u = (w[0:1] * back2 + w[1:2] * back1 + w[2:3] * v[off:off + CHUNK] + w[3:4] * ahead
             + bias_ref[...])
        return _silu(u)

    if has_h0:
        st_s[...] = h0f_ref[...].T
    else:
        st_s[...] = jnp.zeros((D_STATE, GROUP_W), F32)

    def fwd(c):
        r0 = c * CHUNK
        xc = conv_silu(x_ref, r0, cwx_ref, cbx_ref)
        bc = conv_silu(b_ref, r0, cwb_ref, cbb_ref)
        cc = conv_silu(c_ref, r0, cwc_ref, cbc_ref)
        xcb = xc.astype(BF16)
        bcb = bc.astype(BF16)
        ccb = cc.astype(BF16)
        btb = bc.T.astype(BF16)
        xc_s[pl.ds(r0, CHUNK), :] = xcb
        bt_s[pl.ds(r0, CHUNK), :] = btb
        c_s[pl.ds(r0, CHUNK), :] = ccb

        raw_dt = dt_ref[pl.ds(r0, CHUNK), :].T[0:n_dir, :] + dtb
        e = jnp.exp(-jnp.abs(raw_dt))
        log1p_e = jnp.where(e < 1e-4, e * (1.0 - 0.5 * e), jnp.log(1.0 + e))
        dt_t = jnp.maximum(raw_dt, 0.0) + log1p_e
        a = dt_t * a_neg
        cum_t = jnp.where(row_fwd, _split_dot(a, triu, 3), _split_dot(a, tril, 3))
        cum = jnp.concatenate([cum_t, pad_rows], axis=0).T
        dtv = jnp.concatenate([dt_t, pad_rows], axis=0).T
        cum_s[pl.ds(r0, CHUNK), :] = cum
        dtv_s[pl.ds(r0, CHUNK), :] = dtv

        g = _dot_nt(ccb, bcb)
        parts = []
        for pp in range(HEADS_PER_GROUP // 2):
            ws = []
            for hh in (2 * pp, 2 * pp + 1):
                hb = HEADS_PER_GROUP + hh
                seg = jnp.where(lower, cum[:, hh:hh + 1] - cum_t[hh:hh + 1, :],
                                cum[:, hb:hb + 1] - cum_t[hb:hb + 1, :])
                d_f = dt_t[hh:hh + 1, :]
                dtm = jnp.where(lower, d_f, dt_t[hb:hb + 1, :]) + jnp.where(diag, d_f, 0.0)
                ws.append((g * jnp.exp(seg) * dtm).astype(BF16))
            wp = jnp.concatenate(ws, axis=1)
            xp = xcb[:, pp * LANE:(pp + 1) * LANE]
            zero = jnp.zeros_like(xp)
            rhs = jnp.concatenate([jnp.where(lo64, xp, zero), jnp.where(lo64, zero, xp)], axis=0)
            parts.append(_dot(wp, rhs))
        y = jnp.concatenate(parts, axis=1)

        ef_x = _split_dot(jnp.exp(cum), exp_f, EXPAND_TERMS)
        to_end = jnp.where(is_fwd, cum[CHUNK - 1:CHUNK, :] - cum, 0.0)
        ff_x = _split_dot(jnp.exp(to_end) * dtv, exp_f, EXPAND_TERMS)
        st = st_s[...]
        y = y + _dot(ccb, st.astype(BF16)) * ef_x + dsk * xc
        st_s[...] = st * ef_x[CHUNK - 1:CHUNK, :] + _dot(btb, (xc * ff_x).astype(BF16))
        y_s[pl.ds(r0, CHUNK), :] = y

    for c in range(n_chunks):
        fwd(c)

    if emit_state:
        hf_ref[...] = st_s[...].T
    if has_h0:
        st_s[...] = h0b_ref[...].T
    else:
        st_s[...] = jnp.zeros((D_STATE, GROUP_W), F32)

    def bwd(c):
        r0 = c * CHUNK
        cum = cum_s[pl.ds(r0, CHUNK), :]
        dtv = dtv_s[pl.ds(r0, CHUNK), :]
        eb_x = _split_dot(jnp.exp(cum), exp_b, EXPAND_TERMS)
        to_start = jnp.where(is_fwd, 0.0, cum[0:1, :] - cum)
        fb_x = _split_dot(jnp.exp(to_start) * dtv, exp_b, EXPAND_TERMS)
        xc = xc_s[pl.ds(r0, CHUNK), :].astype(F32)
        st = st_s[...]
        y = y_s[pl.ds(r0, CHUNK), :] + _dot(c_s[pl.ds(r0, CHUNK), :], st.astype(BF16)) * eb_x
        st_s[...] = st * eb_x[0:1, :] + _dot(bt_s[pl.ds(r0, CHUNK), :], (xc * fb_x).astype(BF16))
        z = z_ref[pl.ds(r0, CHUNK), :].astype(F32)
        y_ref[pl.ds(r0, CHUNK), :] = (y * _silu(z)).astype(BF16)

    for c in reversed(range(n_chunks)):
        bwd(c)

    if emit_state:
        hb_ref[...] = st_s[...].T


def _ssd_mixer(p, dt32, conv_w, conv_b, dtb_g, alog_g, dsk, *, context, layer=0, h0f=None, h0b=None):
    seq_len = SEQ if context else DEC_SEQ
    n_seq = BATCH if context else DEC_BATCH
    row0 = 0 if context else N_CTX // DEC_SEQ
    has_h0 = h0f is not None
    emit_state = context
    xw_, bw_ = GROUP_W, D_STATE
    x_cb = P_XBC // xw_
    b_cb = (P_XBC + SSD_INNER) // bw_
    c_cb = b_cb + SSD_GROUPS
    z_cb = P_Z // xw_

    in_specs = [
        pl.BlockSpec((seq_len, xw_), lambda b, g: (row0 + b, x_cb + g)),
        pl.BlockSpec((seq_len, bw_), lambda b, g: (row0 + b, b_cb + g)),
        pl.BlockSpec((seq_len, bw_), lambda b, g: (row0 + b, c_cb + g)),
        pl.BlockSpec((seq_len, xw_), lambda b, g: (row0 + b, z_cb + g)),
        pl.BlockSpec((seq_len, LANE), lambda b, g: (row0 + b, g)),
        pl.BlockSpec((SSD_CONV, xw_), lambda b, g: (0, g)),
        pl.BlockSpec((SSD_CONV, bw_), lambda b, g: (0, SSD_INNER // bw_ + g)),
        pl.BlockSpec((SSD_CONV, bw_), lambda b, g: (0, SSD_INNER // bw_ + SSD_GROUPS + g)),
        pl.BlockSpec((1, xw_), lambda b, g: (0, g)),
        pl.BlockSpec((1, bw_), lambda b, g: (0, SSD_INNER // bw_ + g)),
        pl.BlockSpec((1, bw_), lambda b, g: (0, SSD_INNER // bw_ + SSD_GROUPS + g)),
        pl.BlockSpec((None, 2 * HEADS_PER_GROUP, 1), lambda b, g: (g, 0, 0)),
        pl.BlockSpec((None, 2 * HEADS_PER_GROUP, 1), lambda b, g: (g, 0, 0)),
        pl.BlockSpec((1, xw_), lambda b, g: (0, g)),
    ]
    args = [p, p, p, p, dt32, conv_w, conv_w, conv_w, conv_b, conv_b, conv_b, dtb_g, alog_g, dsk]
    if has_h0:
        st_spec = pl.BlockSpec((None, None, xw_, D_STATE), lambda b, g: (b, layer, g, 0))
        in_specs += [st_spec, st_spec]
        args += [h0f, h0b]
    out_shape = [jax.ShapeDtypeStruct((n_seq * seq_len, SSD_INNER), BF16)]
    out_specs = [pl.BlockSpec((seq_len, xw_), lambda b, g: (b, g))]
    if emit_state:
        so = pl.BlockSpec((None, xw_, D_STATE), lambda b, g: (b, g, 0))
        out_shape += [jax.ShapeDtypeStruct((n_seq, SSD_INNER, D_STATE), F32)] * 2
        out_specs += [so, so]
    scratch = [
        pltpu.VMEM((seq_len, xw_), BF16),
        pltpu.VMEM((seq_len, bw_), BF16),
        pltpu.VMEM((seq_len, bw_), BF16),
        pltpu.VMEM((seq_len, LANE), F32),
        pltpu.VMEM((seq_len, LANE), F32),
        pltpu.VMEM((seq_len, xw_), F32),
        pltpu.VMEM((D_STATE, xw_), F32),
    ]
    return pl.pallas_call(
        functools.partial(_ssd_kernel, seq_len=seq_len, has_h0=has_h0, emit_state=emit_state),
        out_shape=tuple(out_shape),
        grid=(n_seq, SSD_GROUPS),
        in_specs=in_specs,
        out_specs=tuple(out_specs),
        scratch_shapes=scratch,
        compiler_params=pltpu.CompilerParams(
            dimension_semantics=("parallel", "parallel"), vmem_limit_bytes=VMEM_LIMIT),
        name="ssd_ctx" if context else "ssd_lat",
    )(*args)


def _merge_kernel(x_ref, nac_ref, nal_ref, sc_ref, sl_ref, ga_ref, gb_ref, mod_ref, sg_ref,
                  wna_ref, wssd_ref, wo_ref, o_ref, *, n_ctx_tiles):
    is_ctx = pl.program_id(0) < n_ctx_tiles
    na = jnp.where(is_ctx, nac_ref[...], nal_ref[...])
    y = jnp.where(is_ctx, sc_ref[...], sl_ref[...]).astype(F32)
    ms = jnp.mean(y * y, axis=-1, keepdims=True)
    yn = (y * lax.rsqrt(ms + EPS) * sg_ref[...]).astype(BF16)
    a = _dot(na, wna_ref[...])
    s = _dot(yn, wssd_ref[...])
    mix = _sigmoid(ga_ref[...].astype(F32)) * a + _sigmoid(gb_ref[...].astype(F32)) * s
    o = _dot(mix.astype(BF16), wo_ref[...])
    gate = mod_ref[:, 2 * D_MODEL:3 * D_MODEL]
    o_ref[...] = x_ref[...] + gate * o


def _merge(x, na_ctx, na_lat, ssd_ctx, ssd_lat, p, mod, ssd_g, w_na, w_ssd, w_o):
    tm = TM_MERGE
    n_ctx_tiles = N_CTX // tm
    ctx_i = lambda i: (jnp.minimum(i, n_ctx_tiles - 1), 0)
    lat_i = lambda i: (jnp.maximum(i - n_ctx_tiles, 0), 0)
    const = lambda i: (0, 0)
    return pl.pallas_call(
        functools.partial(_merge_kernel, n_ctx_tiles=n_ctx_tiles),
        out_shape=jax.ShapeDtypeStruct((N_TOK, D_MODEL), F32),
        grid=(N_TOK // tm,),
        in_specs=[
            pl.BlockSpec((tm, D_MODEL), lambda i: (i, 0)),
            pl.BlockSpec((tm, NA_WIDTH), ctx_i),
            pl.BlockSpec((tm, NA_WIDTH), lat_i),
            pl.BlockSpec((tm, SSD_INNER), ctx_i),
            pl.BlockSpec((tm, SSD_INNER), lat_i),
            pl.BlockSpec((tm, D_MODEL), lambda i: (i, P_GA // D_MODEL)),
            pl.BlockSpec((tm, D_MODEL), lambda i: (i, P_GB // D_MODEL)),
            pl.BlockSpec((None, 1, 6 * D_MODEL), lambda i: (_mod_row(i, tm), 0, 0)),
            pl.BlockSpec((1, SSD_INNER), const),
            pl.BlockSpec((NA_WIDTH, D_MODEL), const),
            pl.BlockSpec((SSD_INNER, D_MODEL), const),
            pl.BlockSpec((D_MODEL, D_MODEL), const),
        ],
        out_specs=pl.BlockSpec((tm, D_MODEL), lambda i: (i, 0)),
        compiler_params=pltpu.CompilerParams(
            dimension_semantics=("parallel",), vmem_limit_bytes=VMEM_LIMIT),
        name="merge",
    )(x, na_ctx, na_lat, ssd_ctx, ssd_lat, p, p, mod, ssd_g, w_na, w_ssd, w_o)


def _ffn_kernel(x_ref, xp_ref, xn_ref, mod_ref, g_ref, wup_ref, cw_ref, cb_ref, wd_ref,
                o_ref, h_ref, act_ref, *, tm):
    i = pl.program_id(0)
    n_ff = D_FF // FF_TILE
    shift = mod_ref[:, 3 * D_MODEL:4 * D_MODEL]
    scale = mod_ref[:, 4 * D_MODEL:5 * D_MODEL]
    g = g_ref[...]

    def norm_mod(xv):
        ms = jnp.mean(xv * xv, axis=-1, keepdims=True)
        return (xv * lax.rsqrt(ms + EPS) * g * (1.0 + scale) + shift).astype(BF16)

    h_ref[0:tm, :] = norm_mod(x_ref[...])
    h_ref[tm:tm + 2 * SUBLANE, :] = norm_mod(jnp.concatenate([xp_ref[...], xn_ref[...]], axis=0))

    seq = jnp.where(i * tm < N_CTX, SEQ, DEC_SEQ)
    row = lax.broadcasted_iota(jnp.int32, (tm, 1), 0)
    pos = (i * tm + row) & (seq - 1)
    has_prev = pos != 0
    has_next = pos != seq - 1
    first = row == 0
    last = row == tm - 1

    def conv(pre, t):
        cw = cw_ref[t]
        main = pre[0:tm]
        left = jnp.where(first, pre[tm + SUBLANE - 1:tm + SUBLANE], pltpu.roll(main, 1, 0))
        right = jnp.where(last, pre[tm + SUBLANE:tm + SUBLANE + 1], pltpu.roll(main, tm - 1, 0))
        return (cw[0:1] * jnp.where(has_prev, left, 0.0) + cw[1:2] * main
                + cw[2:3] * jnp.where(has_next, right, 0.0) + cb_ref[t])

    for j in range(n_ff):
        h = h_ref[...]
        val = conv(_dot(h, wup_ref[j]), j)
        gate = conv(_dot(h, wup_ref[n_ff + j]), n_ff + j)
        act_ref[j] = (_silu(gate) * val).astype(BF16)

    acc = _dot(act_ref[0], wd_ref[0])
    for j in range(1, n_ff):
        acc = acc + _dot(act_ref[j], wd_ref[j])
    o_ref[...] = x_ref[...] + mod_ref[:, 5 * D_MODEL:6 * D_MODEL] * acc


def _conv_ffn(x, mod, g2, w_up_t, conv_w_t, conv_b_t, w_down_t):
    tm = TM_FFN
    n_ff = D_FF // FF_TILE
    hb = tm // SUBLANE
    last_h = N_TOK // SUBLANE - 1
    resident = lambda shape: pl.BlockSpec(shape, lambda i: (0,) * len(shape), pipeline_mode=pl.Buffered(1))
    return pl.pallas_call(
        functools.partial(_ffn_kernel, tm=tm),
        out_shape=jax.ShapeDtypeStruct((N_TOK, D_MODEL), F32),
        grid=(N_TOK // tm,),
        in_specs=[
            pl.BlockSpec((tm, D_MODEL), lambda i: (i, 0)),
            pl.BlockSpec((SUBLANE, D_MODEL), lambda i: (jnp.maximum(i * hb - 1, 0), 0)),
            pl.BlockSpec((SUBLANE, D_MODEL), lambda i: (jnp.minimum((i + 1) * hb, last_h), 0)),
            pl.BlockSpec((None, 1, 6 * D_MODEL), lambda i: (_mod_row(i, tm), 0, 0)),
            pl.BlockSpec((1, D_MODEL), lambda i: (0, 0)),
            resident((2 * n_ff, D_MODEL, FF_TILE)),
            resident((2 * n_ff, FFN_CONV, FF_TILE)),
            resident((2 * n_ff, 1, FF_TILE)),
            resident((n_ff, FF_TILE, D_MODEL)),
        ],
        out_specs=pl.BlockSpec((tm, D_MODEL), lambda i: (i, 0)),
        scratch_shapes=[
            pltpu.VMEM((tm + 2 * SUBLANE, D_MODEL), BF16),
            pltpu.VMEM((n_ff, tm, FF_TILE), BF16),
        ],
        compiler_params=pltpu.CompilerParams(
            dimension_semantics=("parallel",), vmem_limit_bytes=VMEM_LIMIT),
        name="conv_ffn",
    )(x, x, x, mod, g2, w_up_t, conv_w_t, conv_b_t, w_down_t)


def _ffn_tiles(a):
    r = a.shape[0]
    return jnp.transpose(a.reshape(r, 2 * D_FF // FF_TILE, FF_TILE), (1, 0, 2))


def _reorder_w_in(w):
    dt0 = P_XBC + CONV_DIM
    ga0 = dt0 + 2 * SSD_HEADS
    dt = w[:, dt0:ga0]
    hg = HEADS_PER_GROUP
    blocks = []
    for g in range(SSD_GROUPS):
        blocks += [dt[:, g * hg:(g + 1) * hg], dt[:, SSD_HEADS + g * hg:SSD_HEADS + (g + 1) * hg],
                   jnp.zeros((D_MODEL, LANE - 2 * hg), w.dtype)]
    return jnp.concatenate([w[:, :dt0], w[:, ga0:]] + blocks, axis=1).astype(BF16)


def _by_group(v):
    v = v.astype(F32).reshape(2, SSD_GROUPS, HEADS_PER_GROUP)
    return jnp.concatenate([v[0], v[1]], axis=1)[:, :, None]


def kernel(x_prompt, x_sample, c, cache_k, cache_v, state_ssd_fwd, state_ssd_bwd, c_ctx, w_ada, b_ada, norm1_g, w_in, q_norm_g, k_norm_g, rpb, ssd_conv_w, ssd_conv_b, a_log, dt_bias, d_skip, ssd_norm_g, w_na_out, w_ssd_out, w_o, norm2_g, w_up, ffn_conv_w, ffn_conv_b, w_down):
    x = jnp.concatenate([x_prompt.reshape(N_CTX, D_MODEL), x_sample.reshape(N_LAT, D_MODEL)], axis=0)
    cond = jnp.concatenate([c_ctx[None, :], c, jnp.zeros((16 - 1 - DEC_BATCH, D_MODEL), F32)], axis=0)
    mod = _modulation(cond, w_ada, b_ada.reshape(DEPTH, 1, 6 * D_MODEL))
    mod = mod.reshape(DEPTH, 16, 1, 6 * D_MODEL)

    ck = cache_k.reshape(DEC_BATCH, DEPTH, PAST_LEN, NA_WIDTH)
    cv = cache_v.reshape(DEC_BATCH, DEPTH, PAST_LEN, NA_WIDTH)
    h0f = state_ssd_fwd.reshape(DEC_BATCH, DEPTH, SSD_INNER, D_STATE)
    h0b = state_ssd_bwd.reshape(DEC_BATCH, DEPTH, SSD_INNER, D_STATE)
    bd = (jnp.arange(MXU_N)[:, None] // HEAD_DIM == jnp.arange(MXU_N)[None, :] // HEAD_DIM).astype(BF16)

    new_k, new_v, new_hf, new_hb = [], [], [], []
    for l in range(DEPTH):
        qkg = jnp.concatenate([jnp.tile(q_norm_g[l] * HEAD_DIM ** -0.5, NA_HEADS),
                               jnp.tile(k_norm_g[l], NA_HEADS)])[None, :]
        p, kv32, dt32 = _in_projection(x, mod[l], norm1_g[l][None, :], _reorder_w_in(w_in[l]), qkg, bd)
        new_k.append(kv32[:, :NA_WIDTH].reshape(BATCH, SEQ, NA_HEADS, HEAD_DIM))
        new_v.append(kv32[:, NA_WIDTH:].reshape(BATCH, SEQ, NA_HEADS, HEAD_DIM))

        na_ctx = _context_attention(p)
        na_lat = _neighbourhood_attention(p, ck, cv, _na_bias(_rpb_reversed(rpb[l])), l)

        conv_b = ssd_conv_b[l][None, :]
        dtb_g = _by_group(dt_bias[l])
        alog_g = _by_group(a_log[l])
        dsk = jnp.repeat(d_skip[l], SSD_HEADDIM)[None, :]
        ssd_ctx, hf, hb = _ssd_mixer(p, dt32, ssd_conv_w[l], conv_b, dtb_g, alog_g, dsk, context=True)
        (ssd_lat,) = _ssd_mixer(p, dt32, ssd_conv_w[l], conv_b, dtb_g, alog_g, dsk, context=False,
                                layer=l, h0f=h0f, h0b=h0b)
        new_hf.append(hf.reshape(BATCH, SSD_HEADS, SSD_HEADDIM, D_STATE))
        new_hb.append(hb.reshape(BATCH, SSD_HEADS, SSD_HEADDIM, D_STATE))

        x = _merge(x, na_ctx, na_lat, ssd_ctx, ssd_lat, p, mod[l], ssd_norm_g[l][None, :],
                   w_na_out[l].astype(BF16), w_ssd_out[l].astype(BF16), w_o[l].astype(BF16))
        x = _conv_ffn(x, mod[l], norm2_g[l][None, :], _ffn_tiles(w_up[l].astype(BF16)),
                      _ffn_tiles(ffn_conv_w[l]), _ffn_tiles(ffn_conv_b[l][None, :]),
                      w_down[l].astype(BF16).reshape(D_FF // FF_TILE, FF_TILE, D_MODEL))

    y_prompt = x[:N_CTX].reshape(BATCH, SEQ, D_MODEL)
    y_sample = x[N_CTX:].reshape(DEC_BATCH, DEC_SEQ, D_MODEL)
    return (y_prompt, y_sample, jnp.stack(new_k, axis=1), jnp.stack(new_v, axis=1),
            jnp.stack(new_hf, axis=1), jnp.stack(new_hb, axis=1))
```

```python
import functools

import jax
import jax.numpy as jnp
from jax import lax
from jax.experimental import pallas as pl
from jax.experimental.pallas import tpu as pltpu

D_MODEL = 1024
BATCH = 16
SEQ = 256
DEPTH = 2
DEC_BATCH = 8
DEC_SEQ = 2048
PAST_LEN = 256
GRID_W = 64
NA_HEADS = 16
HEAD_DIM = 64
NA_WIDTH = NA_HEADS * HEAD_DIM
WIN_R = 8
WIN_C = 16
SSD_INNER = 2 * D_MODEL
SSD_HEADDIM = 64
SSD_HEADS = SSD_INNER // SSD_HEADDIM
SSD_GROUPS = 4
D_STATE = 128
SSD_CONV = 4
CHUNK = 128
CONV_DIM = SSD_INNER + 2 * SSD_GROUPS * D_STATE
D_FF = 2816
FFN_CONV = 3
EPS = 1e-6

N_CTX = BATCH * SEQ
N_LAT = DEC_BATCH * DEC_SEQ
N_TOK = N_CTX + N_LAT
GRID_ROWS = DEC_SEQ // GRID_W
HEADS_PER_GROUP = SSD_HEADS // SSD_GROUPS
GROUP_W = HEADS_PER_GROUP * SSD_HEADDIM

LANE = 128
SUBLANE = 8
BF16_ROWS = 16
MXU_N = 256

A_Q, A_K, A_V, A_Z, A_W = 0, 1024, 2048, 3072, 5120
B_X, B_B, B_C, B_GA, B_GB, B_W = 0, 2048, 2560, 3072, 4096, 5120
DT_W = SSD_GROUPS * LANE
IN_TILE_N = 2 * MXU_N
TM_IN = 512
TM_MERGE = 512
TM_FFN = 512
FF_TILE = 256
MASK_NEG = -1e30
EXPAND_TERMS = 1

NA_QROWS = 4
NA_KROWS = 12
NA_Q = NA_QROWS * GRID_W
NA_KW = NA_KROWS * GRID_W
NA_KBLK = 256
NA_STEPS = GRID_ROWS // NA_QROWS
NA_VARIANTS = 3

F32 = jnp.float32
BF16 = jnp.bfloat16
NT = (((1,), (1,)), ((), ()))

VMEM_LIMIT = 56 * 1024 * 1024


def _sigmoid(x):
    return 0.5 + 0.5 * jnp.tanh(0.5 * x)


def _silu(x):
    h = 0.5 * x
    return h + h * jnp.tanh(h)


def _dot(a, b):
    return jnp.dot(a, b, preferred_element_type=F32)


def _dot_nt(a, b):
    return lax.dot_general(a, b, NT, preferred_element_type=F32)


def _split_dot(a, b, terms):
    acc = None
    r = a
    for t in range(terms):
        p = r.astype(BF16)
        d = _dot(p, b)
        acc = d if acc is None else acc + d
        if t + 1 < terms:
            r = r - p.astype(F32)
    return acc


def _mod_row(i, tm):
    start = i * tm
    return jnp.where(start < N_CTX, 0, 1 + (start - N_CTX) // DEC_SEQ)


def _resident(shape):
    return pl.BlockSpec(shape, lambda *_: (0,) * len(shape), pipeline_mode=pl.Buffered(1))


def _norm_mod(xv, g, shift, scale):
    ms = jnp.mean(xv * xv, axis=-1, keepdims=True)
    return (xv * lax.rsqrt(ms + EPS) * g * (1.0 + scale) + shift).astype(BF16)


def _mod_kernel(c_ref, w_ref, b_ref, o_ref):
    s = _silu(c_ref[...]).astype(BF16)
    o_ref[...] = _dot(s, w_ref[...].astype(BF16)) + b_ref[...]


def _modulation(cond, w_ada, b_ada):
    tn = 1536
    return pl.pallas_call(
        _mod_kernel,
        out_shape=jax.ShapeDtypeStruct((DEPTH, 16, 6 * D_MODEL), F32),
        grid=(DEPTH, 6 * D_MODEL // tn),
        in_specs=[
            pl.BlockSpec((16, D_MODEL), lambda l, j: (0, 0)),
            pl.BlockSpec((None, D_MODEL, tn), lambda l, j: (l, 0, j)),
            pl.BlockSpec((None, 1, tn), lambda l, j: (l, 0, j)),
        ],
        out_specs=pl.BlockSpec((None, 16, tn), lambda l, j: (l, 0, j)),
        compiler_params=pltpu.CompilerParams(
            dimension_semantics=("parallel", "parallel"), vmem_limit_bytes=VMEM_LIMIT),
        name="adaln_mod",
    )(cond, w_ada, b_ada)


def _inproj_a_kernel(x_ref, mod_ref, g_ref, w_ref, qkg_ref, bd_ref, *rest, emit_kv):
    if emit_kv:
        p_ref, kv_ref, h_ref = rest
    else:
        p_ref, h_ref = rest
    h_ref[...] = _norm_mod(x_ref[...], g_ref[...], mod_ref[:, 0:D_MODEL], mod_ref[:, D_MODEL:2 * D_MODEL])
    for c0 in range(0, A_W, IN_TILE_N):
        cols = slice(c0, c0 + IN_TILE_N)
        acc = _dot(h_ref[...], w_ref[:, cols])
        if c0 < A_V:
            sq = acc * acc
            ss = jnp.concatenate(
                [_split_dot(sq[:, t:t + MXU_N], bd_ref[...], 2) for t in range(0, IN_TILE_N, MXU_N)], axis=1)
            acc = acc * lax.rsqrt(ss * (1.0 / HEAD_DIM) + EPS) * qkg_ref[:, cols]
        p_ref[:, cols] = acc.astype(BF16)
        if emit_kv and A_K <= c0 < A_Z:
            kv_ref[:, c0 - A_K:c0 - A_K + IN_TILE_N] = acc


def _in_projection_a(x, mod, g1, w, qkg, bd, *, context):
    tm = TM_IN
    row0 = 0 if context else N_CTX // tm
    n_rows = N_CTX if context else N_LAT
    out_shape = [jax.ShapeDtypeStruct((n_rows, A_W), BF16)]
    out_specs = [pl.BlockSpec((tm, A_W), lambda i: (i, 0))]
    if context:
        out_shape.append(jax.ShapeDtypeStruct((n_rows, 2 * NA_WIDTH), F32))
        out_specs.append(pl.BlockSpec((tm, 2 * NA_WIDTH), lambda i: (i, 0)))
    return pl.pallas_call(
        functools.partial(_inproj_a_kernel, emit_kv=context),
        out_shape=tuple(out_shape),
        grid=(n_rows // tm,),
        in_specs=[
            pl.BlockSpec((tm, D_MODEL), lambda i: (row0 + i, 0)),
            pl.BlockSpec((None, 1, 6 * D_MODEL), lambda i: (_mod_row(row0 + i, tm), 0, 0)),
            pl.BlockSpec((1, D_MODEL), lambda i: (0, 0)),
            _resident((D_MODEL, A_W)),
            pl.BlockSpec((1, A_V), lambda i: (0, 0)),
            pl.BlockSpec((MXU_N, MXU_N), lambda i: (0, 0)),
        ],
        out_specs=tuple(out_specs),
        scratch_shapes=[pltpu.VMEM((tm, D_MODEL), BF16)],
        compiler_params=pltpu.CompilerParams(
            dimension_semantics=("parallel",), vmem_limit_bytes=VMEM_LIMIT),
        name="in_proj_a_ctx" if context else "in_proj_a_lat",
    )(x, mod, g1, w, qkg, bd)


def _inproj_b_kernel(x_ref, xp_ref, xn_ref, mod_ref, g_ref, w_ref, cw_ref, cb_ref,
                     p_ref, dt_ref, h_ref, *, tm, seq_len, row0):
    i = pl.program_id(0)
    g = g_ref[...]
    shift = mod_ref[:, 0:D_MODEL]
    scale = mod_ref[:, D_MODEL:2 * D_MODEL]
    h_ref[0:tm, :] = _norm_mod(x_ref[...], g, shift, scale)
    h_ref[tm:tm + 2 * SUBLANE, :] = _norm_mod(
        jnp.concatenate([xp_ref[...], xn_ref[...]], axis=0), g, shift, scale)

    ext = tm + 2 * SUBLANE
    start = (row0 + i) * tm
    row = lax.broadcasted_iota(jnp.int32, (tm, 1), 0)
    pos = (start + row) & (seq_len - 1)
    inner_edges = seq_len < tm
    keep_before = (start & (seq_len - 1)) != 0
    keep_after = ((start + tm) & (seq_len - 1)) != 0
    for c0 in range(0, B_GA, IN_TILE_N):
        cols = slice(c0, c0 + IN_TILE_N)
        pre = _dot(h_ref[...], w_ref[:, cols])
        before = jnp.where(keep_before, pre[tm:tm + SUBLANE], 0.0)
        after = jnp.where(keep_after, pre[tm + SUBLANE:ext], 0.0)
        v = jnp.concatenate([before, pre[0:tm], after], axis=0)
        back2 = pltpu.roll(v, 2, 0)[SUBLANE:SUBLANE + tm]
        back1 = pltpu.roll(v, 1, 0)[SUBLANE:SUBLANE + tm]
        ahead = pltpu.roll(v, ext - 1, 0)[SUBLANE:SUBLANE + tm]
        if inner_edges:
            back2 = jnp.where(pos >= 2, back2, 0.0)
            back1 = jnp.where(pos >= 1, back1, 0.0)
            ahead = jnp.where(pos != seq_len - 1, ahead, 0.0)
        cw = cw_ref[:, cols]
        u = cw[0:1] * back2 + cw[1:2] * back1 + cw[2:3] * pre[0:tm] + cw[3:4] * ahead + cb_ref[:, cols]
        p_ref[:, cols] = _silu(u).astype(BF16)
    for c0 in range(B_GA, B_W, IN_TILE_N):
        cols = slice(c0, c0 + IN_TILE_N)
        p_ref[:, cols] = _dot(h_ref[0:tm, :], w_ref[:, cols]).astype(BF16)
    dt_ref[...] = _dot(h_ref[0:tm, :], w_ref[:, B_W:B_W + DT_W])


def _in_projection_b(x, mod, g1, w, conv_w, conv_b, *, context):
    tm = TM_IN
    row0 = 0 if context else N_CTX // tm
    n_rows = N_CTX if context else N_LAT
    hb = tm // SUBLANE
    last_h = N_TOK // SUBLANE - 1
    return pl.pallas_call(
        functools.partial(_inproj_b_kernel, tm=tm, seq_len=SEQ if context else DEC_SEQ, row0=row0),
        out_shape=(jax.ShapeDtypeStruct((n_rows, B_W), BF16), jax.ShapeDtypeStruct((n_rows, DT_W), F32)),
        grid=(n_rows // tm,),
        in_specs=[
            pl.BlockSpec((tm, D_MODEL), lambda i: (row0 + i, 0)),
            pl.BlockSpec((SUBLANE, D_MODEL), lambda i: (jnp.maximum((row0 + i) * hb - 1, 0), 0)),
            pl.BlockSpec((SUBLANE, D_MODEL), lambda i: (jnp.minimum((row0 + i + 1) * hb, last_h), 0)),
            pl.BlockSpec((None, 1, 6 * D_MODEL), lambda i: (_mod_row(row0 + i, tm), 0, 0)),
            pl.BlockSpec((1, D_MODEL), lambda i: (0, 0)),
            _resident((D_MODEL, B_W + DT_W)),
            pl.BlockSpec((SSD_CONV, CONV_DIM), lambda i: (0, 0)),
            pl.BlockSpec((1, CONV_DIM), lambda i: (0, 0)),
        ],
        out_specs=(pl.BlockSpec((tm, B_W), lambda i: (i, 0)), pl.BlockSpec((tm, DT_W), lambda i: (i, 0))),
        scratch_shapes=[pltpu.VMEM((tm + 2 * SUBLANE, D_MODEL), BF16)],
        compiler_params=pltpu.CompilerParams(
            dimension_semantics=("parallel",), vmem_limit_bytes=VMEM_LIMIT),
        name="in_proj_b_ctx" if context else "in_proj_b_lat",
    )(x, x, x, mod, g1, w, conv_w, conv_b)


def _head_masks():
    lane = lax.broadcasted_iota(jnp.int32, (1, LANE), 1)
    return lane < HEAD_DIM


def _ctx_attn_kernel(q_ref, k_ref, v_ref, o_ref):
    lo = _head_masks()
    for c0 in range(0, NA_WIDTH, LANE):
        cols = slice(c0, c0 + LANE)
        q = q_ref[:, cols]
        k = k_ref[:, cols]
        v = v_ref[:, cols]
        outs = []
        for hh in range(2):
            m = lo if hh == 0 else jnp.logical_not(lo)
            qm = jnp.where(m, q, jnp.zeros_like(q))
            s = _dot_nt(qm, k)
            mx = jnp.max(s, axis=-1, keepdims=True)
            p = jnp.exp(s - mx)
            l = jnp.sum(p, axis=-1, keepdims=True)
            outs.append(_dot(p.astype(BF16), v) / l)
        o_ref[:, cols] = jnp.where(lo, outs[0], outs[1]).astype(BF16)


def _context_attention(pa):
    blk = (SEQ, NA_WIDTH)
    return pl.pallas_call(
        _ctx_attn_kernel,
        out_shape=jax.ShapeDtypeStruct((N_CTX, NA_WIDTH), BF16),
        grid=(BATCH,),
        in_specs=[
            pl.BlockSpec(blk, lambda b: (b, A_Q // NA_WIDTH)),
            pl.BlockSpec(blk, lambda b: (b, A_K // NA_WIDTH)),
            pl.BlockSpec(blk, lambda b: (b, A_V // NA_WIDTH)),
        ],
        out_specs=pl.BlockSpec(blk, lambda b: (b, 0)),
        compiler_params=pltpu.CompilerParams(
            dimension_semantics=("parallel",), vmem_limit_bytes=VMEM_LIMIT),
        name="ctx_attn",
    )(pa, pa, pa)


def _na_bias_kernel(r_ref, o_ref):
    ck = lax.broadcasted_iota(jnp.int32, (GRID_W, LANE), 0)
    ln = lax.broadcasted_iota(jnp.int32, (GRID_W, LANE), 1)
    cq = ln & (GRID_W - 1)
    cs = jnp.clip(cq - WIN_C // 2, 0, GRID_W - WIN_C)
    col_ok = jnp.logical_and(ck >= cs, ck < cs + WIN_C)
    left = ln < GRID_W
    n_dr = 2 * WIN_R - 1
    toep = []
    for dr in range(n_dr):
        base = jnp.broadcast_to(r_ref[dr:dr + 1, :], (GRID_W, LANE))
        toep.append([pltpu.roll(base, s * GRID_W, 1, stride=1, stride_axis=0) for s in range(2)])
    neg = jnp.full((GRID_W, LANE), MASK_NEG, F32)

    def rel(v, j, q4):
        if v == 0:
            return j - q4 + WIN_R - 1, j < WIN_R
        if v == 1:
            return j - q4 + WIN_R // 2 - 1, q4 <= j < q4 + WIN_R
        return j - q4 - 1, j >= NA_KROWS - WIN_R

    for v in range(NA_VARIANTS):
        for j in range(NA_KROWS):
            for lb in range(NA_QROWS // 2):
                halves = []
                for s in range(2):
                    dr, ok = rel(v, j, 2 * lb + s)
                    halves.append(jnp.where(col_ok, toep[dr][s], neg) if ok else neg)
                o_ref[v, j * GRID_W:(j + 1) * GRID_W, lb * LANE:(lb + 1) * LANE] = (
                    jnp.where(left, halves[0], halves[1]))


def _na_bias(rrev):
    return pl.pallas_call(
        _na_bias_kernel,
        out_shape=jax.ShapeDtypeStruct((NA_HEADS // 2, NA_VARIANTS, NA_KW, 2 * NA_Q), F32),
        grid=(NA_HEADS,),
        in_specs=[pl.BlockSpec((None, 2 * WIN_R, LANE), lambda h: (h, 0, 0))],
        out_specs=pl.BlockSpec((None, NA_VARIANTS, NA_KW, NA_Q), lambda h: (h // 2, 0, 0, h % 2)),
        compiler_params=pltpu.CompilerParams(
            dimension_semantics=("parallel",), vmem_limit_bytes=VMEM_LIMIT),
        name="na_bias",
    )(rrev)


def _na_kernel(q_ref, k_ref, v_ref, kc_ref, vc_ref, bias_ref, o_ref, vt_ref):
    lo = _head_masks()
    kc = kc_ref[...].astype(BF16)
    vct = vc_ref[...].T.astype(BF16)
    for t in range(DEC_SEQ // NA_KBLK):
        vt_ref[t] = v_ref[t * NA_KBLK:(t + 1) * NA_KBLK, :].astype(F32).T.astype(BF16)
    n_wblk = NA_KW // NA_KBLK

    for m in range(NA_STEPS):
        blk0 = min(max(m - 1, 0), DEC_SEQ // NA_KBLK - n_wblk)
        var = 0 if m == 0 else (2 if m == NA_STEPS - 1 else 1)
        q0 = m * NA_Q
        k0 = blk0 * NA_KBLK
        qg = q_ref[q0:q0 + NA_Q, :]
        kw = k_ref[k0:k0 + NA_KW, :]
        zq = jnp.zeros_like(qg)
        qcat = jnp.concatenate([jnp.where(lo, qg, zq), jnp.where(lo, zq, qg)], axis=0)
        sw = _dot_nt(kw, qcat) + bias_ref[var]
        sc = _dot_nt(kc, qcat)
        mx = jnp.maximum(jnp.max(sw, axis=0, keepdims=True), jnp.max(sc, axis=0, keepdims=True))
        pw = jnp.exp(sw - mx)
        pc = jnp.exp(sc - mx)
        l = jnp.sum(pw, axis=0, keepdims=True) + jnp.sum(pc, axis=0, keepdims=True)
        pwb = pw.astype(BF16)
        o = _dot(vct, pc.astype(BF16))
        for t in range(n_wblk):
            o = o + _dot(vt_ref[blk0 + t], pwb[t * NA_KBLK:(t + 1) * NA_KBLK, :])
        o = o / l
        both = jnp.concatenate([o[0:HEAD_DIM, 0:NA_Q], o[HEAD_DIM:2 * HEAD_DIM, NA_Q:2 * NA_Q]], axis=0)
        o_ref[q0:q0 + NA_Q, :] = both.T.astype(BF16)


def _neighbourhood_attention(pa, cache_k, cache_v, bias, layer):
    hp = NA_HEADS // 2
    blk = (DEC_SEQ, LANE)
    cblk = (None, None, PAST_LEN, LANE)
    return pl.pallas_call(
        _na_kernel,
        out_shape=jax.ShapeDtypeStruct((N_LAT, NA_WIDTH), BF16),
        grid=(hp, DEC_BATCH),
        in_specs=[
            pl.BlockSpec(blk, lambda h, b: (b, A_Q // LANE + h)),
            pl.BlockSpec(blk, lambda h, b: (b, A_K // LANE + h)),
            pl.BlockSpec(blk, lambda h, b: (b, A_V // LANE + h)),
            pl.BlockSpec(cblk, lambda h, b: (b, layer, 0, h)),
            pl.BlockSpec(cblk, lambda h, b: (b, layer, 0, h)),
            pl.BlockSpec((None, NA_VARIANTS, NA_KW, 2 * NA_Q), lambda h, b: (h, 0, 0, 0)),
        ],
        out_specs=pl.BlockSpec(blk, lambda h, b: (b, h)),
        scratch_shapes=[pltpu.VMEM((DEC_SEQ // NA_KBLK, LANE, NA_KBLK), BF16)],
        compiler_params=pltpu.CompilerParams(
            dimension_semantics=("parallel", "parallel"), vmem_limit_bytes=VMEM_LIMIT),
        name="na_attn",
    )(pa, pa, pa, cache_k, cache_v, bias)


def _rpb_reversed(rpb_l):
    n = WIN_C - 1
    pos = rpb_l[:, :, n::-1]
    neg = rpb_l[:, :, :n:-1]
    z = jnp.zeros(rpb_l.shape[:2] + (LANE - 2 * n - 1,), F32)
    r = jnp.concatenate([pos.astype(F32), z, neg.astype(F32)], axis=-1)
    return jnp.pad(r, ((0, 0), (0, 1), (0, 0)))


def _ssd_kernel(*refs, seq_len, has_h0, emit_state):
    it = iter(refs)
    x_ref, b_ref, c_ref, z_ref, dt_ref, dtb_ref, alog_ref, dsk_ref = (next(it) for _ in range(8))
    if has_h0:
        h0f_ref, h0b_ref = next(it), next(it)
    y_ref = next(it)
    if emit_state:
        hf_ref, hb_ref = next(it), next(it)
    bt_s, cum_s, dtv_s, y_s, st_s = (next(it) for _ in range(5))

    n_chunks = seq_len // CHUNK
    n_dir = 2 * HEADS_PER_GROUP

    lane = lax.broadcasted_iota(jnp.int32, (1, LANE), 1)
    ii = lax.broadcasted_iota(jnp.int32, (CHUNK, CHUNK), 0)
    jj = lax.broadcasted_iota(jnp.int32, (CHUNK, CHUNK), 1)
    lower = jj < ii
    diag = jj == ii
    tril = jnp.where(jj <= ii, 1.0, 0.0).astype(BF16)
    triu = jnp.where(jj >= ii, 1.0, 0.0).astype(BF16)
    er = lax.broadcasted_iota(jnp.int32, (LANE, GROUP_W), 0)
    ec = lax.broadcasted_iota(jnp.int32, (LANE, GROUP_W), 1) // SSD_HEADDIM
    exp_f = jnp.where(er == ec, 1.0, 0.0).astype(BF16)
    exp_b = jnp.where(er == ec + HEADS_PER_GROUP, 1.0, 0.0).astype(BF16)
    lo64 = lane < SSD_HEADDIM
    is_fwd = lane < HEADS_PER_GROUP
    a_neg = -jnp.exp(alog_ref[...])
    dtb = dtb_ref[...]
    dsk = dsk_ref[...]
    row_fwd = lax.broadcasted_iota(jnp.int32, (n_dir, 1), 0) < HEADS_PER_GROUP
    pad_rows = jnp.zeros((LANE - n_dir, CHUNK), F32)

    if has_h0:
        st_s[...] = h0f_ref[...].T
    else:
        st_s[...] = jnp.zeros((D_STATE, GROUP_W), F32)

    def local(cs):
        rows = [slice(c * CHUNK, (c + 1) * CHUNK) for c in cs]
        xcb = [x_ref[r, :] for r in rows]
        bcb = [b_ref[r, :] for r in rows]
        ccb = [c_ref[r, :] for r in rows]
        btb = [b.astype(F32).T.astype(BF16) for b in bcb]
        for r, b in zip(rows, btb):
            bt_s[r, :] = b
        raw = [dt_ref[r, :].T[0:n_dir, :] + dtb for r in rows]
        e = [jnp.exp(-jnp.abs(x)) for x in raw]
        log1p_e = [jnp.where(v < 1e-4, v * (1.0 - 0.5 * v), jnp.log(1.0 + v)) for v in e]
        dt_t = [jnp.maximum(x, 0.0) + l for x, l in zip(raw, log1p_e)]
        a = [d * a_neg for d in dt_t]
        cum_t = [jnp.where(row_fwd, _split_dot(v, triu, 3), _split_dot(v, tril, 3)) for v in a]
        cum = [jnp.concatenate([v, pad_rows], axis=0).T for v in cum_t]
        dtv = [jnp.concatenate([v, pad_rows], axis=0).T for v in dt_t]
        for k, r in enumerate(rows):
            cum_s[r, :] = cum[k]
            dtv_s[r, :] = dtv[k]
        g = [_dot_nt(cc, bb) for cc, bb in zip(ccb, bcb)]
        parts = [[] for _ in cs]
        for pp in range(HEADS_PER_GROUP // 2):
            for k in range(len(cs)):
                ws = []
                for hh in (2 * pp, 2 * pp + 1):
                    hb = HEADS_PER_GROUP + hh
                    seg = jnp.where(lower, cum[k][:, hh:hh + 1] - cum_t[k][hh:hh + 1, :],
                                    cum[k][:, hb:hb + 1] - cum_t[k][hb:hb + 1, :])
                    d_f = dt_t[k][hh:hh + 1, :]
                    dtm = jnp.where(lower, d_f, dt_t[k][hb:hb + 1, :]) + jnp.where(diag, d_f, 0.0)
                    ws.append((g[k] * jnp.exp(seg) * dtm).astype(BF16))
                wp = jnp.concatenate(ws, axis=1)
                xp = xcb[k][:, pp * LANE:(pp + 1) * LANE]
                zero = jnp.zeros_like(xp)
                rhs = jnp.concatenate([jnp.where(lo64, xp, zero), jnp.where(lo64, zero, xp)], axis=0)
                parts[k].append(_dot(wp, rhs))
        y = [jnp.concatenate(p, axis=1) for p in parts]
        ef_x = [_split_dot(jnp.exp(v), exp_f, EXPAND_TERMS) for v in cum]
        to_end = [jnp.where(is_fwd, v[CHUNK - 1:CHUNK, :] - v, 0.0) for v in cum]
        ff_x = [_split_dot(jnp.exp(t) * d, exp_f, EXPAND_TERMS) for t, d in zip(to_end, dtv)]
        return [(rows[k], xcb[k], ccb[k], btb[k], y[k], ef_x[k], ff_x[k]) for k in range(len(cs))]

    def carry_fwd(vals):
        r, xcb, ccb, btb, y, ef_x, ff_x = vals
        xc = xcb.astype(F32)
        st = st_s[...]
        y_s[r, :] = y + _dot(ccb, st.astype(BF16)) * ef_x + dsk * xc
        st_s[...] = st * ef_x[CHUNK - 1:CHUNK, :] + _dot(btb, (xc * ff_x).astype(BF16))

    step = 4 if n_chunks % 4 == 0 else (2 if n_chunks % 2 == 0 else 1)
    for c in range(0, n_chunks, step):
        for vals in local(list(range(c, c + step))):
            carry_fwd(vals)

    if emit_state:
        hf_ref[...] = st_s[...].T
    if has_h0:
        st_s[...] = h0b_ref[...].T
    else:
        st_s[...] = jnp.zeros((D_STATE, GROUP_W), F32)

    def local_bwd(cs):
        rows = [slice(c * CHUNK, (c + 1) * CHUNK) for c in cs]
        cum = [cum_s[r, :] for r in rows]
        dtv = [dtv_s[r, :] for r in rows]
        eb_x = [_split_dot(jnp.exp(v), exp_b, EXPAND_TERMS) for v in cum]
        to_start = [jnp.where(is_fwd, 0.0, v[0:1, :] - v) for v in cum]
        fb_x = [_split_dot(jnp.exp(t) * d, exp_b, EXPAND_TERMS) for t, d in zip(to_start, dtv)]
        xw = [(x_ref[r, :].astype(F32) * f).astype(BF16) for r, f in zip(rows, fb_x)]
        gate = [_silu(z_ref[r, :].astype(F32)) for r in rows]
        return [(rows[k], eb_x[k], xw[k], gate[k]) for k in range(len(cs))]

    def carry_bwd(vals):
        r, eb_x, xw, gate = vals
        st = st_s[...]
        y = y_s[r, :] + _dot(c_ref[r, :], st.astype(BF16)) * eb_x
        st_s[...] = st * eb_x[0:1, :] + _dot(bt_s[r, :], xw)
        y_ref[r, :] = (y * gate).astype(BF16)

    for c in reversed(range(0, n_chunks, step)):
        for vals in reversed(local_bwd(list(range(c, c + step)))):
            carry_bwd(vals)

    if emit_state:
        hb_ref[...] = st_s[...].T


def _ssd_mixer(pa, pb, dt32, dtb_g, alog_g, dsk, *, context, layer=0, h0f=None, h0b=None):
    seq_len = SEQ if context else DEC_SEQ
    n_seq = BATCH if context else DEC_BATCH
    has_h0 = h0f is not None
    emit_state = context
    xw_, bw_ = GROUP_W, D_STATE

    in_specs = [
        pl.BlockSpec((seq_len, xw_), lambda b, g: (b, B_X // xw_ + g)),
        pl.BlockSpec((seq_len, bw_), lambda b, g: (b, B_B // bw_ + g)),
        pl.BlockSpec((seq_len, bw_), lambda b, g: (b, B_C // bw_ + g)),
        pl.BlockSpec((seq_len, xw_), lambda b, g: (b, A_Z // xw_ + g)),
        pl.BlockSpec((seq_len, LANE), lambda b, g: (b, g)),
        pl.BlockSpec((None, 2 * HEADS_PER_GROUP, 1), lambda b, g: (g, 0, 0)),
        pl.BlockSpec((None, 2 * HEADS_PER_GROUP, 1), lambda b, g: (g, 0, 0)),
        pl.BlockSpec((1, xw_), lambda b, g: (0, g)),
    ]
    args = [pb, pb, pb, pa, dt32, dtb_g, alog_g, dsk]
    if has_h0:
        st_spec = pl.BlockSpec((None, None, xw_, D_STATE), lambda b, g: (b, layer, g, 0))
        in_specs += [st_spec, st_spec]
        args += [h0f, h0b]
    out_shape = [jax.ShapeDtypeStruct((n_seq * seq_len, SSD_INNER), BF16)]
    out_specs = [pl.BlockSpec((seq_len, xw_), lambda b, g: (b, g))]
    if emit_state:
        so = pl.BlockSpec((None, xw_, D_STATE), lambda b, g: (b, g, 0))
        out_shape += [jax.ShapeDtypeStruct((n_seq, SSD_INNER, D_STATE), F32)] * 2
        out_specs += [so, so]
    scratch = [
        pltpu.VMEM((seq_len, bw_), BF16),
        pltpu.VMEM((seq_len, LANE), F32),
        pltpu.VMEM((seq_len, LANE), F32),
        pltpu.VMEM((seq_len, xw_), F32),
        pltpu.VMEM((D_STATE, xw_), F32),
    ]
    return pl.pallas_call(
        functools.partial(_ssd_kernel, seq_len=seq_len, has_h0=has_h0, emit_state=emit_state),
        out_shape=tuple(out_shape),
        grid=(n_seq, SSD_GROUPS),
        in_specs=in_specs,
        out_specs=tuple(out_specs),
        scratch_shapes=scratch,
        compiler_params=pltpu.CompilerParams(
            dimension_semantics=("parallel", "parallel"), vmem_limit_bytes=VMEM_LIMIT),
        name="ssd_ctx" if context else "ssd_lat",
    )(*args)


def _merge_kernel(x_ref, nac_ref, nal_ref, sc_ref, sl_ref, gac_ref, gal_ref, gbc_ref, gbl_ref,
                  mod_ref, sg_ref, wna_ref, wssd_ref, wo_ref, o_ref, *, n_ctx_tiles):
    is_ctx = pl.program_id(0) < n_ctx_tiles
    na = jnp.where(is_ctx, nac_ref[...], nal_ref[...])
    y = jnp.where(is_ctx, sc_ref[...], sl_ref[...]).astype(F32)
    ga = jnp.where(is_ctx, gac_ref[...], gal_ref[...]).astype(F32)
    gb = jnp.where(is_ctx, gbc_ref[...], gbl_ref[...]).astype(F32)
    ms = jnp.mean(y * y, axis=-1, keepdims=True)
    yn = (y * lax.rsqrt(ms + EPS) * sg_ref[...]).astype(BF16)
    a = _dot(na, wna_ref[...])
    s = _dot(yn, wssd_ref[...])
    mix = _sigmoid(ga) * a + _sigmoid(gb) * s
    o = _dot(mix.astype(BF16), wo_ref[...])
    gate = mod_ref[:, 2 * D_MODEL:3 * D_MODEL]
    o_ref[...] = x_ref[...] + gate * o


def _merge(x, na_ctx, na_lat, ssd_ctx, ssd_lat, pb_ctx, pb_lat, mod, ssd_g, w_na, w_ssd, w_o):
    tm = TM_MERGE
    n_ctx_tiles = N_CTX // tm

    def ctx_i(col):
        return lambda i: (jnp.minimum(i, n_ctx_tiles - 1), col)

    def lat_i(col):
        return lambda i: (jnp.maximum(i - n_ctx_tiles, 0), col)

    return pl.pallas_call(
        functools.partial(_merge_kernel, n_ctx_tiles=n_ctx_tiles),
        out_shape=jax.ShapeDtypeStruct((N_TOK, D_MODEL), F32),
        grid=(N_TOK // tm,),
        in_specs=[
            pl.BlockSpec((tm, D_MODEL), lambda i: (i, 0)),
            pl.BlockSpec((tm, NA_WIDTH), ctx_i(0)),
            pl.BlockSpec((tm, NA_WIDTH), lat_i(0)),
            pl.BlockSpec((tm, SSD_INNER), ctx_i(0)),
            pl.BlockSpec((tm, SSD_INNER), lat_i(0)),
            pl.BlockSpec((tm, D_MODEL), ctx_i(B_GA // D_MODEL)),
            pl.BlockSpec((tm, D_MODEL), lat_i(B_GA // D_MODEL)),
            pl.BlockSpec((tm, D_MODEL), ctx_i(B_GB // D_MODEL)),
            pl.BlockSpec((tm, D_MODEL), lat_i(B_GB // D_MODEL)),
            pl.BlockSpec((None, 1, 6 * D_MODEL), lambda i: (_mod_row(i, tm), 0, 0)),
            pl.BlockSpec((1, SSD_INNER), lambda i: (0, 0)),
            _resident((NA_WIDTH, D_MODEL)),
            _resident((SSD_INNER, D_MODEL)),
            _resident((D_MODEL, D_MODEL)),
        ],
        out_specs=pl.BlockSpec((tm, D_MODEL), lambda i: (i, 0)),
        compiler_params=pltpu.CompilerParams(
            dimension_semantics=("parallel",), vmem_limit_bytes=VMEM_LIMIT),
        name="merge",
    )(x, na_ctx, na_lat, ssd_ctx, ssd_lat, pb_ctx, pb_lat, pb_ctx, pb_lat, mod, ssd_g, w_na, w_ssd, w_o)


def _ffn_kernel(x_ref, xp_ref, xn_ref, mod_ref, g_ref, wup_ref, cw_ref, cb_ref, wd_ref,
                o_ref, h_ref, act_ref, *, tm):
    i = pl.program_id(0)
    g = g_ref[...]
    shift = mod_ref[:, 3 * D_MODEL:4 * D_MODEL]
    scale = mod_ref[:, 4 * D_MODEL:5 * D_MODEL]
    h_ref[0:tm, :] = _norm_mod(x_ref[...], g, shift, scale)
    h_ref[tm:tm + 2 * SUBLANE, :] = _norm_mod(
        jnp.concatenate([xp_ref[...], xn_ref[...]], axis=0), g, shift, scale)

    seq = jnp.where(i * tm < N_CTX, SEQ, DEC_SEQ)
    row = lax.broadcasted_iota(jnp.int32, (tm, 1), 0)
    pos = (i * tm + row) & (seq - 1)
    has_prev = pos != 0
    has_next = pos != seq - 1
    first = row == 0
    last = row == tm - 1

    def conv(c0):
        cols = slice(c0, c0 + FF_TILE)
        pre = _dot(h_ref[...], wup_ref[:, cols])
        cw = cw_ref[:, cols]
        main = pre[0:tm]
        left = jnp.where(first, pre[tm + SUBLANE - 1:tm + SUBLANE], pltpu.roll(main, 1, 0))
        right = jnp.where(last, pre[tm + SUBLANE:tm + SUBLANE + 1], pltpu.roll(main, tm - 1, 0))
        return (cw[0:1] * jnp.where(has_prev, left, 0.0) + cw[1:2] * main
                + cw[2:3] * jnp.where(has_next, right, 0.0) + cb_ref[:, cols])

    for c0 in range(0, D_FF, FF_TILE):
        val = conv(c0)
        gate = conv(D_FF + c0)
        act_ref[:, c0:c0 + FF_TILE] = (_silu(gate) * val).astype(BF16)

    acc = _dot(act_ref[:, 0:FF_TILE], wd_ref[0:FF_TILE, :])
    for c0 in range(FF_TILE, D_FF, FF_TILE):
        acc = acc + _dot(act_ref[:, c0:c0 + FF_TILE], wd_ref[c0:c0 + FF_TILE, :])
    o_ref[...] = x_ref[...] + mod_ref[:, 5 * D_MODEL:6 * D_MODEL] * acc


def _conv_ffn(x, mod, g2, w_up, conv_w, conv_b, w_down):
    tm = TM_FFN
    hb = tm // SUBLANE
    last_h = N_TOK // SUBLANE - 1
    return pl.pallas_call(
        functools.partial(_ffn_kernel, tm=tm),
        out_shape=jax.ShapeDtypeStruct((N_TOK, D_MODEL), F32),
        grid=(N_TOK // tm,),
        in_specs=[
            pl.BlockSpec((tm, D_MODEL), lambda i: (i, 0)),
            pl.BlockSpec((SUBLANE, D_MODEL), lambda i: (jnp.maximum(i * hb - 1, 0), 0)),
            pl.BlockSpec((SUBLANE, D_MODEL), lambda i: (jnp.minimum((i + 1) * hb, last_h), 0)),
            pl.BlockSpec((None, 1, 6 * D_MODEL), lambda i: (_mod_row(i, tm), 0, 0)),
            pl.BlockSpec((1, D_MODEL), lambda i: (0, 0)),
            _resident((D_MODEL, 2 * D_FF)),
            _resident((FFN_CONV, 2 * D_FF)),
            _resident((1, 2 * D_FF)),
            _resident((D_FF, D_MODEL)),
        ],
        out_specs=pl.BlockSpec((tm, D_MODEL), lambda i: (i, 0)),
        scratch_shapes=[
            pltpu.VMEM((tm + 2 * SUBLANE, D_MODEL), BF16),
            pltpu.VMEM((tm, D_FF), BF16),
        ],
        compiler_params=pltpu.CompilerParams(
            dimension_semantics=("parallel",), vmem_limit_bytes=VMEM_LIMIT),
        name="conv_ffn",
    )(x, x, x, mod, g2, w_up, conv_w, conv_b, w_down)


def _w_in_group_b(w):
    x0 = A_W
    dt0 = x0 + CONV_DIM
    ga0 = dt0 + 2 * SSD_HEADS
    dt = w[:, dt0:ga0]
    hg = HEADS_PER_GROUP
    blocks = []
    for g in range(SSD_GROUPS):
        blocks += [dt[:, g * hg:(g + 1) * hg], dt[:, SSD_HEADS + g * hg:SSD_HEADS + (g + 1) * hg],
                   jnp.zeros((D_MODEL, LANE - 2 * hg), w.dtype)]
    return jnp.concatenate([w[:, x0:dt0], w[:, ga0:]] + blocks, axis=1).astype(BF16)


def _by_group(v):
    v = v.astype(F32).reshape(2, SSD_GROUPS, HEADS_PER_GROUP)
    return jnp.concatenate([v[0], v[1]], axis=1)[:, :, None]


def kernel(x_prompt, x_sample, c, cache_k, cache_v, state_ssd_fwd, state_ssd_bwd, c_ctx, w_ada, b_ada, norm1_g, w_in, q_norm_g, k_norm_g, rpb, ssd_conv_w, ssd_conv_b, a_log, dt_bias, d_skip, ssd_norm_g, w_na_out, w_ssd_out, w_o, norm2_g, w_up, ffn_conv_w, ffn_conv_b, w_down):
    x = jnp.concatenate([x_prompt.reshape(N_CTX, D_MODEL), x_sample.reshape(N_LAT, D_MODEL)], axis=0)
    cond = jnp.concatenate([c_ctx[None, :], c, jnp.zeros((16 - 1 - DEC_BATCH, D_MODEL), F32)], axis=0)
    mod = _modulation(cond, w_ada, b_ada.reshape(DEPTH, 1, 6 * D_MODEL))
    mod = mod.reshape(DEPTH, 16, 1, 6 * D_MODEL)

    ck = cache_k.reshape(DEC_BATCH, DEPTH, PAST_LEN, NA_WIDTH)
    cv = cache_v.reshape(DEC_BATCH, DEPTH, PAST_LEN, NA_WIDTH)
    h0f = state_ssd_fwd.reshape(DEC_BATCH, DEPTH, SSD_INNER, D_STATE)
    h0b = state_ssd_bwd.reshape(DEC_BATCH, DEPTH, SSD_INNER, D_STATE)
    bd = (jnp.arange(MXU_N)[:, None] // HEAD_DIM == jnp.arange(MXU_N)[None, :] // HEAD_DIM).astype(BF16)

    new_k, new_v, new_hf, new_hb = [], [], [], []
    for l in range(DEPTH):
        qkg = jnp.concatenate([jnp.tile(q_norm_g[l] * HEAD_DIM ** -0.5, NA_HEADS),
                               jnp.tile(k_norm_g[l], NA_HEADS)])[None, :]
        g1 = norm1_g[l][None, :]
        w_a = w_in[l][:, :A_W].astype(BF16)
        w_b = _w_in_group_b(w_in[l])
        conv_b = ssd_conv_b[l][None, :]
        pa_ctx, kv32 = _in_projection_a(x, mod[l], g1, w_a, qkg, bd, context=True)
        (pa_lat,) = _in_projection_a(x, mod[l], g1, w_a, qkg, bd, context=False)
        pb_ctx, dt_ctx = _in_projection_b(x, mod[l], g1, w_b, ssd_conv_w[l], conv_b, context=True)
        pb_lat, dt_lat = _in_projection_b(x, mod[l], g1, w_b, ssd_conv_w[l], conv_b, context=False)
        new_k.append(kv32[:, :NA_WIDTH].reshape(BATCH, SEQ, NA_HEADS, HEAD_DIM))
        new_v.append(kv32[:, NA_WIDTH:].reshape(BATCH, SEQ, NA_HEADS, HEAD_DIM))

        na_ctx = _context_attention(pa_ctx)
        na_lat = _neighbourhood_attention(pa_lat, ck, cv, _na_bias(_rpb_reversed(rpb[l])), l)

        dtb_g = _by_group(dt_bias[l])
        alog_g = _by_group(a_log[l])
        dsk = jnp.repeat(d_skip[l], SSD_HEADDIM)[None, :]
        ssd_ctx, hf, hb = _ssd_mixer(pa_ctx, pb_ctx, dt_ctx, dtb_g, alog_g, dsk, context=True)
        (ssd_lat,) = _ssd_mixer(pa_lat, pb_lat, dt_lat, dtb_g, alog_g, dsk, context=False,
                                layer=l, h0f=h0f, h0b=h0b)
        new_hf.append(hf.reshape(BATCH, SSD_HEADS, SSD_HEADDIM, D_STATE))
        new_hb.append(hb.reshape(BATCH, SSD_HEADS, SSD_HEADDIM, D_STATE))

        x = _merge(x, na_ctx, na_lat, ssd_ctx, ssd_lat, pb_ctx, pb_lat, mod[l], ssd_norm_g[l][None, :],
                   w_na_out[l].astype(BF16), w_ssd_out[l].astype(BF16), w_o[l].astype(BF16))
        x = _conv_ffn(x, mod[l], norm2_g[l][None, :], w_up[l].astype(BF16), ffn_conv_w[l],
                      ffn_conv_b[l][None, :], w_down[l].astype(BF16))

    y_prompt = x[:N_CTX].reshape(BATCH, SEQ, D_MODEL)
    y_sample = x[N_CTX:].reshape(DEC_BATCH, DEC_SEQ, D_MODEL)
    return (y_prompt, y_sample, jnp.stack(new_k, axis=1), jnp.stack(new_v, axis=1),
            jnp.stack(new_hf, axis=1), jnp.stack(new_hb, axis=1))
```

```python
import functools

import jax
import jax.numpy as jnp
from jax import lax
from jax.experimental import pallas as pl
from jax.experimental.pallas import tpu as pltpu

D_MODEL = 1024
BATCH = 16
SEQ = 256
DEPTH = 2
DEC_BATCH = 8
DEC_SEQ = 2048
PAST_LEN = 256
GRID_W = 64
NA_HEADS = 16
HEAD_DIM = 64
NA_WIDTH = NA_HEADS * HEAD_DIM
WIN_R = 8
WIN_C = 16
SSD_INNER = 2 * D_MODEL
SSD_HEADDIM = 64
SSD_HEADS = SSD_INNER // SSD_HEADDIM
SSD_GROUPS = 4
D_STATE = 128
SSD_CONV = 4
CHUNK = 128
CONV_DIM = SSD_INNER + 2 * SSD_GROUPS * D_STATE
D_FF = 2816
FFN_CONV = 3
EPS = 1e-6

N_CTX = BATCH * SEQ
N_LAT = DEC_BATCH * DEC_SEQ
N_TOK = N_CTX + N_LAT
GRID_ROWS = DEC_SEQ // GRID_W
HEADS_PER_GROUP = SSD_HEADS // SSD_GROUPS
GROUP_W = HEADS_PER_GROUP * SSD_HEADDIM

LANE = 128
SUBLANE = 8
BF16_ROWS = 16
MXU_N = 256

A_Q, A_K, A_V, A_Z, A_W = 0, 1024, 2048, 3072, 5120
B_X, B_B, B_C, B_GA, B_GB, B_W = 0, 2048, 2560, 3072, 4096, 5120
DT_W = SSD_GROUPS * LANE
IN_TILE_N = 2 * MXU_N
TM_IN = 512
TM_MERGE = 512
TM_FFN = 512
FF_TILE = 256
MASK_NEG = -1e30
EXPAND_TERMS = 1
LOG2E = 1.4426950408889634

NA_QROWS = 4
NA_KROWS = 12
NA_Q = NA_QROWS * GRID_W
NA_KW = NA_KROWS * GRID_W
NA_KBLK = 256
NA_STEPS = GRID_ROWS // NA_QROWS
NA_VARIANTS = 3

F32 = jnp.float32
BF16 = jnp.bfloat16
NT = (((1,), (1,)), ((), ()))

VMEM_LIMIT = 56 * 1024 * 1024


def _sigmoid(x):
    return 0.5 + 0.5 * jnp.tanh(0.5 * x)


def _silu(x):
    h = 0.5 * x
    return h + h * jnp.tanh(h)


def _dot(a, b):
    return jnp.dot(a, b, preferred_element_type=F32)


def _dot_nt(a, b):
    return lax.dot_general(a, b, NT, preferred_element_type=F32)


def _split_dot(a, b, terms):
    acc = None
    r = a
    for t in range(terms):
        p = r.astype(BF16)
        d = _dot(p, b)
        acc = d if acc is None else acc + d
        if t + 1 < terms:
            r = r - p.astype(F32)
    return acc


def _mod_row(i, tm, context):
    return 0 if context else 1 + (i * tm) // DEC_SEQ


def _resident(shape):
    return pl.BlockSpec(shape, lambda *_: (0,) * len(shape), pipeline_mode=pl.Buffered(1))


def _norm_mod(xv, g, shift, scale):
    ms = jnp.mean(xv * xv, axis=-1, keepdims=True)
    return (xv * lax.rsqrt(ms + EPS) * g * (1.0 + scale) + shift).astype(BF16)


def _mod_kernel(c_ref, w_ref, b_ref, o_ref):
    s = _silu(c_ref[...]).astype(BF16)
    o_ref[...] = _dot(s, w_ref[...].astype(BF16)) + b_ref[...]


def _modulation(cond, w_ada, b_ada):
    tn = 1536
    return pl.pallas_call(
        _mod_kernel,
        out_shape=jax.ShapeDtypeStruct((DEPTH, 16, 6 * D_MODEL), F32),
        grid=(DEPTH, 6 * D_MODEL // tn),
        in_specs=[
            pl.BlockSpec((16, D_MODEL), lambda l, j: (0, 0)),
            pl.BlockSpec((None, D_MODEL, tn), lambda l, j: (l, 0, j)),
            pl.BlockSpec((None, 1, tn), lambda l, j: (l, 0, j)),
        ],
        out_specs=pl.BlockSpec((None, 16, tn), lambda l, j: (l, 0, j)),
        compiler_params=pltpu.CompilerParams(
            dimension_semantics=("parallel", "parallel"), vmem_limit_bytes=VMEM_LIMIT),
        name="adaln_mod",
    )(cond, w_ada, b_ada)


def _inproj_a_kernel(x_ref, mod_ref, g_ref, w_ref, qkg_ref, bd_ref, *rest, emit_kv):
    if emit_kv:
        p_ref, kv_ref, h_ref = rest
    else:
        p_ref, h_ref = rest
    h_ref[...] = _norm_mod(x_ref[...], g_ref[...], mod_ref[:, 0:D_MODEL], mod_ref[:, D_MODEL:2 * D_MODEL])
    for c0 in range(0, A_W, IN_TILE_N):
        cols = slice(c0, c0 + IN_TILE_N)
        acc = _dot(h_ref[...], w_ref[:, cols])
        if c0 < A_V:
            sq = acc * acc
            ss = jnp.concatenate(
                [_split_dot(sq[:, t:t + MXU_N], bd_ref[...], 2) for t in range(0, IN_TILE_N, MXU_N)], axis=1)
            acc = acc * lax.rsqrt(ss * (1.0 / HEAD_DIM) + EPS) * qkg_ref[:, cols]
        p_ref[:, cols] = acc.astype(BF16)
        if emit_kv and A_K <= c0 < A_Z:
            kv_ref[:, c0 - A_K:c0 - A_K + IN_TILE_N] = acc


def _in_projection_a(x, mod, g1, w, qkg, bd, *, context):
    tm = TM_IN
    n_rows = x.shape[0]
    out_shape = [jax.ShapeDtypeStruct((n_rows, A_W), BF16)]
    out_specs = [pl.BlockSpec((tm, A_W), lambda i: (i, 0))]
    if context:
        out_shape.append(jax.ShapeDtypeStruct((n_rows, 2 * NA_WIDTH), F32))
        out_specs.append(pl.BlockSpec((tm, 2 * NA_WIDTH), lambda i: (i, 0)))
    return pl.pallas_call(
        functools.partial(_inproj_a_kernel, emit_kv=context),
        out_shape=tuple(out_shape),
        grid=(n_rows // tm,),
        in_specs=[
            pl.BlockSpec((tm, D_MODEL), lambda i: (i, 0)),
            pl.BlockSpec((None, 1, 6 * D_MODEL), lambda i: (_mod_row(i, tm, context), 0, 0)),
            pl.BlockSpec((1, D_MODEL), lambda i: (0, 0)),
            _resident((D_MODEL, A_W)),
            pl.BlockSpec((1, A_V), lambda i: (0, 0)),
            pl.BlockSpec((MXU_N, MXU_N), lambda i: (0, 0)),
        ],
        out_specs=tuple(out_specs),
        scratch_shapes=[pltpu.VMEM((tm, D_MODEL), BF16)],
        compiler_params=pltpu.CompilerParams(
            dimension_semantics=("parallel",), vmem_limit_bytes=VMEM_LIMIT),
        name="in_proj_a_ctx" if context else "in_proj_a_lat",
    )(x, mod, g1, w, qkg, bd)


def _inproj_b_kernel(x_ref, xp_ref, xn_ref, mod_ref, g_ref, w_ref, cw_ref, cb_ref,
                     p_ref, dt_ref, h_ref, *, tm, seq_len):
    i = pl.program_id(0)
    g = g_ref[...]
    shift = mod_ref[:, 0:D_MODEL]
    scale = mod_ref[:, D_MODEL:2 * D_MODEL]
    h_ref[0:tm, :] = _norm_mod(x_ref[...], g, shift, scale)
    h_ref[tm:tm + 2 * SUBLANE, :] = _norm_mod(
        jnp.concatenate([xp_ref[...], xn_ref[...]], axis=0), g, shift, scale)

    ext = tm + 2 * SUBLANE
    start = i * tm
    row = lax.broadcasted_iota(jnp.int32, (tm, 1), 0)
    pos = (start + row) & (seq_len - 1)
    inner_edges = seq_len < tm
    keep_before = (start & (seq_len - 1)) != 0
    keep_after = ((start + tm) & (seq_len - 1)) != 0
    for c0 in range(0, B_GA, IN_TILE_N):
        cols = slice(c0, c0 + IN_TILE_N)
        pre = _dot(h_ref[...], w_ref[:, cols])
        before = jnp.where(keep_before, pre[tm:tm + SUBLANE], 0.0)
        after = jnp.where(keep_after, pre[tm + SUBLANE:ext], 0.0)
        v = jnp.concatenate([before, pre[0:tm], after], axis=0)
        back2 = pltpu.roll(v, 2, 0)[SUBLANE:SUBLANE + tm]
        back1 = pltpu.roll(v, 1, 0)[SUBLANE:SUBLANE + tm]
        ahead = pltpu.roll(v, ext - 1, 0)[SUBLANE:SUBLANE + tm]
        if inner_edges:
            back2 = jnp.where(pos >= 2, back2, 0.0)
            back1 = jnp.where(pos >= 1, back1, 0.0)
            ahead = jnp.where(pos != seq_len - 1, ahead, 0.0)
        cw = cw_ref[:, cols]
        u = cw[0:1] * back2 + cw[1:2] * back1 + cw[2:3] * pre[0:tm] + cw[3:4] * ahead + cb_ref[:, cols]
        p_ref[:, cols] = _silu(u).astype(BF16)
    for c0 in range(B_GA, B_W, IN_TILE_N):
        cols = slice(c0, c0 + IN_TILE_N)
        p_ref[:, cols] = _dot(h_ref[0:tm, :], w_ref[:, cols]).astype(BF16)
    dt_ref[...] = _dot(h_ref[0:tm, :], w_ref[:, B_W:B_W + DT_W])


def _in_projection_b(x, mod, g1, w, conv_w, conv_b, *, context):
    tm = TM_IN
    n_rows = x.shape[0]
    hb = tm // SUBLANE
    last_h = n_rows // SUBLANE - 1
    return pl.pallas_call(
        functools.partial(_inproj_b_kernel, tm=tm, seq_len=SEQ if context else DEC_SEQ),
        out_shape=(jax.ShapeDtypeStruct((n_rows, B_W), BF16), jax.ShapeDtypeStruct((n_rows, DT_W), F32)),
        grid=(n_rows // tm,),
        in_specs=[
            pl.BlockSpec((tm, D_MODEL), lambda i: (i, 0)),
            pl.BlockSpec((SUBLANE, D_MODEL), lambda i: (jnp.maximum(i * hb - 1, 0), 0)),
            pl.BlockSpec((SUBLANE, D_MODEL), lambda i: (jnp.minimum((i + 1) * hb, last_h), 0)),
            pl.BlockSpec((None, 1, 6 * D_MODEL), lambda i: (_mod_row(i, tm, context), 0, 0)),
            pl.BlockSpec((1, D_MODEL), lambda i: (0, 0)),
            _resident((D_MODEL, B_W + DT_W)),
            pl.BlockSpec((SSD_CONV, CONV_DIM), lambda i: (0, 0)),
            pl.BlockSpec((1, CONV_DIM), lambda i: (0, 0)),
        ],
        out_specs=(pl.BlockSpec((tm, B_W), lambda i: (i, 0)), pl.BlockSpec((tm, DT_W), lambda i: (i, 0))),
        scratch_shapes=[pltpu.VMEM((tm + 2 * SUBLANE, D_MODEL), BF16)],
        compiler_params=pltpu.CompilerParams(
            dimension_semantics=("parallel",), vmem_limit_bytes=VMEM_LIMIT),
        name="in_proj_b_ctx" if context else "in_proj_b_lat",
    )(x, x, x, mod, g1, w, conv_w, conv_b)


def _head_masks():
    lane = lax.broadcasted_iota(jnp.int32, (1, LANE), 1)
    return lane < HEAD_DIM


def _ctx_attn_kernel(q_ref, k_ref, v_ref, o_ref):
    lo = _head_masks()
    for c0 in range(0, NA_WIDTH, LANE):
        cols = slice(c0, c0 + LANE)
        q = q_ref[:, cols]
        k = k_ref[:, cols]
        v = v_ref[:, cols]
        outs = []
        for hh in range(2):
            m = lo if hh == 0 else jnp.logical_not(lo)
            qm = jnp.where(m, q, jnp.zeros_like(q))
            s = _dot_nt(qm, k)
            mx = jnp.max(s, axis=-1, keepdims=True)
            p = jnp.exp2(s - mx)
            l = jnp.sum(p, axis=-1, keepdims=True)
            outs.append(_dot(p.astype(BF16), v) / l)
        o_ref[:, cols] = jnp.where(lo, outs[0], outs[1]).astype(BF16)


def _context_attention(pa):
    blk = (SEQ, NA_WIDTH)
    return pl.pallas_call(
        _ctx_attn_kernel,
        out_shape=jax.ShapeDtypeStruct((N_CTX, NA_WIDTH), BF16),
        grid=(BATCH,),
        in_specs=[
            pl.BlockSpec(blk, lambda b: (b, A_Q // NA_WIDTH)),
            pl.BlockSpec(blk, lambda b: (b, A_K // NA_WIDTH)),
            pl.BlockSpec(blk, lambda b: (b, A_V // NA_WIDTH)),
        ],
        out_specs=pl.BlockSpec(blk, lambda b: (b, 0)),
        compiler_params=pltpu.CompilerParams(
            dimension_semantics=("parallel",), vmem_limit_bytes=VMEM_LIMIT),
        name="ctx_attn",
    )(pa, pa, pa)


def _na_bias_kernel(r_ref, o_ref):
    ck = lax.broadcasted_iota(jnp.int32, (GRID_W, LANE), 0)
    ln = lax.broadcasted_iota(jnp.int32, (GRID_W, LANE), 1)
    cq = ln & (GRID_W - 1)
    cs = jnp.clip(cq - WIN_C // 2, 0, GRID_W - WIN_C)
    col_ok = jnp.logical_and(ck >= cs, ck < cs + WIN_C)
    left = ln < GRID_W
    n_dr = 2 * WIN_R - 1
    toep = []
    for dr in range(n_dr):
        base = jnp.broadcast_to(r_ref[dr:dr + 1, :] * LOG2E, (GRID_W, LANE))
        toep.append([pltpu.roll(base, s * GRID_W, 1, stride=1, stride_axis=0) for s in range(2)])
    neg = jnp.full((GRID_W, LANE), MASK_NEG, F32)

    def rel(v, j, q4):
        if v == 0:
            return j - q4 + WIN_R - 1, j < WIN_R
        if v == 1:
            return j - q4 + WIN_R // 2 - 1, q4 <= j < q4 + WIN_R
        return j - q4 - 1, j >= NA_KROWS - WIN_R

    for v in range(NA_VARIANTS):
        for j in range(NA_KROWS):
            for lb in range(NA_QROWS // 2):
                halves = []
                for s in range(2):
                    dr, ok = rel(v, j, 2 * lb + s)
                    halves.append(jnp.where(col_ok, toep[dr][s], neg) if ok else neg)
                o_ref[v, j * GRID_W:(j + 1) * GRID_W, lb * LANE:(lb + 1) * LANE] = (
                    jnp.where(left, halves[0], halves[1]))


def _na_bias(rrev):
    return pl.pallas_call(
        _na_bias_kernel,
        out_shape=jax.ShapeDtypeStruct((NA_HEADS // 2, NA_VARIANTS, NA_KW, 2 * NA_Q), F32),
        grid=(NA_HEADS,),
        in_specs=[pl.BlockSpec((None, 2 * WIN_R, LANE), lambda h: (h, 0, 0))],
        out_specs=pl.BlockSpec((None, NA_VARIANTS, NA_KW, NA_Q), lambda h: (h // 2, 0, 0, h % 2)),
        compiler_params=pltpu.CompilerParams(
            dimension_semantics=("parallel",), vmem_limit_bytes=VMEM_LIMIT),
        name="na_bias",
    )(rrev)


def _na_kernel(q_ref, k_ref, v_ref, kc_ref, vc_ref, bias_ref, o_ref, vt_ref):
    lo = _head_masks()
    kc = kc_ref[...].astype(BF16)
    vct = vc_ref[...].T.astype(BF16)
    for t in range(DEC_SEQ // NA_KBLK):
        vt_ref[t] = v_ref[t * NA_KBLK:(t + 1) * NA_KBLK, :].astype(F32).T.astype(BF16)
    n_wblk = NA_KW // NA_KBLK

    def steps(ms):
        n = range(len(ms))
        blk0 = [min(max(m - 1, 0), DEC_SEQ // NA_KBLK - n_wblk) for m in ms]
        var = [0 if m == 0 else (2 if m == NA_STEPS - 1 else 1) for m in ms]
        qg = [q_ref[m * NA_Q:(m + 1) * NA_Q, :] for m in ms]
        kw = [k_ref[b * NA_KBLK:b * NA_KBLK + NA_KW, :] for b in blk0]
        qcat = [jnp.concatenate([jnp.where(lo, q, jnp.zeros_like(q)), jnp.where(lo, jnp.zeros_like(q), q)], axis=0)
                for q in qg]
        sw = [_dot_nt(kw[i], qcat[i]) + bias_ref[var[i]] for i in n]
        sc = [_dot_nt(kc, qcat[i]) for i in n]
        mx = [jnp.maximum(jnp.max(sw[i], axis=0, keepdims=True), jnp.max(sc[i], axis=0, keepdims=True)) for i in n]
        pw = [jnp.exp2(sw[i] - mx[i]) for i in n]
        pc = [jnp.exp2(sc[i] - mx[i]) for i in n]
        l = [jnp.sum(pw[i], axis=0, keepdims=True) + jnp.sum(pc[i], axis=0, keepdims=True) for i in n]
        pwb = [p.astype(BF16) for p in pw]
        o = [_dot(vct, p.astype(BF16)) for p in pc]
        for t in range(n_wblk):
            o = [o[i] + _dot(vt_ref[blk0[i] + t], pwb[i][t * NA_KBLK:(t + 1) * NA_KBLK, :]) for i in n]
        for i in n:
            on = o[i] / l[i]
            both = jnp.concatenate([on[0:HEAD_DIM, 0:NA_Q], on[HEAD_DIM:2 * HEAD_DIM, NA_Q:2 * NA_Q]], axis=0)
            o_ref[ms[i] * NA_Q:(ms[i] + 1) * NA_Q, :] = both.T.astype(BF16)

    for m in range(0, NA_STEPS, 2):
        steps([m, m + 1])


def _neighbourhood_attention(pa, cache_k, cache_v, bias, layer):
    hp = NA_HEADS // 2
    blk = (DEC_SEQ, LANE)
    cblk = (None, None, PAST_LEN, LANE)
    return pl.pallas_call(
        _na_kernel,
        out_shape=jax.ShapeDtypeStruct((N_LAT, NA_WIDTH), BF16),
        grid=(hp, DEC_BATCH),
        in_specs=[
            pl.BlockSpec(blk, lambda h, b: (b, A_Q // LANE + h)),
            pl.BlockSpec(blk, lambda h, b: (b, A_K // LANE + h)),
            pl.BlockSpec(blk, lambda h, b: (b, A_V // LANE + h)),
            pl.BlockSpec(cblk, lambda h, b: (b, layer, 0, h)),
            pl.BlockSpec(cblk, lambda h, b: (b, layer, 0, h)),
            pl.BlockSpec((None, NA_VARIANTS, NA_KW, 2 * NA_Q), lambda h, b: (h, 0, 0, 0)),
        ],
        out_specs=pl.BlockSpec(blk, lambda h, b: (b, h)),
        scratch_shapes=[pltpu.VMEM((DEC_SEQ // NA_KBLK, LANE, NA_KBLK), BF16)],
        compiler_params=pltpu.CompilerParams(
            dimension_semantics=("parallel", "parallel"), vmem_limit_bytes=VMEM_LIMIT),
        name="na_attn",
    )(pa, pa, pa, cache_k, cache_v, bias)


def _rpb_reversed(rpb_l):
    n = WIN_C - 1
    pos = rpb_l[:, :, n::-1]
    neg = rpb_l[:, :, :n:-1]
    z = jnp.zeros(rpb_l.shape[:2] + (LANE - 2 * n - 1,), F32)
    r = jnp.concatenate([pos.astype(F32), z, neg.astype(F32)], axis=-1)
    return jnp.pad(r, ((0, 0), (0, 1), (0, 0)))


def _ssd_kernel(*refs, seq_len, has_h0, emit_state):
    it = iter(refs)
    x_ref, b_ref, c_ref, z_ref, dt_ref, dtb_ref, alog_ref, dsk_ref = (next(it) for _ in range(8))
    if has_h0:
        h0f_ref, h0b_ref = next(it), next(it)
    y_ref = next(it)
    if emit_state:
        hf_ref, hb_ref = next(it), next(it)
    bt_s, cum_s, dtv_s, y_s, st_s = (next(it) for _ in range(5))

    n_chunks = seq_len // CHUNK
    n_dir = 2 * HEADS_PER_GROUP

    lane = lax.broadcasted_iota(jnp.int32, (1, LANE), 1)
    ii = lax.broadcasted_iota(jnp.int32, (CHUNK, CHUNK), 0)
    jj = lax.broadcasted_iota(jnp.int32, (CHUNK, CHUNK), 1)
    lower = jj < ii
    diag = jj == ii
    tril = jnp.where(jj <= ii, 1.0, 0.0).astype(BF16)
    triu = jnp.where(jj >= ii, 1.0, 0.0).astype(BF16)
    er = lax.broadcasted_iota(jnp.int32, (LANE, GROUP_W), 0)
    ec = lax.broadcasted_iota(jnp.int32, (LANE, GROUP_W), 1) // SSD_HEADDIM
    exp_f = jnp.where(er == ec, 1.0, 0.0).astype(BF16)
    exp_b = jnp.where(er == ec + HEADS_PER_GROUP, 1.0, 0.0).astype(BF16)
    lo64 = lane < SSD_HEADDIM
    is_fwd = lane < HEADS_PER_GROUP
    a_neg = -jnp.exp(alog_ref[...])
    dtb = dtb_ref[...]
    dsk = dsk_ref[...]
    row_fwd = lax.broadcasted_iota(jnp.int32, (n_dir, 1), 0) < HEADS_PER_GROUP
    pad_rows = jnp.zeros((LANE - n_dir, CHUNK), F32)

    if has_h0:
        st_s[...] = h0f_ref[...].T
    else:
        st_s[...] = jnp.zeros((D_STATE, GROUP_W), F32)

    def local(cs):
        rows = [slice(c * CHUNK, (c + 1) * CHUNK) for c in cs]
        xcb = [x_ref[r, :] for r in rows]
        bcb = [b_ref[r, :] for r in rows]
        ccb = [c_ref[r, :] for r in rows]
        btb = [b.astype(F32).T.astype(BF16) for b in bcb]
        for r, b in zip(rows, btb):
            bt_s[r, :] = b
        raw = [dt_ref[r, :].T[0:n_dir, :] + dtb for r in rows]
        e = [jnp.exp(-jnp.abs(x)) for x in raw]
        log1p_e = [jnp.where(v < 1e-4, v * (1.0 - 0.5 * v), jnp.log(1.0 + v)) for v in e]
        dt_t = [jnp.maximum(x, 0.0) + l for x, l in zip(raw, log1p_e)]
        a = [d * a_neg for d in dt_t]
        cum_t = [jnp.where(row_fwd, _split_dot(v, triu, 3), _split_dot(v, tril, 3)) for v in a]
        cum = [jnp.concatenate([v, pad_rows], axis=0).T for v in cum_t]
        dtv = [jnp.concatenate([v, pad_rows], axis=0).T for v in dt_t]
        for k, r in enumerate(rows):
            cum_s[r, :] = cum[k]
            dtv_s[r, :] = dtv[k]
        g = [_dot_nt(cc, bb) for cc, bb in zip(ccb, bcb)]
        src_t = [c_t - jnp.log(d) for c_t, d in zip(cum_t, dt_t)]
        parts = [[] for _ in cs]
        for pp in range(HEADS_PER_GROUP // 2):
            for k in range(len(cs)):
                ws = []
                for hh in (2 * pp, 2 * pp + 1):
                    hb = HEADS_PER_GROUP + hh
                    seg = jnp.where(lower, cum[k][:, hh:hh + 1] - src_t[k][hh:hh + 1, :],
                                    cum[k][:, hb:hb + 1] - src_t[k][hb:hb + 1, :])
                    both = jnp.exp(seg) + jnp.where(diag, dt_t[k][hh:hh + 1, :], 0.0)
                    ws.append((g[k] * both).astype(BF16))
                wp = jnp.concatenate(ws, axis=1)
                xp = xcb[k][:, pp * LANE:(pp + 1) * LANE]
                zero = jnp.zeros_like(xp)
                rhs = jnp.concatenate([jnp.where(lo64, xp, zero), jnp.where(lo64, zero, xp)], axis=0)
                parts[k].append(_dot(wp, rhs))
        y = [jnp.concatenate(p, axis=1) for p in parts]
        ef_x = [_split_dot(jnp.exp(v), exp_f, EXPAND_TERMS) for v in cum]
        to_end = [jnp.where(is_fwd, v[CHUNK - 1:CHUNK, :] - v, 0.0) for v in cum]
        ff_x = [_split_dot(jnp.exp(t) * d, exp_f, EXPAND_TERMS) for t, d in zip(to_end, dtv)]
        return [(rows[k], xcb[k], ccb[k], btb[k], y[k], ef_x[k], ff_x[k]) for k in range(len(cs))]

    def carry_fwd(vals):
        r, xcb, ccb, btb, y, ef_x, ff_x = vals
        xc = xcb.astype(F32)
        st = st_s[...]
        y_s[r, :] = y + _dot(ccb, st.astype(BF16)) * ef_x + dsk * xc
        st_s[...] = st * ef_x[CHUNK - 1:CHUNK, :] + _dot(btb, (xc * ff_x).astype(BF16))

    step = 4 if n_chunks % 4 == 0 else (2 if n_chunks % 2 == 0 else 1)
    for c in range(0, n_chunks, step):
        for vals in local(list(range(c, c + step))):
            carry_fwd(vals)

    if emit_state:
        hf_ref[...] = st_s[...].T
    if has_h0:
        st_s[...] = h0b_ref[...].T
    else:
        st_s[...] = jnp.zeros((D_STATE, GROUP_W), F32)

    def local_bwd(cs):
        rows = [slice(c * CHUNK, (c + 1) * CHUNK) for c in cs]
        cum = [cum_s[r, :] for r in rows]
        dtv = [dtv_s[r, :] for r in rows]
        eb_x = [_split_dot(jnp.exp(v), exp_b, EXPAND_TERMS) for v in cum]
        to_start = [jnp.where(is_fwd, 0.0, v[0:1, :] - v) for v in cum]
        fb_x = [_split_dot(jnp.exp(t) * d, exp_b, EXPAND_TERMS) for t, d in zip(to_start, dtv)]
        xw = [(x_ref[r, :].astype(F32) * f).astype(BF16) for r, f in zip(rows, fb_x)]
        gate = [_silu(z_ref[r, :].astype(F32)) for r in rows]
        return [(rows[k], eb_x[k], xw[k], gate[k]) for k in range(len(cs))]

    def carry_bwd(vals):
        r, eb_x, xw, gate = vals
        st = st_s[...]
        y = y_s[r, :] + _dot(c_ref[r, :], st.astype(BF16)) * eb_x
        st_s[...] = st * eb_x[0:1, :] + _dot(bt_s[r, :], xw)
        y_ref[r, :] = (y * gate).astype(BF16)

    for c in reversed(range(0, n_chunks, step)):
        for vals in reversed(local_bwd(list(range(c, c + step)))):
            carry_bwd(vals)

    if emit_state:
        hb_ref[...] = st_s[...].T


def _ssd_mixer(pa, pb, dt32, dtb_g, alog_g, dsk, *, context, layer=0, h0f=None, h0b=None):
    seq_len = SEQ if context else DEC_SEQ
    n_seq = BATCH if context else DEC_BATCH
    has_h0 = h0f is not None
    emit_state = context
    xw_, bw_ = GROUP_W, D_STATE

    in_specs = [
        pl.BlockSpec((seq_len, xw_), lambda b, g: (b, B_X // xw_ + g)),
        pl.BlockSpec((seq_len, bw_), lambda b, g: (b, B_B // bw_ + g)),
        pl.BlockSpec((seq_len, bw_), lambda b, g: (b, B_C // bw_ + g)),
        pl.BlockSpec((seq_len, xw_), lambda b, g: (b, A_Z // xw_ + g)),
        pl.BlockSpec((seq_len, LANE), lambda b, g: (b, g)),
        pl.BlockSpec((None, 2 * HEADS_PER_GROUP, 1), lambda b, g: (g, 0, 0)),
        pl.BlockSpec((None, 2 * HEADS_PER_GROUP, 1), lambda b, g: (g, 0, 0)),
        pl.BlockSpec((1, xw_), lambda b, g: (0, g)),
    ]
    args = [pb, pb, pb, pa, dt32, dtb_g, alog_g, dsk]
    if has_h0:
        st_spec = pl.BlockSpec((None, None, xw_, D_STATE), lambda b, g: (b, layer, g, 0))
        in_specs += [st_spec, st_spec]
        args += [h0f, h0b]
    out_shape = [jax.ShapeDtypeStruct((n_seq * seq_len, SSD_INNER), BF16)]
    out_specs = [pl.BlockSpec((seq_len, xw_), lambda b, g: (b, g))]
    if emit_state:
        so = pl.BlockSpec((None, xw_, D_STATE), lambda b, g: (b, g, 0))
        out_shape += [jax.ShapeDtypeStruct((n_seq, SSD_INNER, D_STATE), F32)] * 2
        out_specs += [so, so]
    scratch = [
        pltpu.VMEM((seq_len, bw_), BF16),
        pltpu.VMEM((seq_len, LANE), F32),
        pltpu.VMEM((seq_len, LANE), F32),
        pltpu.VMEM((seq_len, xw_), F32),
        pltpu.VMEM((D_STATE, xw_), F32),
    ]
    return pl.pallas_call(
        functools.partial(_ssd_kernel, seq_len=seq_len, has_h0=has_h0, emit_state=emit_state),
        out_shape=tuple(out_shape),
        grid=(n_seq, SSD_GROUPS),
        in_specs=in_specs,
        out_specs=tuple(out_specs),
        scratch_shapes=scratch,
        compiler_params=pltpu.CompilerParams(
            dimension_semantics=("parallel", "parallel"), vmem_limit_bytes=VMEM_LIMIT),
        name="ssd_ctx" if context else "ssd_lat",
    )(*args)


def _merge_kernel(x_ref, na_ref, s_ref, ga_ref, gb_ref, mod_ref, sg_ref, wna_ref, wssd_ref, wo_ref, o_ref):
    y = s_ref[...].astype(F32)
    ms = jnp.mean(y * y, axis=-1, keepdims=True)
    yn = (y * lax.rsqrt(ms + EPS) * sg_ref[...]).astype(BF16)
    a = _dot(na_ref[...], wna_ref[...])
    s = _dot(yn, wssd_ref[...])
    mix = _sigmoid(ga_ref[...].astype(F32)) * a + _sigmoid(gb_ref[...].astype(F32)) * s
    o = _dot(mix.astype(BF16), wo_ref[...])
    o_ref[...] = x_ref[...] + mod_ref[:, 2 * D_MODEL:3 * D_MODEL] * o


def _merge(x, na, ssd, pb, mod, ssd_g, w_na, w_ssd, w_o, *, context):
    tm = TM_MERGE
    n_rows = x.shape[0]
    return pl.pallas_call(
        _merge_kernel,
        out_shape=jax.ShapeDtypeStruct((n_rows, D_MODEL), F32),
        grid=(n_rows // tm,),
        in_specs=[
            pl.BlockSpec((tm, D_MODEL), lambda i: (i, 0)),
            pl.BlockSpec((tm, NA_WIDTH), lambda i: (i, 0)),
            pl.BlockSpec((tm, SSD_INNER), lambda i: (i, 0)),
            pl.BlockSpec((tm, D_MODEL), lambda i: (i, B_GA // D_MODEL)),
            pl.BlockSpec((tm, D_MODEL), lambda i: (i, B_GB // D_MODEL)),
            pl.BlockSpec((None, 1, 6 * D_MODEL), lambda i: (_mod_row(i, tm, context), 0, 0)),
            pl.BlockSpec((1, SSD_INNER), lambda i: (0, 0)),
            _resident((NA_WIDTH, D_MODEL)),
            _resident((SSD_INNER, D_MODEL)),
            _resident((D_MODEL, D_MODEL)),
        ],
        out_specs=pl.BlockSpec((tm, D_MODEL), lambda i: (i, 0)),
        compiler_params=pltpu.CompilerParams(
            dimension_semantics=("parallel",), vmem_limit_bytes=VMEM_LIMIT),
        name="merge_ctx" if context else "merge_lat",
    )(x, na, ssd, pb, pb, mod, ssd_g, w_na, w_ssd, w_o)


def _ffn_kernel(x_ref, xp_ref, xn_ref, mod_ref, g_ref, wup_ref, cw_ref, cb_ref, wd_ref,
                o_ref, h_ref, act_ref, *, tm, seq_len):
    i = pl.program_id(0)
    g = g_ref[...]
    shift = mod_ref[:, 3 * D_MODEL:4 * D_MODEL]
    scale = mod_ref[:, 4 * D_MODEL:5 * D_MODEL]
    h_ref[0:tm, :] = _norm_mod(x_ref[...], g, shift, scale)
    h_ref[tm:tm + 2 * SUBLANE, :] = _norm_mod(
        jnp.concatenate([xp_ref[...], xn_ref[...]], axis=0), g, shift, scale)

    ext = tm + 2 * SUBLANE
    start = i * tm
    row = lax.broadcasted_iota(jnp.int32, (tm, 1), 0)
    pos = (start + row) & (seq_len - 1)
    inner_edges = seq_len < tm
    keep_before = (start & (seq_len - 1)) != 0
    keep_after = ((start + tm) & (seq_len - 1)) != 0

    def conv(c0):
        cols = slice(c0, c0 + FF_TILE)
        pre = _dot(h_ref[...], wup_ref[:, cols])
        before = jnp.where(keep_before, pre[tm:tm + SUBLANE], 0.0)
        after = jnp.where(keep_after, pre[tm + SUBLANE:ext], 0.0)
        v = jnp.concatenate([before, pre[0:tm], after], axis=0)
        left = pltpu.roll(v, 1, 0)[SUBLANE:SUBLANE + tm]
        right = pltpu.roll(v, ext - 1, 0)[SUBLANE:SUBLANE + tm]
        if inner_edges:
            left = jnp.where(pos != 0, left, 0.0)
            right = jnp.where(pos != seq_len - 1, right, 0.0)
        cw = cw_ref[:, cols]
        return cw[0:1] * left + cw[1:2] * pre[0:tm] + cw[2:3] * right + cb_ref[:, cols]

    for c0 in range(0, D_FF, FF_TILE):
        val = conv(c0)
        gate = conv(D_FF + c0)
        act_ref[:, c0:c0 + FF_TILE] = (_silu(gate) * val).astype(BF16)

    acc = _dot(act_ref[:, 0:FF_TILE], wd_ref[0:FF_TILE, :])
    for c0 in range(FF_TILE, D_FF, FF_TILE):
        acc = acc + _dot(act_ref[:, c0:c0 + FF_TILE], wd_ref[c0:c0 + FF_TILE, :])
    o_ref[...] = x_ref[...] + mod_ref[:, 5 * D_MODEL:6 * D_MODEL] * acc


def _conv_ffn(x, mod, g2, w_up, conv_w, conv_b, w_down, *, context):
    tm = TM_FFN
    n_rows = x.shape[0]
    hb = tm // SUBLANE
    last_h = n_rows // SUBLANE - 1
    return pl.pallas_call(
        functools.partial(_ffn_kernel, tm=tm, seq_len=SEQ if context else DEC_SEQ),
        out_shape=jax.ShapeDtypeStruct((n_rows, D_MODEL), F32),
        grid=(n_rows // tm,),
        in_specs=[
            pl.BlockSpec((tm, D_MODEL), lambda i: (i, 0)),
            pl.BlockSpec((SUBLANE, D_MODEL), lambda i: (jnp.maximum(i * hb - 1, 0), 0)),
            pl.BlockSpec((SUBLANE, D_MODEL), lambda i: (jnp.minimum((i + 1) * hb, last_h), 0)),
            pl.BlockSpec((None, 1, 6 * D_MODEL), lambda i: (_mod_row(i, tm, context), 0, 0)),
            pl.BlockSpec((1, D_MODEL), lambda i: (0, 0)),
            _resident((D_MODEL, 2 * D_FF)),
            _resident((FFN_CONV, 2 * D_FF)),
            _resident((1, 2 * D_FF)),
            _resident((D_FF, D_MODEL)),
        ],
        out_specs=pl.BlockSpec((tm, D_MODEL), lambda i: (i, 0)),
        scratch_shapes=[
            pltpu.VMEM((tm + 2 * SUBLANE, D_MODEL), BF16),
            pltpu.VMEM((tm, D_FF), BF16),
        ],
        compiler_params=pltpu.CompilerParams(
            dimension_semantics=("parallel",), vmem_limit_bytes=VMEM_LIMIT),
        name="conv_ffn_ctx" if context else "conv_ffn_lat",
    )(x, x, x, mod, g2, w_up, conv_w, conv_b, w_down)


def _w_in_group_b(w):
    x0 = A_W
    dt0 = x0 + CONV_DIM
    ga0 = dt0 + 2 * SSD_HEADS
    dt = w[:, dt0:ga0]
    hg = HEADS_PER_GROUP
    blocks = []
    for g in range(SSD_GROUPS):
        blocks += [dt[:, g * hg:(g + 1) * hg], dt[:, SSD_HEADS + g * hg:SSD_HEADS + (g + 1) * hg],
                   jnp.zeros((D_MODEL, LANE - 2 * hg), w.dtype)]
    return jnp.concatenate([w[:, x0:dt0], w[:, ga0:]] + blocks, axis=1).astype(BF16)


def _by_group(v):
    v = v.astype(F32).reshape(2, SSD_GROUPS, HEADS_PER_GROUP)
    return jnp.concatenate([v[0], v[1]], axis=1)[:, :, None]


def kernel(x_prompt, x_sample, c, cache_k, cache_v, state_ssd_fwd, state_ssd_bwd, c_ctx, w_ada, b_ada, norm1_g, w_in, q_norm_g, k_norm_g, rpb, ssd_conv_w, ssd_conv_b, a_log, dt_bias, d_skip, ssd_norm_g, w_na_out, w_ssd_out, w_o, norm2_g, w_up, ffn_conv_w, ffn_conv_b, w_down):
    xs = {True: x_prompt.reshape(N_CTX, D_MODEL), False: x_sample.reshape(N_LAT, D_MODEL)}
    cond = jnp.concatenate([c_ctx[None, :], c, jnp.zeros((16 - 1 - DEC_BATCH, D_MODEL), F32)], axis=0)
    mod = _modulation(cond, w_ada, b_ada.reshape(DEPTH, 1, 6 * D_MODEL))
    mod = mod.reshape(DEPTH, 16, 1, 6 * D_MODEL)

    ck = cache_k.reshape(DEC_BATCH, DEPTH, PAST_LEN, NA_WIDTH)
    cv = cache_v.reshape(DEC_BATCH, DEPTH, PAST_LEN, NA_WIDTH)
    h0f = state_ssd_fwd.reshape(DEC_BATCH, DEPTH, SSD_INNER, D_STATE)
    h0b = state_ssd_bwd.reshape(DEC_BATCH, DEPTH, SSD_INNER, D_STATE)
    bd = (jnp.arange(MXU_N)[:, None] // HEAD_DIM == jnp.arange(MXU_N)[None, :] // HEAD_DIM).astype(BF16)

    new_k, new_v, new_hf, new_hb = [], [], [], []
    for l in range(DEPTH):
        qkg = jnp.concatenate([jnp.tile(q_norm_g[l] * (HEAD_DIM ** -0.5 * LOG2E), NA_HEADS),
                               jnp.tile(k_norm_g[l], NA_HEADS)])[None, :]
        g1 = norm1_g[l][None, :]
        w_a = w_in[l][:, :A_W].astype(BF16)
        w_b = _w_in_group_b(w_in[l])
        conv_b = ssd_conv_b[l][None, :]
        pa_ctx, kv32 = _in_projection_a(xs[True], mod[l], g1, w_a, qkg, bd, context=True)
        (pa_lat,) = _in_projection_a(xs[False], mod[l], g1, w_a, qkg, bd, context=False)
        pb_ctx, dt_ctx = _in_projection_b(xs[True], mod[l], g1, w_b, ssd_conv_w[l], conv_b, context=True)
        pb_lat, dt_lat = _in_projection_b(xs[False], mod[l], g1, w_b, ssd_conv_w[l], conv_b, context=False)
        new_k.append(kv32[:, :NA_WIDTH].reshape(BATCH, SEQ, NA_HEADS, HEAD_DIM))
        new_v.append(kv32[:, NA_WIDTH:].reshape(BATCH, SEQ, NA_HEADS, HEAD_DIM))

        na_ctx = _context_attention(pa_ctx)
        na_lat = _neighbourhood_attention(pa_lat, ck, cv, _na_bias(_rpb_reversed(rpb[l])), l)

        dtb_g = _by_group(dt_bias[l])
        alog_g = _by_group(a_log[l])
        dsk = jnp.repeat(d_skip[l], SSD_HEADDIM)[None, :]
        ssd_ctx, hf, hb = _ssd_mixer(pa_ctx, pb_ctx, dt_ctx, dtb_g, alog_g, dsk, context=True)
        (ssd_lat,) = _ssd_mixer(pa_lat, pb_lat, dt_lat, dtb_g, alog_g, dsk, context=False,
                                layer=l, h0f=h0f, h0b=h0b)
        new_hf.append(hf.reshape(BATCH, SSD_HEADS, SSD_HEADDIM, D_STATE))
        new_hb.append(hb.reshape(BATCH, SSD_HEADS, SSD_HEADDIM, D_STATE))

        branch = {True: (na_ctx, ssd_ctx, pb_ctx), False: (na_lat, ssd_lat, pb_lat)}
        w_merge = (w_na_out[l].astype(BF16), w_ssd_out[l].astype(BF16), w_o[l].astype(BF16))
        w_ffn = (w_up[l].astype(BF16), ffn_conv_w[l], ffn_conv_b[l][None, :], w_down[l].astype(BF16))
        for ctx in (True, False):
            na, ssd, pb = branch[ctx]
            x1 = _merge(xs[ctx], na, ssd, pb, mod[l], ssd_norm_g[l][None, :], *w_merge, context=ctx)
            xs[ctx] = _conv_ffn(x1, mod[l], norm2_g[l][None, :], *w_ffn, context=ctx)

    y_prompt = xs[True].reshape(BATCH, SEQ, D_MODEL)
    y_sample = xs[False].reshape(DEC_BATCH, DEC_SEQ, D_MODEL)
    return (y_prompt, y_sample, jnp.stack(new_k, axis=1), jnp.stack(new_v, axis=1),
            jnp.stack(new_hf, axis=1), jnp.stack(new_hb, axis=1))
```

```python
import functools

import jax
import jax.numpy as jnp
from jax import lax
from jax.experimental import pallas as pl
from jax.experimental.pallas import tpu as pltpu

D_MODEL = 1024
BATCH = 16
SEQ = 256
DEPTH = 2
DEC_BATCH = 8
DEC_SEQ = 2048
PAST_LEN = 256
GRID_W = 64
NA_HEADS = 16
HEAD_DIM = 64
NA_WIDTH = NA_HEADS * HEAD_DIM
WIN_R = 8
WIN_C = 16
SSD_INNER = 2 * D_MODEL
SSD_HEADDIM = 64
SSD_HEADS = SSD_INNER // SSD_HEADDIM
SSD_GROUPS = 4
D_STATE = 128
SSD_CONV = 4
CHUNK = 128
CONV_DIM = SSD_INNER + 2 * SSD_GROUPS * D_STATE
D_FF = 2816
FFN_CONV = 3
EPS = 1e-6

N_CTX = BATCH * SEQ
N_LAT = DEC_BATCH * DEC_SEQ
N_TOK = N_CTX + N_LAT
GRID_ROWS = DEC_SEQ // GRID_W
HEADS_PER_GROUP = SSD_HEADS // SSD_GROUPS
GROUP_W = HEADS_PER_GROUP * SSD_HEADDIM

LANE = 128
SUBLANE = 8
BF16_ROWS = 16
MXU_N = 256

A_Q, A_K, A_V, A_Z, A_W = 0, 1024, 2048, 3072, 5120
B_X, B_B, B_C, B_GA, B_GB, B_W = 0, 2048, 2560, 3072, 4096, 5120
DT_W = SSD_GROUPS * LANE
IN_TILE_N = 2 * MXU_N
TM_IN = 512
TM_MERGE = 512
TM_FFN = 512
FF_TILE = 256
MASK_NEG = -1e30
EXPAND_TERMS = 1
LOG2E = 1.4426950408889634
QK_SQ_TERMS = 1

NA_QROWS = 4
NA_KROWS = 12
NA_Q = NA_QROWS * GRID_W
NA_KW = NA_KROWS * GRID_W
NA_KBLK = 256
NA_STEPS = GRID_ROWS // NA_QROWS
NA_VARIANTS = 3
NA_LOCKSTEP = 4

F32 = jnp.float32
BF16 = jnp.bfloat16
NT = (((1,), (1,)), ((), ()))

VMEM_LIMIT = 56 * 1024 * 1024


def _sigmoid(x):
    return 0.5 + 0.5 * jnp.tanh(0.5 * x)


def _silu(x):
    h = 0.5 * x
    return h + h * jnp.tanh(h)


def _dot(a, b):
    return jnp.dot(a, b, preferred_element_type=F32)


def _dot_nt(a, b):
    return lax.dot_general(a, b, NT, preferred_element_type=F32)


def _split_dot(a, b, terms):
    acc = None
    r = a
    for t in range(terms):
        p = r.astype(BF16)
        d = _dot(p, b)
        acc = d if acc is None else acc + d
        if t + 1 < terms:
            r = r - p.astype(F32)
    return acc


def _mod_row(i, tm, context):
    return 0 if context else 1 + (i * tm) // DEC_SEQ


def _resident(shape):
    return pl.BlockSpec(shape, lambda *_: (0,) * len(shape), pipeline_mode=pl.Buffered(1))


def _norm_mod(xv, g, shift, scale):
    ms = jnp.mean(xv * xv, axis=-1, keepdims=True)
    return (xv * lax.rsqrt(ms + EPS) * g * (1.0 + scale) + shift).astype(BF16)


def _mod_kernel(c_ref, w_ref, b_ref, o_ref):
    s = _silu(c_ref[...]).astype(BF16)
    o_ref[...] = _dot(s, w_ref[...].astype(BF16)) + b_ref[...]


def _modulation(cond, w_ada, b_ada):
    tn = 1536
    return pl.pallas_call(
        _mod_kernel,
        out_shape=jax.ShapeDtypeStruct((DEPTH, 16, 6 * D_MODEL), F32),
        grid=(DEPTH, 6 * D_MODEL // tn),
        in_specs=[
            pl.BlockSpec((16, D_MODEL), lambda l, j: (0, 0)),
            pl.BlockSpec((None, D_MODEL, tn), lambda l, j: (l, 0, j)),
            pl.BlockSpec((None, 1, tn), lambda l, j: (l, 0, j)),
        ],
        out_specs=pl.BlockSpec((None, 16, tn), lambda l, j: (l, 0, j)),
        compiler_params=pltpu.CompilerParams(
            dimension_semantics=("parallel", "parallel"), vmem_limit_bytes=VMEM_LIMIT),
        name="adaln_mod",
    )(cond, w_ada, b_ada)


def _inproj_a_kernel(x_ref, mod_ref, g_ref, w_ref, qkg_ref, bd_ref, *rest, emit_kv):
    if emit_kv:
        p_ref, k_ref, v_ref, h_ref = rest
    else:
        p_ref, h_ref = rest
    h_ref[...] = _norm_mod(x_ref[...], g_ref[...], mod_ref[:, 0:D_MODEL], mod_ref[:, D_MODEL:2 * D_MODEL])
    for c0 in range(0, A_W, IN_TILE_N):
        cols = slice(c0, c0 + IN_TILE_N)
        acc = _dot(h_ref[...], w_ref[:, cols])
        if c0 < A_V:
            sq = acc * acc
            ss = jnp.concatenate(
                [_split_dot(sq[:, t:t + MXU_N], bd_ref[...], QK_SQ_TERMS) for t in range(0, IN_TILE_N, MXU_N)],
                axis=1)
            acc = acc * lax.rsqrt(ss * (1.0 / HEAD_DIM) + EPS) * qkg_ref[:, cols]
        p_ref[:, cols] = acc.astype(BF16)
        if emit_kv and A_K <= c0 < A_V:
            k_ref[:, c0 - A_K:c0 - A_K + IN_TILE_N] = acc
        if emit_kv and A_V <= c0 < A_Z:
            v_ref[:, c0 - A_V:c0 - A_V + IN_TILE_N] = acc


def _in_projection_a(x, mod, g1, w, qkg, bd, *, context):
    tm = TM_IN
    n_rows = x.shape[0]
    out_shape = [jax.ShapeDtypeStruct((n_rows, A_W), BF16)]
    out_specs = [pl.BlockSpec((tm, A_W), lambda i: (i, 0))]
    if context:
        out_shape += [jax.ShapeDtypeStruct((n_rows, NA_WIDTH), F32)] * 2
        out_specs += [pl.BlockSpec((tm, NA_WIDTH), lambda i: (i, 0))] * 2
    return pl.pallas_call(
        functools.partial(_inproj_a_kernel, emit_kv=context),
        out_shape=tuple(out_shape),
        grid=(n_rows // tm,),
        in_specs=[
            pl.BlockSpec((tm, D_MODEL), lambda i: (i, 0)),
            pl.BlockSpec((None, 1, 6 * D_MODEL), lambda i: (_mod_row(i, tm, context), 0, 0)),
            pl.BlockSpec((1, D_MODEL), lambda i: (0, 0)),
            _resident((D_MODEL, A_W)),
            pl.BlockSpec((1, A_V), lambda i: (0, 0)),
            pl.BlockSpec((MXU_N, MXU_N), lambda i: (0, 0)),
        ],
        out_specs=tuple(out_specs),
        scratch_shapes=[pltpu.VMEM((tm, D_MODEL), BF16)],
        compiler_params=pltpu.CompilerParams(
            dimension_semantics=("parallel",), vmem_limit_bytes=VMEM_LIMIT),
        name="in_proj_a_ctx" if context else "in_proj_a_lat",
    )(x, mod, g1, w, qkg, bd)


def _inproj_b_kernel(x_ref, xp_ref, xn_ref, mod_ref, g_ref, w_ref, cw_ref, cb_ref,
                     p_ref, dt_ref, h_ref, *, tm, seq_len):
    i = pl.program_id(0)
    g = g_ref[...]
    shift = mod_ref[:, 0:D_MODEL]
    scale = mod_ref[:, D_MODEL:2 * D_MODEL]
    h_ref[0:tm, :] = _norm_mod(x_ref[...], g, shift, scale)
    h_ref[tm:tm + 2 * SUBLANE, :] = _norm_mod(
        jnp.concatenate([xp_ref[...], xn_ref[...]], axis=0), g, shift, scale)

    ext = tm + 2 * SUBLANE
    start = i * tm
    row = lax.broadcasted_iota(jnp.int32, (tm, 1), 0)
    pos = (start + row) & (seq_len - 1)
    inner_edges = seq_len < tm
    keep_before = (start & (seq_len - 1)) != 0
    keep_after = ((start + tm) & (seq_len - 1)) != 0
    for c0 in range(0, B_GA, IN_TILE_N):
        cols = slice(c0, c0 + IN_TILE_N)
        pre = _dot(h_ref[...], w_ref[:, cols])
        before = jnp.where(keep_before, pre[tm:tm + SUBLANE], 0.0)
        after = jnp.where(keep_after, pre[tm + SUBLANE:ext], 0.0)
        v = jnp.concatenate([before, pre[0:tm], after], axis=0)
        back2 = pltpu.roll(v, 2, 0)[SUBLANE:SUBLANE + tm]
        back1 = pltpu.roll(v, 1, 0)[SUBLANE:SUBLANE + tm]
        ahead = pltpu.roll(v, ext - 1, 0)[SUBLANE:SUBLANE + tm]
        if inner_edges:
            back2 = jnp.where(pos >= 2, back2, 0.0)
            back1 = jnp.where(pos >= 1, back1, 0.0)
            ahead = jnp.where(pos != seq_len - 1, ahead, 0.0)
        cw = cw_ref[:, cols]
        u = cw[0:1] * back2 + cw[1:2] * back1 + cw[2:3] * pre[0:tm] + cw[3:4] * ahead + cb_ref[:, cols]
        p_ref[:, cols] = _silu(u).astype(BF16)
    for c0 in range(B_GA, B_W, IN_TILE_N):
        cols = slice(c0, c0 + IN_TILE_N)
        p_ref[:, cols] = _dot(h_ref[0:tm, :], w_ref[:, cols]).astype(BF16)
    dt_ref[...] = _dot(h_ref[0:tm, :], w_ref[:, B_W:B_W + DT_W])


def _in_projection_b(x, mod, g1, w, conv_w, conv_b, *, context):
    tm = TM_IN
    n_rows = x.shape[0]
    hb = tm // SUBLANE
    last_h = n_rows // SUBLANE - 1
    return pl.pallas_call(
        functools.partial(_inproj_b_kernel, tm=tm, seq_len=SEQ if context else DEC_SEQ),
        out_shape=(jax.ShapeDtypeStruct((n_rows, B_W), BF16), jax.ShapeDtypeStruct((n_rows, DT_W), F32)),
        grid=(n_rows // tm,),
        in_specs=[
            pl.BlockSpec((tm, D_MODEL), lambda i: (i, 0)),
            pl.BlockSpec((SUBLANE, D_MODEL), lambda i: (jnp.maximum(i * hb - 1, 0), 0)),
            pl.BlockSpec((SUBLANE, D_MODEL), lambda i: (jnp.minimum((i + 1) * hb, last_h), 0)),
            pl.BlockSpec((None, 1, 6 * D_MODEL), lambda i: (_mod_row(i, tm, context), 0, 0)),
            pl.BlockSpec((1, D_MODEL), lambda i: (0, 0)),
            _resident((D_MODEL, B_W + DT_W)),
            pl.BlockSpec((SSD_CONV, CONV_DIM), lambda i: (0, 0)),
            pl.BlockSpec((1, CONV_DIM), lambda i: (0, 0)),
        ],
        out_specs=(pl.BlockSpec((tm, B_W), lambda i: (i, 0)), pl.BlockSpec((tm, DT_W), lambda i: (i, 0))),
        scratch_shapes=[pltpu.VMEM((tm + 2 * SUBLANE, D_MODEL), BF16)],
        compiler_params=pltpu.CompilerParams(
            dimension_semantics=("parallel",), vmem_limit_bytes=VMEM_LIMIT),
        name="in_proj_b_ctx" if context else "in_proj_b_lat",
    )(x, x, x, mod, g1, w, conv_w, conv_b)


def _head_masks():
    lane = lax.broadcasted_iota(jnp.int32, (1, LANE), 1)
    return lane < HEAD_DIM


def _ctx_attn_kernel(q_ref, k_ref, v_ref, o_ref):
    lo = _head_masks()
    for c0 in range(0, NA_WIDTH, LANE):
        cols = slice(c0, c0 + LANE)
        q = q_ref[:, cols]
        k = k_ref[:, cols]
        v = v_ref[:, cols]
        outs = []
        for hh in range(2):
            m = lo if hh == 0 else jnp.logical_not(lo)
            qm = jnp.where(m, q, jnp.zeros_like(q))
            s = _dot_nt(qm, k)
            mx = jnp.max(s, axis=-1, keepdims=True)
            p = jnp.exp2(s - mx)
            l = jnp.sum(p, axis=-1, keepdims=True)
            outs.append(_dot(p.astype(BF16), v) / l)
        o_ref[:, cols] = jnp.where(lo, outs[0], outs[1]).astype(BF16)


def _context_attention(pa):
    blk = (SEQ, NA_WIDTH)
    return pl.pallas_call(
        _ctx_attn_kernel,
        out_shape=jax.ShapeDtypeStruct((N_CTX, NA_WIDTH), BF16),
        grid=(BATCH,),
        in_specs=[
            pl.BlockSpec(blk, lambda b: (b, A_Q // NA_WIDTH)),
            pl.BlockSpec(blk, lambda b: (b, A_K // NA_WIDTH)),
            pl.BlockSpec(blk, lambda b: (b, A_V // NA_WIDTH)),
        ],
        out_specs=pl.BlockSpec(blk, lambda b: (b, 0)),
        compiler_params=pltpu.CompilerParams(
            dimension_semantics=("parallel",), vmem_limit_bytes=VMEM_LIMIT),
        name="ctx_attn",
    )(pa, pa, pa)


def _na_bias_kernel(r_ref, o_ref):
    ck = lax.broadcasted_iota(jnp.int32, (GRID_W, LANE), 0)
    ln = lax.broadcasted_iota(jnp.int32, (GRID_W, LANE), 1)
    cq = ln & (GRID_W - 1)
    cs = jnp.clip(cq - WIN_C // 2, 0, GRID_W - WIN_C)
    col_ok = jnp.logical_and(ck >= cs, ck < cs + WIN_C)
    left = ln < GRID_W
    n_dr = 2 * WIN_R - 1
    toep = []
    for dr in range(n_dr):
        base = jnp.broadcast_to(r_ref[dr:dr + 1, :] * LOG2E, (GRID_W, LANE))
        toep.append([pltpu.roll(base, s * GRID_W, 1, stride=1, stride_axis=0) for s in range(2)])
    neg = jnp.full((GRID_W, LANE), MASK_NEG, F32)

    def rel(v, j, q4):
        if v == 0:
            return j - q4 + WIN_R - 1, j < WIN_R
        if v == 1:
            return j - q4 + WIN_R // 2 - 1, q4 <= j < q4 + WIN_R
        return j - q4 - 1, j >= NA_KROWS - WIN_R

    for v in range(NA_VARIANTS):
        for j in range(NA_KROWS):
            for lb in range(NA_QROWS // 2):
                halves = []
                for s in range(2):
                    dr, ok = rel(v, j, 2 * lb + s)
                    halves.append(jnp.where(col_ok, toep[dr][s], neg) if ok else neg)
                o_ref[v, j * GRID_W:(j + 1) * GRID_W, lb * LANE:(lb + 1) * LANE] = (
                    jnp.where(left, halves[0], halves[1]))


def _na_bias(rrev):
    return pl.pallas_call(
        _na_bias_kernel,
        out_shape=jax.ShapeDtypeStruct((NA_HEADS // 2, NA_VARIANTS, NA_KW, 2 * NA_Q), F32),
        grid=(NA_HEADS,),
        in_specs=[pl.BlockSpec((None, 2 * WIN_R, LANE), lambda h: (h, 0, 0))],
        out_specs=pl.BlockSpec((None, NA_VARIANTS, NA_KW, NA_Q), lambda h: (h // 2, 0, 0, h % 2)),
        compiler_params=pltpu.CompilerParams(
            dimension_semantics=("parallel",), vmem_limit_bytes=VMEM_LIMIT),
        name="na_bias",
    )(rrev)


def _na_kernel(q_ref, k_ref, v_ref, kc_ref, vc_ref, bias_ref, o_ref, vt_ref):
    lo = _head_masks()
    kc = kc_ref[...].astype(BF16)
    vct = vc_ref[...].T.astype(BF16)
    for t in range(DEC_SEQ // NA_KBLK):
        vt_ref[t] = v_ref[t * NA_KBLK:(t + 1) * NA_KBLK, :].astype(F32).T.astype(BF16)
    n_wblk = NA_KW // NA_KBLK

    def steps(ms):
        n = range(len(ms))
        blk0 = [min(max(m - 1, 0), DEC_SEQ // NA_KBLK - n_wblk) for m in ms]
        var = [0 if m == 0 else (2 if m == NA_STEPS - 1 else 1) for m in ms]
        qg = [q_ref[m * NA_Q:(m + 1) * NA_Q, :] for m in ms]
        kw = [k_ref[b * NA_KBLK:b * NA_KBLK + NA_KW, :] for b in blk0]
        qcat = [jnp.concatenate([jnp.where(lo, q, jnp.zeros_like(q)), jnp.where(lo, jnp.zeros_like(q), q)], axis=0)
                for q in qg]
        sw = [_dot_nt(kw[i], qcat[i]) + bias_ref[var[i]] for i in n]
        sc = [_dot_nt(kc, qcat[i]) for i in n]
        mx = [jnp.maximum(jnp.max(sw[i], axis=0, keepdims=True), jnp.max(sc[i], axis=0, keepdims=True)) for i in n]
        pw = [jnp.exp2(sw[i] - mx[i]) for i in n]
        pc = [jnp.exp2(sc[i] - mx[i]) for i in n]
        l = [jnp.sum(pw[i], axis=0, keepdims=True) + jnp.sum(pc[i], axis=0, keepdims=True) for i in n]
        pwb = [p.astype(BF16) for p in pw]
        o = [_dot(vct, p.astype(BF16)) for p in pc]
        for t in range(n_wblk):
            o = [o[i] + _dot(vt_ref[blk0[i] + t], pwb[i][t * NA_KBLK:(t + 1) * NA_KBLK, :]) for i in n]
        for i in n:
            on = o[i] / l[i]
            both = jnp.concatenate([on[0:HEAD_DIM, 0:NA_Q], on[HEAD_DIM:2 * HEAD_DIM, NA_Q:2 * NA_Q]], axis=0)
            o_ref[ms[i] * NA_Q:(ms[i] + 1) * NA_Q, :] = both.T.astype(BF16)

    for m in range(0, NA_STEPS, NA_LOCKSTEP):
        steps(list(range(m, m + NA_LOCKSTEP)))


def _neighbourhood_attention(pa, cache_k, cache_v, bias, layer):
    hp = NA_HEADS // 2
    blk = (DEC_SEQ, LANE)
    cblk = (None, None, PAST_LEN, LANE)
    return pl.pallas_call(
        _na_kernel,
        out_shape=jax.ShapeDtypeStruct((N_LAT, NA_WIDTH), BF16),
        grid=(hp, DEC_BATCH),
        in_specs=[
            pl.BlockSpec(blk, lambda h, b: (b, A_Q // LANE + h)),
            pl.BlockSpec(blk, lambda h, b: (b, A_K // LANE + h)),
            pl.BlockSpec(blk, lambda h, b: (b, A_V // LANE + h)),
            pl.BlockSpec(cblk, lambda h, b: (b, layer, 0, h)),
            pl.BlockSpec(cblk, lambda h, b: (b, layer, 0, h)),
            pl.BlockSpec((None, NA_VARIANTS, NA_KW, 2 * NA_Q), lambda h, b: (h, 0, 0, 0)),
        ],
        out_specs=pl.BlockSpec(blk, lambda h, b: (b, h)),
        scratch_shapes=[pltpu.VMEM((DEC_SEQ // NA_KBLK, LANE, NA_KBLK), BF16)],
        compiler_params=pltpu.CompilerParams(
            dimension_semantics=("parallel", "parallel"), vmem_limit_bytes=VMEM_LIMIT),
        name="na_attn",
    )(pa, pa, pa, cache_k, cache_v, bias)


def _rpb_reversed(rpb_l):
    n = WIN_C - 1
    pos = rpb_l[:, :, n::-1]
    neg = rpb_l[:, :, :n:-1]
    z = jnp.zeros(rpb_l.shape[:2] + (LANE - 2 * n - 1,), F32)
    r = jnp.concatenate([pos.astype(F32), z, neg.astype(F32)], axis=-1)
    return jnp.pad(r, ((0, 0), (0, 1), (0, 0)))


def _ssd_kernel(*refs, seq_len, has_h0, emit_state):
    it = iter(refs)
    x_ref, b_ref, c_ref, z_ref, dt_ref, dtb_ref, alog_ref, dsk_ref = (next(it) for _ in range(8))
    if has_h0:
        h0f_ref, h0b_ref = next(it), next(it)
    y_ref = next(it)
    if emit_state:
        hf_ref, hb_ref = next(it), next(it)
    bt_s, cum_s, dtv_s, y_s, st_s = (next(it) for _ in range(5))

    n_chunks = seq_len // CHUNK
    n_dir = 2 * HEADS_PER_GROUP

    lane = lax.broadcasted_iota(jnp.int32, (1, LANE), 1)
    ii = lax.broadcasted_iota(jnp.int32, (CHUNK, CHUNK), 0)
    jj = lax.broadcasted_iota(jnp.int32, (CHUNK, CHUNK), 1)
    lower = jj < ii
    diag = jj == ii
    tril = jnp.where(jj <= ii, 1.0, 0.0).astype(BF16)
    triu = jnp.where(jj >= ii, 1.0, 0.0).astype(BF16)
    er = lax.broadcasted_iota(jnp.int32, (LANE, GROUP_W), 0)
    ec = lax.broadcasted_iota(jnp.int32, (LANE, GROUP_W), 1) // SSD_HEADDIM
    exp_f = jnp.where(er == ec, 1.0, 0.0).astype(BF16)
    exp_b = jnp.where(er == ec + HEADS_PER_GROUP, 1.0, 0.0).astype(BF16)
    lo64 = lane < SSD_HEADDIM
    is_fwd = lane < HEADS_PER_GROUP
    a_neg = -jnp.exp(alog_ref[...])
    dtb = dtb_ref[...]
    dsk = dsk_ref[...]
    row_fwd = lax.broadcasted_iota(jnp.int32, (n_dir, 1), 0) < HEADS_PER_GROUP
    pad_rows = jnp.zeros((LANE - n_dir, CHUNK), F32)

    if has_h0:
        st_s[...] = h0f_ref[...].T
    else:
        st_s[...] = jnp.zeros((D_STATE, GROUP_W), F32)

    def local(cs):
        rows = [slice(c * CHUNK, (c + 1) * CHUNK) for c in cs]
        xcb = [x_ref[r, :] for r in rows]
        bcb = [b_ref[r, :] for r in rows]
        ccb = [c_ref[r, :] for r in rows]
        btb = [b.astype(F32).T.astype(BF16) for b in bcb]
        for r, b in zip(rows, btb):
            bt_s[r, :] = b
        raw = [dt_ref[r, :].T[0:n_dir, :] + dtb for r in rows]
        e = [jnp.exp(-jnp.abs(x)) for x in raw]
        log1p_e = [jnp.where(v < 1e-4, v * (1.0 - 0.5 * v), jnp.log(1.0 + v)) for v in e]
        dt_t = [jnp.maximum(x, 0.0) + l for x, l in zip(raw, log1p_e)]
        a = [d * a_neg for d in dt_t]
        cum_t = [jnp.where(row_fwd, _split_dot(v, triu, 3), _split_dot(v, tril, 3)) for v in a]
        cum = [jnp.concatenate([v, pad_rows], axis=0).T for v in cum_t]
        dtv = [jnp.concatenate([v, pad_rows], axis=0).T for v in dt_t]
        for k, r in enumerate(rows):
            cum_s[r, :] = cum[k]
            dtv_s[r, :] = dtv[k]
        g = [_dot_nt(cc, bb) for cc, bb in zip(ccb, bcb)]
        src_t = [c_t - jnp.log(d) for c_t, d in zip(cum_t, dt_t)]
        parts = [[] for _ in cs]
        for pp in range(HEADS_PER_GROUP // 2):
            for k in range(len(cs)):
                ws = []
                for hh in (2 * pp, 2 * pp + 1):
                    hb = HEADS_PER_GROUP + hh
                    seg = jnp.where(lower, cum[k][:, hh:hh + 1] - src_t[k][hh:hh + 1, :],
                                    cum[k][:, hb:hb + 1] - src_t[k][hb:hb + 1, :])
                    both = jnp.exp(seg) + jnp.where(diag, dt_t[k][hh:hh + 1, :], 0.0)
                    ws.append((g[k] * both).astype(BF16))
                wp = jnp.concatenate(ws, axis=1)
                xp = xcb[k][:, pp * LANE:(pp + 1) * LANE]
                zero = jnp.zeros_like(xp)
                rhs = jnp.concatenate([jnp.where(lo64, xp, zero), jnp.where(lo64, zero, xp)], axis=0)
                parts[k].append(_dot(wp, rhs))
        y = [jnp.concatenate(p, axis=1) for p in parts]
        ef_x = [_split_dot(jnp.exp(v), exp_f, EXPAND_TERMS) for v in cum]
        to_end = [jnp.where(is_fwd, v[CHUNK - 1:CHUNK, :] - v, 0.0) for v in cum]
        ff_x = [_split_dot(jnp.exp(t) * d, exp_f, EXPAND_TERMS) for t, d in zip(to_end, dtv)]
        return [(rows[k], xcb[k], ccb[k], btb[k], y[k], ef_x[k], ff_x[k]) for k in range(len(cs))]

    def carry_fwd(vals):
        r, xcb, ccb, btb, y, ef_x, ff_x = vals
        xc = xcb.astype(F32)
        st = st_s[...]
        y_s[r, :] = y + _dot(ccb, st.astype(BF16)) * ef_x + dsk * xc
        st_s[...] = st * ef_x[CHUNK - 1:CHUNK, :] + _dot(btb, (xc * ff_x).astype(BF16))

    step = 4 if n_chunks % 4 == 0 else (2 if n_chunks % 2 == 0 else 1)
    for c in range(0, n_chunks, step):
        for vals in local(list(range(c, c + step))):
            carry_fwd(vals)

    if emit_state:
        hf_ref[...] = st_s[...].T
    if has_h0:
        st_s[...] = h0b_ref[...].T
    else:
        st_s[...] = jnp.zeros((D_STATE, GROUP_W), F32)

    def local_bwd(cs):
        rows = [slice(c * CHUNK, (c + 1) * CHUNK) for c in cs]
        cum = [cum_s[r, :] for r in rows]
        dtv = [dtv_s[r, :] for r in rows]
        eb_x = [_split_dot(jnp.exp(v), exp_b, EXPAND_TERMS) for v in cum]
        to_start = [jnp.where(is_fwd, 0.0, v[0:1, :] - v) for v in cum]
        fb_x = [_split_dot(jnp.exp(t) * d, exp_b, EXPAND_TERMS) for t, d in zip(to_start, dtv)]
        xw = [(x_ref[r, :].astype(F32) * f).astype(BF16) for r, f in zip(rows, fb_x)]
        gate = [_silu(z_ref[r, :].astype(F32)) for r in rows]
        return [(rows[k], eb_x[k], xw[k], gate[k]) for k in range(len(cs))]

    def carry_bwd(vals):
        r, eb_x, xw, gate = vals
        st = st_s[...]
        y = y_s[r, :] + _dot(c_ref[r, :], st.astype(BF16)) * eb_x
        st_s[...] = st * eb_x[0:1, :] + _dot(bt_s[r, :], xw)
        y_ref[r, :] = (y * gate).astype(BF16)

    for c in reversed(range(0, n_chunks, step)):
        for vals in reversed(local_bwd(list(range(c, c + step)))):
            carry_bwd(vals)

    if emit_state:
        hb_ref[...] = st_s[...].T


def _ssd_mixer(pa, pb, dt32, dtb_g, alog_g, dsk, *, context, layer=0, h0f=None, h0b=None):
    seq_len = SEQ if context else DEC_SEQ
    n_seq = BATCH if context else DEC_BATCH
    has_h0 = h0f is not None
    emit_state = context
    xw_, bw_ = GROUP_W, D_STATE

    in_specs = [
        pl.BlockSpec((seq_len, xw_), lambda b, g: (b, B_X // xw_ + g)),
        pl.BlockSpec((seq_len, bw_), lambda b, g: (b, B_B // bw_ + g)),
        pl.BlockSpec((seq_len, bw_), lambda b, g: (b, B_C // bw_ + g)),
        pl.BlockSpec((seq_len, xw_), lambda b, g: (b, A_Z // xw_ + g)),
        pl.BlockSpec((seq_len, LANE), lambda b, g: (b, g)),
        pl.BlockSpec((None, 2 * HEADS_PER_GROUP, 1), lambda b, g: (g, 0, 0)),
        pl.BlockSpec((None, 2 * HEADS_PER_GROUP, 1), lambda b, g: (g, 0, 0)),
        pl.BlockSpec((1, xw_), lambda b, g: (0, g)),
    ]
    args = [pb, pb, pb, pa, dt32, dtb_g, alog_g, dsk]
    if has_h0:
        st_spec = pl.BlockSpec((None, None, xw_, D_STATE), lambda b, g: (b, layer, g, 0))
        in_specs += [st_spec, st_spec]
        args += [h0f, h0b]
    out_shape = [jax.ShapeDtypeStruct((n_seq * seq_len, SSD_INNER), BF16)]
    out_specs = [pl.BlockSpec((seq_len, xw_), lambda b, g: (b, g))]
    if emit_state:
        so = pl.BlockSpec((None, xw_, D_STATE), lambda b, g: (b, g, 0))
        out_shape += [jax.ShapeDtypeStruct((n_seq, SSD_INNER, D_STATE), F32)] * 2
        out_specs += [so, so]
    scratch = [
        pltpu.VMEM((seq_len, bw_), BF16),
        pltpu.VMEM((seq_len, LANE), F32),
        pltpu.VMEM((seq_len, LANE), F32),
        pltpu.VMEM((seq_len, xw_), F32),
        pltpu.VMEM((D_STATE, xw_), F32),
    ]
    return pl.pallas_call(
        functools.partial(_ssd_kernel, seq_len=seq_len, has_h0=has_h0, emit_state=emit_state),
        out_shape=tuple(out_shape),
        grid=(n_seq, SSD_GROUPS),
        in_specs=in_specs,
        out_specs=tuple(out_specs),
        scratch_shapes=scratch,
        compiler_params=pltpu.CompilerParams(
            dimension_semantics=("parallel", "parallel"), vmem_limit_bytes=VMEM_LIMIT),
        name="ssd_ctx" if context else "ssd_lat",
    )(*args)


def _merge_kernel(x_ref, na_ref, s_ref, ga_ref, gb_ref, mod_ref, sg_ref, wna_ref, wssd_ref, wo_ref, o_ref):
    y = s_ref[...].astype(F32)
    ms = jnp.mean(y * y, axis=-1, keepdims=True)
    yn = (y * lax.rsqrt(ms + EPS) * sg_ref[...]).astype(BF16)
    a = _dot(na_ref[...], wna_ref[...])
    s = _dot(yn, wssd_ref[...])
    mix = _sigmoid(ga_ref[...].astype(F32)) * a + _sigmoid(gb_ref[...].astype(F32)) * s
    o = _dot(mix.astype(BF16), wo_ref[...])
    o_ref[...] = x_ref[...] + mod_ref[:, 2 * D_MODEL:3 * D_MODEL] * o


def _merge(x, na, ssd, pb, mod, ssd_g, w_na, w_ssd, w_o, *, context):
    tm = TM_MERGE
    n_rows = x.shape[0]
    return pl.pallas_call(
        _merge_kernel,
        out_shape=jax.ShapeDtypeStruct((n_rows, D_MODEL), F32),
        grid=(n_rows // tm,),
        in_specs=[
            pl.BlockSpec((tm, D_MODEL), lambda i: (i, 0)),
            pl.BlockSpec((tm, NA_WIDTH), lambda i: (i, 0)),
            pl.BlockSpec((tm, SSD_INNER), lambda i: (i, 0)),
            pl.BlockSpec((tm, D_MODEL), lambda i: (i, B_GA // D_MODEL)),
            pl.BlockSpec((tm, D_MODEL), lambda i: (i, B_GB // D_MODEL)),
            pl.BlockSpec((None, 1, 6 * D_MODEL), lambda i: (_mod_row(i, tm, context), 0, 0)),
            pl.BlockSpec((1, SSD_INNER), lambda i: (0, 0)),
            _resident((NA_WIDTH, D_MODEL)),
            _resident((SSD_INNER, D_MODEL)),
            _resident((D_MODEL, D_MODEL)),
        ],
        out_specs=pl.BlockSpec((tm, D_MODEL), lambda i: (i, 0)),
        compiler_params=pltpu.CompilerParams(
            dimension_semantics=("parallel",), vmem_limit_bytes=VMEM_LIMIT),
        name="merge_ctx" if context else "merge_lat",
    )(x, na, ssd, pb, pb, mod, ssd_g, w_na, w_ssd, w_o)


def _ffn_kernel(x_ref, xp_ref, xn_ref, mod_ref, g_ref, wup_ref, cw_ref, cb_ref, wd_ref,
                o_ref, h_ref, act_ref, *, tm, seq_len):
    i = pl.program_id(0)
    g = g_ref[...]
    shift = mod_ref[:, 3 * D_MODEL:4 * D_MODEL]
    scale = mod_ref[:, 4 * D_MODEL:5 * D_MODEL]
    h_ref[0:tm, :] = _norm_mod(x_ref[...], g, shift, scale)
    h_ref[tm:tm + 2 * SUBLANE, :] = _norm_mod(
        jnp.concatenate([xp_ref[...], xn_ref[...]], axis=0), g, shift, scale)

    ext = tm + 2 * SUBLANE
    start = i * tm
    row = lax.broadcasted_iota(jnp.int32, (tm, 1), 0)
    pos = (start + row) & (seq_len - 1)
    inner_edges = seq_len < tm
    keep_before = (start & (seq_len - 1)) != 0
    keep_after = ((start + tm) & (seq_len - 1)) != 0

    def conv(c0):
        cols = slice(c0, c0 + FF_TILE)
        pre = _dot(h_ref[...], wup_ref[:, cols])
        before = jnp.where(keep_before, pre[tm:tm + SUBLANE], 0.0)
        after = jnp.where(keep_after, pre[tm + SUBLANE:ext], 0.0)
        v = jnp.concatenate([before, pre[0:tm], after], axis=0)
        left = pltpu.roll(v, 1, 0)[SUBLANE:SUBLANE + tm]
        right = pltpu.roll(v, ext - 1, 0)[SUBLANE:SUBLANE + tm]
        if inner_edges:
            left = jnp.where(pos != 0, left, 0.0)
            right = jnp.where(pos != seq_len - 1, right, 0.0)
        cw = cw_ref[:, cols]
        return cw[0:1] * left + cw[1:2] * pre[0:tm] + cw[2:3] * right + cb_ref[:, cols]

    for c0 in range(0, D_FF, FF_TILE):
        val = conv(c0)
        gate = conv(D_FF + c0)
        act_ref[:, c0:c0 + FF_TILE] = (_silu(gate) * val).astype(BF16)

    acc = _dot(act_ref[:, 0:FF_TILE], wd_ref[0:FF_TILE, :])
    for c0 in range(FF_TILE, D_FF, FF_TILE):
        acc = acc + _dot(act_ref[:, c0:c0 + FF_TILE], wd_ref[c0:c0 + FF_TILE, :])
    o_ref[...] = x_ref[...] + mod_ref[:, 5 * D_MODEL:6 * D_MODEL] * acc


def _conv_ffn(x, mod, g2, w_up, conv_w, conv_b, w_down, *, context):
    tm = TM_FFN
    n_rows = x.shape[0]
    hb = tm // SUBLANE
    last_h = n_rows // SUBLANE - 1
    return pl.pallas_call(
        functools.partial(_ffn_kernel, tm=tm, seq_len=SEQ if context else DEC_SEQ),
        out_shape=jax.ShapeDtypeStruct((n_rows, D_MODEL), F32),
        grid=(n_rows // tm,),
        in_specs=[
            pl.BlockSpec((tm, D_MODEL), lambda i: (i, 0)),
            pl.BlockSpec((SUBLANE, D_MODEL), lambda i: (jnp.maximum(i * hb - 1, 0), 0)),
            pl.BlockSpec((SUBLANE, D_MODEL), lambda i: (jnp.minimum((i + 1) * hb, last_h), 0)),
            pl.BlockSpec((None, 1, 6 * D_MODEL), lambda i: (_mod_row(i, tm, context), 0, 0)),
            pl.BlockSpec((1, D_MODEL), lambda i: (0, 0)),
            _resident((D_MODEL, 2 * D_FF)),
            _resident((FFN_CONV, 2 * D_FF)),
            _resident((1, 2 * D_FF)),
            _resident((D_FF, D_MODEL)),
        ],
        out_specs=pl.BlockSpec((tm, D_MODEL), lambda i: (i, 0)),
        scratch_shapes=[
            pltpu.VMEM((tm + 2 * SUBLANE, D_MODEL), BF16),
            pltpu.VMEM((tm, D_FF), BF16),
        ],
        compiler_params=pltpu.CompilerParams(
            dimension_semantics=("parallel",), vmem_limit_bytes=VMEM_LIMIT),
        name="conv_ffn_ctx" if context else "conv_ffn_lat",
    )(x, x, x, mod, g2, w_up, conv_w, conv_b, w_down)


def _w_in_group_b(w):
    x0 = A_W
    dt0 = x0 + CONV_DIM
    ga0 = dt0 + 2 * SSD_HEADS
    dt = w[:, dt0:ga0]
    hg = HEADS_PER_GROUP
    blocks = []
    for g in range(SSD_GROUPS):
        blocks += [dt[:, g * hg:(g + 1) * hg], dt[:, SSD_HEADS + g * hg:SSD_HEADS + (g + 1) * hg],
                   jnp.zeros((D_MODEL, LANE - 2 * hg), w.dtype)]
    return jnp.concatenate([w[:, x0:dt0], w[:, ga0:]] + blocks, axis=1).astype(BF16)


def _by_group(v):
    v = v.astype(F32).reshape(2, SSD_GROUPS, HEADS_PER_GROUP)
    return jnp.concatenate([v[0], v[1]], axis=1)[:, :, None]


def kernel(x_prompt, x_sample, c, cache_k, cache_v, state_ssd_fwd, state_ssd_bwd, c_ctx, w_ada, b_ada, norm1_g, w_in, q_norm_g, k_norm_g, rpb, ssd_conv_w, ssd_conv_b, a_log, dt_bias, d_skip, ssd_norm_g, w_na_out, w_ssd_out, w_o, norm2_g, w_up, ffn_conv_w, ffn_conv_b, w_down):
    xs = {True: x_prompt.reshape(N_CTX, D_MODEL), False: x_sample.reshape(N_LAT, D_MODEL)}
    cond = jnp.concatenate([c_ctx[None, :], c, jnp.zeros((16 - 1 - DEC_BATCH, D_MODEL), F32)], axis=0)
    mod = _modulation(cond, w_ada, b_ada.reshape(DEPTH, 1, 6 * D_MODEL))
    mod = mod.reshape(DEPTH, 16, 1, 6 * D_MODEL)

    ck = cache_k.reshape(DEC_BATCH, DEPTH, PAST_LEN, NA_WIDTH)
    cv = cache_v.reshape(DEC_BATCH, DEPTH, PAST_LEN, NA_WIDTH)
    h0f = state_ssd_fwd.reshape(DEC_BATCH, DEPTH, SSD_INNER, D_STATE)
    h0b = state_ssd_bwd.reshape(DEC_BATCH, DEPTH, SSD_INNER, D_STATE)
    bd = (jnp.arange(MXU_N)[:, None] // HEAD_DIM == jnp.arange(MXU_N)[None, :] // HEAD_DIM).astype(BF16)

    new_k, new_v, new_hf, new_hb = [], [], [], []
    for l in range(DEPTH):
        qkg = jnp.concatenate([jnp.tile(q_norm_g[l] * (HEAD_DIM ** -0.5 * LOG2E), NA_HEADS),
                               jnp.tile(k_norm_g[l], NA_HEADS)])[None, :]
        g1 = norm1_g[l][None, :]
        w_a = w_in[l][:, :A_W].astype(BF16)
        w_b = _w_in_group_b(w_in[l])
        conv_b = ssd_conv_b[l][None, :]
        pa_ctx, k32, v32 = _in_projection_a(xs[True], mod[l], g1, w_a, qkg, bd, context=True)
        (pa_lat,) = _in_projection_a(xs[False], mod[l], g1, w_a, qkg, bd, context=False)
        pb_ctx, dt_ctx = _in_projection_b(xs[True], mod[l], g1, w_b, ssd_conv_w[l], conv_b, context=True)
        pb_lat, dt_lat = _in_projection_b(xs[False], mod[l], g1, w_b, ssd_conv_w[l], conv_b, context=False)
        new_k.append(k32.reshape(BATCH, SEQ, NA_HEADS, HEAD_DIM))
        new_v.append(v32.reshape(BATCH, SEQ, NA_HEADS, HEAD_DIM))

        na_ctx = _context_attention(pa_ctx)
        na_lat = _neighbourhood_attention(pa_lat, ck, cv, _na_bias(_rpb_reversed(rpb[l])), l)

        dtb_g = _by_group(dt_bias[l])
        alog_g = _by_group(a_log[l])
        dsk = jnp.repeat(d_skip[l], SSD_HEADDIM)[None, :]
        ssd_ctx, hf, hb = _ssd_mixer(pa_ctx, pb_ctx, dt_ctx, dtb_g, alog_g, dsk, context=True)
        (ssd_lat,) = _ssd_mixer(pa_lat, pb_lat, dt_lat, dtb_g, alog_g, dsk, context=False,
                                layer=l, h0f=h0f, h0b=h0b)
        new_hf.append(hf.reshape(BATCH, SSD_HEADS, SSD_HEADDIM, D_STATE))
        new_hb.append(hb.reshape(BATCH, SSD_HEADS, SSD_HEADDIM, D_STATE))

        branch = {True: (na_ctx, ssd_ctx, pb_ctx), False: (na_lat, ssd_lat, pb_lat)}
        w_merge = (w_na_out[l].astype(BF16), w_ssd_out[l].astype(BF16), w_o[l].astype(BF16))
        w_ffn = (w_up[l].astype(BF16), ffn_conv_w[l], ffn_conv_b[l][None, :], w_down[l].astype(BF16))
        for ctx in (True, False):
            na, ssd, pb = branch[ctx]
            x1 = _merge(xs[ctx], na, ssd, pb, mod[l], ssd_norm_g[l][None, :], *w_merge, context=ctx)
            xs[ctx] = _conv_ffn(x1, mod[l], norm2_g[l][None, :], *w_ffn, context=ctx)

    y_prompt = xs[True].reshape(BATCH, SEQ, D_MODEL)
    y_sample = xs[False].reshape(DEC_BATCH, DEC_SEQ, D_MODEL)
    return (y_prompt, y_sample, jnp.stack(new_k, axis=1), jnp.stack(new_v, axis=1),
            jnp.stack(new_hf, axis=1), jnp.stack(new_hb, axis=1))
```

```python
import functools

import jax
import jax.numpy as jnp
from jax import lax
from jax.experimental import pallas as pl
from jax.experimental.pallas import tpu as pltpu

D_MODEL = 1024
BATCH = 16
SEQ = 256
DEPTH = 2
DEC_BATCH = 8
DEC_SEQ = 2048
PAST_LEN = 256
GRID_W = 64
NA_HEADS = 16
HEAD_DIM = 64
NA_WIDTH = NA_HEADS * HEAD_DIM
WIN_R = 8
WIN_C = 16
SSD_INNER = 2 * D_MODEL
SSD_HEADDIM = 64
SSD_HEADS = SSD_INNER // SSD_HEADDIM
SSD_GROUPS = 4
D_STATE = 128
SSD_CONV = 4
CHUNK = 128
CONV_DIM = SSD_INNER + 2 * SSD_GROUPS * D_STATE
D_FF = 2816
FFN_CONV = 3
EPS = 1e-6

N_CTX = BATCH * SEQ
N_LAT = DEC_BATCH * DEC_SEQ
N_TOK = N_CTX + N_LAT
GRID_ROWS = DEC_SEQ // GRID_W
HEADS_PER_GROUP = SSD_HEADS // SSD_GROUPS
GROUP_W = HEADS_PER_GROUP * SSD_HEADDIM

LANE = 128
SUBLANE = 8
BF16_ROWS = 16
MXU_N = 256

A_Q, A_K, A_V, A_Z, A_W = 0, 1024, 2048, 3072, 5120
B_X, B_B, B_C, B_GA, B_GB, B_W = 0, 2048, 2560, 3072, 4096, 5120
DT_W = SSD_GROUPS * LANE
IN_TILE_N = 2 * MXU_N
TM_IN = 512
TM_MERGE = 512
TM_FFN = 512
FF_TILE = 256
CAST_ROWS = 128
MASK_NEG = -1e30
EXPAND_TERMS = 1
LOG2E = 1.4426950408889634
QK_SQ_TERMS = 1

NA_QROWS = 4
NA_KROWS = 12
NA_Q = NA_QROWS * GRID_W
NA_KW = NA_KROWS * GRID_W
NA_KBLK = 256
NA_STEPS = GRID_ROWS // NA_QROWS
NA_VARIANTS = 3
NA_LOCKSTEP = 4

F32 = jnp.float32
BF16 = jnp.bfloat16
NT = (((1,), (1,)), ((), ()))

VMEM_LIMIT = 56 * 1024 * 1024


def _sigmoid(x):
    return 0.5 + 0.5 * jnp.tanh(0.5 * x)


def _silu(x):
    h = 0.5 * x
    return h + h * jnp.tanh(h)


def _dot(a, b):
    return jnp.dot(a, b, preferred_element_type=F32)


def _dot_nt(a, b):
    return lax.dot_general(a, b, NT, preferred_element_type=F32)


def _split_dot(a, b, terms):
    acc = None
    r = a
    for t in range(terms):
        p = r.astype(BF16)
        d = _dot(p, b)
        acc = d if acc is None else acc + d
        if t + 1 < terms:
            r = r - p.astype(F32)
    return acc


def _mod_row(i, tm, context):
    return 0 if context else 1 + (i * tm) // DEC_SEQ


def _resident(shape, layer=None):
    if layer is None:
        return pl.BlockSpec(shape, lambda *_: (0,) * len(shape), pipeline_mode=pl.Buffered(1))
    return pl.BlockSpec((None,) + shape, lambda *_: (layer,) + (0,) * len(shape), pipeline_mode=pl.Buffered(1))


def _norm_mod(xv, g, shift, scale):
    ms = jnp.mean(xv * xv, axis=-1, keepdims=True)
    return (xv * lax.rsqrt(ms + EPS) * g * (1.0 + scale) + shift).astype(BF16)


def _mod_kernel(c_ref, w_ref, b_ref, o_ref):
    s = _silu(c_ref[...]).astype(BF16)
    o_ref[...] = _dot(s, w_ref[...].astype(BF16)) + b_ref[...]


def _modulation(cond, w_ada, b_ada):
    tn = 1536
    return pl.pallas_call(
        _mod_kernel,
        out_shape=jax.ShapeDtypeStruct((DEPTH, 16, 6 * D_MODEL), F32),
        grid=(DEPTH, 6 * D_MODEL // tn),
        in_specs=[
            pl.BlockSpec((16, D_MODEL), lambda l, j: (0, 0)),
            pl.BlockSpec((None, D_MODEL, tn), lambda l, j: (l, 0, j)),
            pl.BlockSpec((None, 1, tn), lambda l, j: (l, 0, j)),
        ],
        out_specs=pl.BlockSpec((None, 16, tn), lambda l, j: (l, 0, j)),
        compiler_params=pltpu.CompilerParams(
            dimension_semantics=("parallel", "parallel"), vmem_limit_bytes=VMEM_LIMIT),
        name="adaln_mod",
    )(cond, w_ada, b_ada)


def _inproj_a_kernel(x_ref, mod_ref, g_ref, w_ref, qkg_ref, bd_ref, *rest, emit_kv):
    if emit_kv:
        p_ref, k_ref, v_ref, h_ref = rest
    else:
        p_ref, h_ref = rest
    h_ref[...] = _norm_mod(x_ref[...], g_ref[...], mod_ref[:, 0:D_MODEL], mod_ref[:, D_MODEL:2 * D_MODEL])
    for c0 in range(0, A_W, IN_TILE_N):
        cols = slice(c0, c0 + IN_TILE_N)
        acc = _dot(h_ref[...], w_ref[:, cols])
        if c0 < A_V:
            sq = acc * acc
            ss = jnp.concatenate(
                [_split_dot(sq[:, t:t + MXU_N], bd_ref[...], QK_SQ_TERMS) for t in range(0, IN_TILE_N, MXU_N)],
                axis=1)
            acc = acc * lax.rsqrt(ss * (1.0 / HEAD_DIM) + EPS) * qkg_ref[:, cols]
        p_ref[:, cols] = acc.astype(BF16)
        if emit_kv and A_K <= c0 < A_V:
            k_ref[:, c0 - A_K:c0 - A_K + IN_TILE_N] = acc
        if emit_kv and A_V <= c0 < A_Z:
            v_ref[:, c0 - A_V:c0 - A_V + IN_TILE_N] = acc


def _in_projection_a(x, mod, g1, w, qkg, bd, *, context, layer):
    tm = TM_IN
    n_rows = x.shape[0]
    out_shape = [jax.ShapeDtypeStruct((n_rows, A_W), BF16)]
    out_specs = [pl.BlockSpec((tm, A_W), lambda i: (i, 0))]
    if context:
        out_shape += [jax.ShapeDtypeStruct((n_rows, NA_WIDTH), F32)] * 2
        out_specs += [pl.BlockSpec((tm, NA_WIDTH), lambda i: (i, 0))] * 2
    return pl.pallas_call(
        functools.partial(_inproj_a_kernel, emit_kv=context),
        out_shape=tuple(out_shape),
        grid=(n_rows // tm,),
        in_specs=[
            pl.BlockSpec((tm, D_MODEL), lambda i: (i, 0)),
            pl.BlockSpec((None, 1, 6 * D_MODEL), lambda i: (_mod_row(i, tm, context), 0, 0)),
            pl.BlockSpec((1, D_MODEL), lambda i: (0, 0)),
            _resident((D_MODEL, A_W), layer),
            pl.BlockSpec((1, A_V), lambda i: (0, 0)),
            pl.BlockSpec((MXU_N, MXU_N), lambda i: (0, 0)),
        ],
        out_specs=tuple(out_specs),
        scratch_shapes=[pltpu.VMEM((tm, D_MODEL), BF16)],
        compiler_params=pltpu.CompilerParams(
            dimension_semantics=("parallel",), vmem_limit_bytes=VMEM_LIMIT),
        name="in_proj_a_ctx" if context else "in_proj_a_lat",
    )(x, mod, g1, w, qkg, bd)


def _inproj_b_kernel(x_ref, xp_ref, xn_ref, mod_ref, g_ref, w_ref, cw_ref, cb_ref,
                     p_ref, dt_ref, h_ref, *, tm, seq_len):
    i = pl.program_id(0)
    g = g_ref[...]
    shift = mod_ref[:, 0:D_MODEL]
    scale = mod_ref[:, D_MODEL:2 * D_MODEL]
    h_ref[0:tm, :] = _norm_mod(x_ref[...], g, shift, scale)
    h_ref[tm:tm + 2 * SUBLANE, :] = _norm_mod(
        jnp.concatenate([xp_ref[...], xn_ref[...]], axis=0), g, shift, scale)

    ext = tm + 2 * SUBLANE
    start = i * tm
    row = lax.broadcasted_iota(jnp.int32, (tm, 1), 0)
    pos = (start + row) & (seq_len - 1)
    inner_edges = seq_len < tm
    keep_before = (start & (seq_len - 1)) != 0
    keep_after = ((start + tm) & (seq_len - 1)) != 0
    for c0 in range(0, B_GA, IN_TILE_N):
        cols = slice(c0, c0 + IN_TILE_N)
        pre = _dot(h_ref[...], w_ref[:, cols])
        before = jnp.where(keep_before, pre[tm:tm + SUBLANE], 0.0)
        after = jnp.where(keep_after, pre[tm + SUBLANE:ext], 0.0)
        v = jnp.concatenate([before, pre[0:tm], after], axis=0)
        back2 = pltpu.roll(v, 2, 0)[SUBLANE:SUBLANE + tm]
        back1 = pltpu.roll(v, 1, 0)[SUBLANE:SUBLANE + tm]
        ahead = pltpu.roll(v, ext - 1, 0)[SUBLANE:SUBLANE + tm]
        if inner_edges:
            back2 = jnp.where(pos >= 2, back2, 0.0)
            back1 = jnp.where(pos >= 1, back1, 0.0)
            ahead = jnp.where(pos != seq_len - 1, ahead, 0.0)
        cw = cw_ref[:, cols]
        u = cw[0:1] * back2 + cw[1:2] * back1 + cw[2:3] * pre[0:tm] + cw[3:4] * ahead + cb_ref[:, cols]
        p_ref[:, cols] = _silu(u).astype(BF16)
    for c0 in range(B_GA, B_W, IN_TILE_N):
        cols = slice(c0, c0 + IN_TILE_N)
        p_ref[:, cols] = _dot(h_ref[0:tm, :], w_ref[:, cols]).astype(BF16)
    dt_ref[...] = _dot(h_ref[0:tm, :], w_ref[:, B_W:B_W + DT_W])


def _in_projection_b(x, mod, g1, w, conv_w, conv_b, *, context):
    tm = TM_IN
    n_rows = x.shape[0]
    hb = tm // SUBLANE
    last_h = n_rows // SUBLANE - 1
    return pl.pallas_call(
        functools.partial(_inproj_b_kernel, tm=tm, seq_len=SEQ if context else DEC_SEQ),
        out_shape=(jax.ShapeDtypeStruct((n_rows, B_W), BF16), jax.ShapeDtypeStruct((n_rows, DT_W), F32)),
        grid=(n_rows // tm,),
        in_specs=[
            pl.BlockSpec((tm, D_MODEL), lambda i: (i, 0)),
            pl.BlockSpec((SUBLANE, D_MODEL), lambda i: (jnp.maximum(i * hb - 1, 0), 0)),
            pl.BlockSpec((SUBLANE, D_MODEL), lambda i: (jnp.minimum((i + 1) * hb, last_h), 0)),
            pl.BlockSpec((None, 1, 6 * D_MODEL), lambda i: (_mod_row(i, tm, context), 0, 0)),
            pl.BlockSpec((1, D_MODEL), lambda i: (0, 0)),
            _resident((D_MODEL, B_W + DT_W)),
            pl.BlockSpec((SSD_CONV, CONV_DIM), lambda i: (0, 0)),
            pl.BlockSpec((1, CONV_DIM), lambda i: (0, 0)),
        ],
        out_specs=(pl.BlockSpec((tm, B_W), lambda i: (i, 0)), pl.BlockSpec((tm, DT_W), lambda i: (i, 0))),
        scratch_shapes=[pltpu.VMEM((tm + 2 * SUBLANE, D_MODEL), BF16)],
        compiler_params=pltpu.CompilerParams(
            dimension_semantics=("parallel",), vmem_limit_bytes=VMEM_LIMIT),
        name="in_proj_b_ctx" if context else "in_proj_b_lat",
    )(x, x, x, mod, g1, w, conv_w, conv_b)


def _head_masks():
    lane = lax.broadcasted_iota(jnp.int32, (1, LANE), 1)
    return lane < HEAD_DIM


def _ctx_attn_kernel(q_ref, k_ref, v_ref, o_ref):
    lo = _head_masks()
    for c0 in range(0, NA_WIDTH, LANE):
        cols = slice(c0, c0 + LANE)
        q = q_ref[:, cols]
        k = k_ref[:, cols]
        v = v_ref[:, cols]
        outs = []
        for hh in range(2):
            m = lo if hh == 0 else jnp.logical_not(lo)
            qm = jnp.where(m, q, jnp.zeros_like(q))
            s = _dot_nt(qm, k)
            mx = jnp.max(s, axis=-1, keepdims=True)
            p = jnp.exp2(s - mx)
            l = jnp.sum(p, axis=-1, keepdims=True)
            outs.append(_dot(p.astype(BF16), v) / l)
        o_ref[:, cols] = jnp.where(lo, outs[0], outs[1]).astype(BF16)


def _context_attention(pa):
    blk = (SEQ, NA_WIDTH)
    return pl.pallas_call(
        _ctx_attn_kernel,
        out_shape=jax.ShapeDtypeStruct((N_CTX, NA_WIDTH), BF16),
        grid=(BATCH,),
        in_specs=[
            pl.BlockSpec(blk, lambda b: (b, A_Q // NA_WIDTH)),
            pl.BlockSpec(blk, lambda b: (b, A_K // NA_WIDTH)),
            pl.BlockSpec(blk, lambda b: (b, A_V // NA_WIDTH)),
        ],
        out_specs=pl.BlockSpec(blk, lambda b: (b, 0)),
        compiler_params=pltpu.CompilerParams(
            dimension_semantics=("parallel",), vmem_limit_bytes=VMEM_LIMIT),
        name="ctx_attn",
    )(pa, pa, pa)


def _na_bias_kernel(r_ref, o_ref):
    ck = lax.broadcasted_iota(jnp.int32, (GRID_W, LANE), 0)
    ln = lax.broadcasted_iota(jnp.int32, (GRID_W, LANE), 1)
    cq = ln & (GRID_W - 1)
    cs = jnp.clip(cq - WIN_C // 2, 0, GRID_W - WIN_C)
    col_ok = jnp.logical_and(ck >= cs, ck < cs + WIN_C)
    left = ln < GRID_W
    n_dr = 2 * WIN_R - 1
    toep = []
    for dr in range(n_dr):
        base = jnp.broadcast_to(r_ref[dr:dr + 1, :] * LOG2E, (GRID_W, LANE))
        toep.append([pltpu.roll(base, s * GRID_W, 1, stride=1, stride_axis=0) for s in range(2)])
    neg = jnp.full((GRID_W, LANE), MASK_NEG, F32)

    def rel(v, j, q4):
        if v == 0:
            return j - q4 + WIN_R - 1, j < WIN_R
        if v == 1:
            return j - q4 + WIN_R // 2 - 1, q4 <= j < q4 + WIN_R
        return j - q4 - 1, j >= NA_KROWS - WIN_R

    for v in range(NA_VARIANTS):
        for j in range(NA_KROWS):
            for lb in range(NA_QROWS // 2):
                halves = []
                for s in range(2):
                    dr, ok = rel(v, j, 2 * lb + s)
                    halves.append(jnp.where(col_ok, toep[dr][s], neg) if ok else neg)
                o_ref[v, j * GRID_W:(j + 1) * GRID_W, lb * LANE:(lb + 1) * LANE] = (
                    jnp.where(left, halves[0], halves[1]))


def _na_bias(rrev):
    return pl.pallas_call(
        _na_bias_kernel,
        out_shape=jax.ShapeDtypeStruct((NA_HEADS // 2, NA_VARIANTS, NA_KW, 2 * NA_Q), F32),
        grid=(NA_HEADS,),
        in_specs=[pl.BlockSpec((None, 2 * WIN_R, LANE), lambda h: (h, 0, 0))],
        out_specs=pl.BlockSpec((None, NA_VARIANTS, NA_KW, NA_Q), lambda h: (h // 2, 0, 0, h % 2)),
        compiler_params=pltpu.CompilerParams(
            dimension_semantics=("parallel",), vmem_limit_bytes=VMEM_LIMIT),
        name="na_bias",
    )(rrev)


def _na_kernel(q_ref, k_ref, v_ref, kc_ref, vc_ref, bias_ref, o_ref, vt_ref):
    lo = _head_masks()
    kc = kc_ref[...].astype(BF16)
    vct = vc_ref[...].T.astype(BF16)
    for t in range(DEC_SEQ // NA_KBLK):
        vt_ref[t] = v_ref[t * NA_KBLK:(t + 1) * NA_KBLK, :].astype(F32).T.astype(BF16)
    n_wblk = NA_KW // NA_KBLK

    def steps(ms):
        n = range(len(ms))
        blk0 = [min(max(m - 1, 0), DEC_SEQ // NA_KBLK - n_wblk) for m in ms]
        var = [0 if m == 0 else (2 if m == NA_STEPS - 1 else 1) for m in ms]
        qg = [q_ref[m * NA_Q:(m + 1) * NA_Q, :] for m in ms]
        kw = [k_ref[b * NA_KBLK:b * NA_KBLK + NA_KW, :] for b in blk0]
        qcat = [jnp.concatenate([jnp.where(lo, q, jnp.zeros_like(q)), jnp.where(lo, jnp.zeros_like(q), q)], axis=0)
                for q in qg]
        sw = [_dot_nt(kw[i], qcat[i]) + bias_ref[var[i]] for i in n]
        sc = [_dot_nt(kc, qcat[i]) for i in n]
        mx = [jnp.maximum(jnp.max(sw[i], axis=0, keepdims=True), jnp.max(sc[i], axis=0, keepdims=True)) for i in n]
        pw = [jnp.exp2(sw[i] - mx[i]) for i in n]
        pc = [jnp.exp2(sc[i] - mx[i]) for i in n]
        l = [jnp.sum(pw[i], axis=0, keepdims=True) + jnp.sum(pc[i], axis=0, keepdims=True) for i in n]
        pwb = [p.astype(BF16) for p in pw]
        o = [_dot(vct, p.astype(BF16)) for p in pc]
        for t in range(n_wblk):
            o = [o[i] + _dot(vt_ref[blk0[i] + t], pwb[i][t * NA_KBLK:(t + 1) * NA_KBLK, :]) for i in n]
        for i in n:
            on = o[i] / l[i]
            both = jnp.concatenate([on[0:HEAD_DIM, 0:NA_Q], on[HEAD_DIM:2 * HEAD_DIM, NA_Q:2 * NA_Q]], axis=0)
            o_ref[ms[i] * NA_Q:(ms[i] + 1) * NA_Q, :] = both.T.astype(BF16)

    for m in range(0, NA_STEPS, NA_LOCKSTEP):
        steps(list(range(m, m + NA_LOCKSTEP)))


def _neighbourhood_attention(pa, cache_k, cache_v, bias, layer):
    hp = NA_HEADS // 2
    blk = (DEC_SEQ, LANE)
    cblk = (None, None, PAST_LEN, LANE)
    return pl.pallas_call(
        _na_kernel,
        out_shape=jax.ShapeDtypeStruct((N_LAT, NA_WIDTH), BF16),
        grid=(hp, DEC_BATCH),
        in_specs=[
            pl.BlockSpec(blk, lambda h, b: (b, A_Q // LANE + h)),
            pl.BlockSpec(blk, lambda h, b: (b, A_K // LANE + h)),
            pl.BlockSpec(blk, lambda h, b: (b, A_V // LANE + h)),
            pl.BlockSpec(cblk, lambda h, b: (b, layer, 0, h)),
            pl.BlockSpec(cblk, lambda h, b: (b, layer, 0, h)),
            pl.BlockSpec((None, NA_VARIANTS, NA_KW, 2 * NA_Q), lambda h, b: (h, 0, 0, 0)),
        ],
        out_specs=pl.BlockSpec(blk, lambda h, b: (b, h)),
        scratch_shapes=[pltpu.VMEM((DEC_SEQ // NA_KBLK, LANE, NA_KBLK), BF16)],
        compiler_params=pltpu.CompilerParams(
            dimension_semantics=("parallel", "parallel"), vmem_limit_bytes=VMEM_LIMIT),
        name="na_attn",
    )(pa, pa, pa, cache_k, cache_v, bias)


def _rpb_reversed(rpb_l):
    n = WIN_C - 1
    pos = rpb_l[:, :, n::-1]
    neg = rpb_l[:, :, :n:-1]
    z = jnp.zeros(rpb_l.shape[:2] + (LANE - 2 * n - 1,), F32)
    r = jnp.concatenate([pos.astype(F32), z, neg.astype(F32)], axis=-1)
    return jnp.pad(r, ((0, 0), (0, 1), (0, 0)))


def _ssd_kernel(*refs, seq_len, has_h0, emit_state):
    it = iter(refs)
    x_ref, b_ref, c_ref, z_ref, dt_ref, dtb_ref, alog_ref, dsk_ref = (next(it) for _ in range(8))
    if has_h0:
        h0f_ref, h0b_ref = next(it), next(it)
    y_ref = next(it)
    if emit_state:
        hf_ref, hb_ref = next(it), next(it)
    bt_s, cum_s, dtv_s, y_s, st_s = (next(it) for _ in range(5))

    n_chunks = seq_len // CHUNK
    n_dir = 2 * HEADS_PER_GROUP

    lane = lax.broadcasted_iota(jnp.int32, (1, LANE), 1)
    ii = lax.broadcasted_iota(jnp.int32, (CHUNK, CHUNK), 0)
    jj = lax.broadcasted_iota(jnp.int32, (CHUNK, CHUNK), 1)
    lower = jj < ii
    diag = jj == ii
    tril = jnp.where(jj <= ii, 1.0, 0.0).astype(BF16)
    triu = jnp.where(jj >= ii, 1.0, 0.0).astype(BF16)
    er = lax.broadcasted_iota(jnp.int32, (LANE, GROUP_W), 0)
    ec = lax.broadcasted_iota(jnp.int32, (LANE, GROUP_W), 1) // SSD_HEADDIM
    exp_f = jnp.where(er == ec, 1.0, 0.0).astype(BF16)
    exp_b = jnp.where(er == ec + HEADS_PER_GROUP, 1.0, 0.0).astype(BF16)
    lo64 = lane < SSD_HEADDIM
    is_fwd = lane < HEADS_PER_GROUP
    a_neg = -jnp.exp(alog_ref[...])
    dtb = dtb_ref[...]
    dsk = dsk_ref[...]
    row_fwd = lax.broadcasted_iota(jnp.int32, (n_dir, 1), 0) < HEADS_PER_GROUP
    pad_rows = jnp.zeros((LANE - n_dir, CHUNK), F32)

    if has_h0:
        st_s[...] = h0f_ref[...].T
    else:
        st_s[...] = jnp.zeros((D_STATE, GROUP_W), F32)

    def local(cs):
        rows = [slice(c * CHUNK, (c + 1) * CHUNK) for c in cs]
        xcb = [x_ref[r, :] for r in rows]
        bcb = [b_ref[r, :] for r in rows]
        ccb = [c_ref[r, :] for r in rows]
        btb = [b.astype(F32).T.astype(BF16) for b in bcb]
        for r, b in zip(rows, btb):
            bt_s[r, :] = b
        raw = [dt_ref[r, :].T[0:n_dir, :] + dtb for r in rows]
        e = [jnp.exp(-jnp.abs(x)) for x in raw]
        log1p_e = [jnp.where(v < 1e-4, v * (1.0 - 0.5 * v), jnp.log(1.0 + v)) for v in e]
        dt_t = [jnp.maximum(x, 0.0) + l for x, l in zip(raw, log1p_e)]
        a = [d * a_neg for d in dt_t]
        cum_t = [jnp.where(row_fwd, _split_dot(v, triu, 3), _split_dot(v, tril, 3)) for v in a]
        cum = [jnp.concatenate([v, pad_rows], axis=0).T for v in cum_t]
        dtv = [jnp.concatenate([v, pad_rows], axis=0).T for v in dt_t]
        for k, r in enumerate(rows):
            cum_s[r, :] = cum[k]
            dtv_s[r, :] = dtv[k]
        g = [_dot_nt(cc, bb) for cc, bb in zip(ccb, bcb)]
        src_t = [c_t - jnp.log(d) for c_t, d in zip(cum_t, dt_t)]
        parts = [[] for _ in cs]
        for pp in range(HEADS_PER_GROUP // 2):
            for k in range(len(cs)):
                ws = []
                for hh in (2 * pp, 2 * pp + 1):
                    hb = HEADS_PER_GROUP + hh
                    seg = jnp.where(lower, cum[k][:, hh:hh + 1] - src_t[k][hh:hh + 1, :],
                                    cum[k][:, hb:hb + 1] - src_t[k][hb:hb + 1, :])
                    both = jnp.exp(seg) + jnp.where(diag, dt_t[k][hh:hh + 1, :], 0.0)
                    ws.append((g[k] * both).astype(BF16))
                wp = jnp.concatenate(ws, axis=1)
                xp = xcb[k][:, pp * LANE:(pp + 1) * LANE]
                zero = jnp.zeros_like(xp)
                rhs = jnp.concatenate([jnp.where(lo64, xp, zero), jnp.where(lo64, zero, xp)], axis=0)
                parts[k].append(_dot(wp, rhs))
        y = [jnp.concatenate(p, axis=1) for p in parts]
        ef_x = [_split_dot(jnp.exp(v), exp_f, EXPAND_TERMS) for v in cum]
        to_end = [jnp.where(is_fwd, v[CHUNK - 1:CHUNK, :] - v, 0.0) for v in cum]
        ff_x = [_split_dot(jnp.exp(t) * d, exp_f, EXPAND_TERMS) for t, d in zip(to_end, dtv)]
        return [(rows[k], xcb[k], ccb[k], btb[k], y[k], ef_x[k], ff_x[k]) for k in range(len(cs))]

    def carry_fwd(vals):
        r, xcb, ccb, btb, y, ef_x, ff_x = vals
        xc = xcb.astype(F32)
        st = st_s[...]
        y_s[r, :] = y + _dot(ccb, st.astype(BF16)) * ef_x + dsk * xc
        st_s[...] = st * ef_x[CHUNK - 1:CHUNK, :] + _dot(btb, (xc * ff_x).astype(BF16))

    step = 4 if n_chunks % 4 == 0 else (2 if n_chunks % 2 == 0 else 1)
    for c in range(0, n_chunks, step):
        for vals in local(list(range(c, c + step))):
            carry_fwd(vals)

    if emit_state:
        hf_ref[...] = st_s[...].T
    if has_h0:
        st_s[...] = h0b_ref[...].T
    else:
        st_s[...] = jnp.zeros((D_STATE, GROUP_W), F32)

    def local_bwd(cs):
        rows = [slice(c * CHUNK, (c + 1) * CHUNK) for c in cs]
        cum = [cum_s[r, :] for r in rows]
        dtv = [dtv_s[r, :] for r in rows]
        eb_x = [_split_dot(jnp.exp(v), exp_b, EXPAND_TERMS) for v in cum]
        to_start = [jnp.where(is_fwd, 0.0, v[0:1, :] - v) for v in cum]
        fb_x = [_split_dot(jnp.exp(t) * d, exp_b, EXPAND_TERMS) for t, d in zip(to_start, dtv)]
        xw = [(x_ref[r, :].astype(F32) * f).astype(BF16) for r, f in zip(rows, fb_x)]
        gate = [_silu(z_ref[r, :].astype(F32)) for r in rows]
        return [(rows[k], eb_x[k], xw[k], gate[k]) for k in range(len(cs))]

    def carry_bwd(vals):
        r, eb_x, xw, gate = vals
        st = st_s[...]
        y = y_s[r, :] + _dot(c_ref[r, :], st.astype(BF16)) * eb_x
        st_s[...] = st * eb_x[0:1, :] + _dot(bt_s[r, :], xw)
        y_ref[r, :] = (y * gate).astype(BF16)

    for c in reversed(range(0, n_chunks, step)):
        for vals in reversed(local_bwd(list(range(c, c + step)))):
            carry_bwd(vals)

    if emit_state:
        hb_ref[...] = st_s[...].T


def _ssd_mixer(pa, pb, dt32, dtb_g, alog_g, dsk, *, context, layer=0, h0f=None, h0b=None):
    seq_len = SEQ if context else DEC_SEQ
    n_seq = BATCH if context else DEC_BATCH
    has_h0 = h0f is not None
    emit_state = context
    xw_, bw_ = GROUP_W, D_STATE

    in_specs = [
        pl.BlockSpec((seq_len, xw_), lambda b, g: (b, B_X // xw_ + g)),
        pl.BlockSpec((seq_len, bw_), lambda b, g: (b, B_B // bw_ + g)),
        pl.BlockSpec((seq_len, bw_), lambda b, g: (b, B_C // bw_ + g)),
        pl.BlockSpec((seq_len, xw_), lambda b, g: (b, A_Z // xw_ + g)),
        pl.BlockSpec((seq_len, LANE), lambda b, g: (b, g)),
        pl.BlockSpec((None, 2 * HEADS_PER_GROUP, 1), lambda b, g: (g, 0, 0)),
        pl.BlockSpec((None, 2 * HEADS_PER_GROUP, 1), lambda b, g: (g, 0, 0)),
        pl.BlockSpec((1, xw_), lambda b, g: (0, g)),
    ]
    args = [pb, pb, pb, pa, dt32, dtb_g, alog_g, dsk]
    if has_h0:
        st_spec = pl.BlockSpec((None, None, xw_, D_STATE), lambda b, g: (b, layer, g, 0))
        in_specs += [st_spec, st_spec]
        args += [h0f, h0b]
    out_shape = [jax.ShapeDtypeStruct((n_seq * seq_len, SSD_INNER), BF16)]
    out_specs = [pl.BlockSpec((seq_len, xw_), lambda b, g: (b, g))]
    if emit_state:
        so = pl.BlockSpec((None, xw_, D_STATE), lambda b, g: (b, g, 0))
        out_shape += [jax.ShapeDtypeStruct((n_seq, SSD_INNER, D_STATE), F32)] * 2
        out_specs += [so, so]
    scratch = [
        pltpu.VMEM((seq_len, bw_), BF16),
        pltpu.VMEM((seq_len, LANE), F32),
        pltpu.VMEM((seq_len, LANE), F32),
        pltpu.VMEM((seq_len, xw_), F32),
        pltpu.VMEM((D_STATE, xw_), F32),
    ]
    return pl.pallas_call(
        functools.partial(_ssd_kernel, seq_len=seq_len, has_h0=has_h0, emit_state=emit_state),
        out_shape=tuple(out_shape),
        grid=(n_seq, SSD_GROUPS),
        in_specs=in_specs,
        out_specs=tuple(out_specs),
        scratch_shapes=scratch,
        compiler_params=pltpu.CompilerParams(
            dimension_semantics=("parallel", "parallel"), vmem_limit_bytes=VMEM_LIMIT),
        name="ssd_ctx" if context else "ssd_lat",
    )(*args)


def _merge_kernel(x_ref, na_ref, s_ref, ga_ref, gb_ref, mod_ref, sg_ref, wna_ref, wssd_ref, wo_ref, o_ref):
    y = s_ref[...].astype(F32)
    ms = jnp.mean(y * y, axis=-1, keepdims=True)
    yn = (y * lax.rsqrt(ms + EPS) * sg_ref[...]).astype(BF16)
    a = _dot(na_ref[...], wna_ref[...])
    s = _dot(yn, wssd_ref[...])
    mix = _sigmoid(ga_ref[...].astype(F32)) * a + _sigmoid(gb_ref[...].astype(F32)) * s
    o = _dot(mix.astype(BF16), wo_ref[...])
    o_ref[...] = x_ref[...] + mod_ref[:, 2 * D_MODEL:3 * D_MODEL] * o


def _merge(x, na, ssd, pb, mod, ssd_g, w_na, w_ssd, w_o, *, context, layer):
    tm = TM_MERGE
    n_rows = x.shape[0]
    return pl.pallas_call(
        _merge_kernel,
        out_shape=jax.ShapeDtypeStruct((n_rows, D_MODEL), F32),
        grid=(n_rows // tm,),
        in_specs=[
            pl.BlockSpec((tm, D_MODEL), lambda i: (i, 0)),
            pl.BlockSpec((tm, NA_WIDTH), lambda i: (i, 0)),
            pl.BlockSpec((tm, SSD_INNER), lambda i: (i, 0)),
            pl.BlockSpec((tm, D_MODEL), lambda i: (i, B_GA // D_MODEL)),
            pl.BlockSpec((tm, D_MODEL), lambda i: (i, B_GB // D_MODEL)),
            pl.BlockSpec((None, 1, 6 * D_MODEL), lambda i: (_mod_row(i, tm, context), 0, 0)),
            pl.BlockSpec((1, SSD_INNER), lambda i: (0, 0)),
            _resident((NA_WIDTH, D_MODEL), layer),
            _resident((SSD_INNER, D_MODEL), layer),
            _resident((D_MODEL, D_MODEL), layer),
        ],
        out_specs=pl.BlockSpec((tm, D_MODEL), lambda i: (i, 0)),
        compiler_params=pltpu.CompilerParams(
            dimension_semantics=("parallel",), vmem_limit_bytes=VMEM_LIMIT),
        name="merge_ctx" if context else "merge_lat",
    )(x, na, ssd, pb, pb, mod, ssd_g, w_na, w_ssd, w_o)


def _ffn_kernel(x_ref, xp_ref, xn_ref, mod_ref, g_ref, wup_ref, cw_ref, cb_ref, wd_ref,
                o_ref, h_ref, act_ref, *, tm, seq_len):
    i = pl.program_id(0)
    g = g_ref[...]
    shift = mod_ref[:, 3 * D_MODEL:4 * D_MODEL]
    scale = mod_ref[:, 4 * D_MODEL:5 * D_MODEL]
    h_ref[0:tm, :] = _norm_mod(x_ref[...], g, shift, scale)
    h_ref[tm:tm + 2 * SUBLANE, :] = _norm_mod(
        jnp.concatenate([xp_ref[...], xn_ref[...]], axis=0), g, shift, scale)

    ext = tm + 2 * SUBLANE
    start = i * tm
    row = lax.broadcasted_iota(jnp.int32, (tm, 1), 0)
    pos = (start + row) & (seq_len - 1)
    inner_edges = seq_len < tm
    keep_before = (start & (seq_len - 1)) != 0
    keep_after = ((start + tm) & (seq_len - 1)) != 0

    def conv(c0):
        cols = slice(c0, c0 + FF_TILE)
        pre = _dot(h_ref[...], wup_ref[:, cols])
        before = jnp.where(keep_before, pre[tm:tm + SUBLANE], 0.0)
        after = jnp.where(keep_after, pre[tm + SUBLANE:ext], 0.0)
        v = jnp.concatenate([before, pre[0:tm], after], axis=0)
        left = pltpu.roll(v, 1, 0)[SUBLANE:SUBLANE + tm]
        right = pltpu.roll(v, ext - 1, 0)[SUBLANE:SUBLANE + tm]
        if inner_edges:
            left = jnp.where(pos != 0, left, 0.0)
            right = jnp.where(pos != seq_len - 1, right, 0.0)
        cw = cw_ref[:, cols]
        return cw[0:1] * left + cw[1:2] * pre[0:tm] + cw[2:3] * right + cb_ref[:, cols]

    for c0 in range(0, D_FF, FF_TILE):
        val = conv(c0)
        gate = conv(D_FF + c0)
        act_ref[:, c0:c0 + FF_TILE] = (_silu(gate) * val).astype(BF16)

    acc = _dot(act_ref[:, 0:FF_TILE], wd_ref[0:FF_TILE, :])
    for c0 in range(FF_TILE, D_FF, FF_TILE):
        acc = acc + _dot(act_ref[:, c0:c0 + FF_TILE], wd_ref[c0:c0 + FF_TILE, :])
    o_ref[...] = x_ref[...] + mod_ref[:, 5 * D_MODEL:6 * D_MODEL] * acc


def _conv_ffn(x, mod, g2, w_up, conv_w, conv_b, w_down, *, context, layer):
    tm = TM_FFN
    n_rows = x.shape[0]
    hb = tm // SUBLANE
    last_h = n_rows // SUBLANE - 1
    return pl.pallas_call(
        functools.partial(_ffn_kernel, tm=tm, seq_len=SEQ if context else DEC_SEQ),
        out_shape=jax.ShapeDtypeStruct((n_rows, D_MODEL), F32),
        grid=(n_rows // tm,),
        in_specs=[
            pl.BlockSpec((tm, D_MODEL), lambda i: (i, 0)),
            pl.BlockSpec((SUBLANE, D_MODEL), lambda i: (jnp.maximum(i * hb - 1, 0), 0)),
            pl.BlockSpec((SUBLANE, D_MODEL), lambda i: (jnp.minimum((i + 1) * hb, last_h), 0)),
            pl.BlockSpec((None, 1, 6 * D_MODEL), lambda i: (_mod_row(i, tm, context), 0, 0)),
            pl.BlockSpec((1, D_MODEL), lambda i: (0, 0)),
            _resident((D_MODEL, 2 * D_FF), layer),
            _resident((FFN_CONV, 2 * D_FF)),
            _resident((1, 2 * D_FF)),
            _resident((D_FF, D_MODEL), layer),
        ],
        out_specs=pl.BlockSpec((tm, D_MODEL), lambda i: (i, 0)),
        scratch_shapes=[
            pltpu.VMEM((tm + 2 * SUBLANE, D_MODEL), BF16),
            pltpu.VMEM((tm, D_FF), BF16),
        ],
        compiler_params=pltpu.CompilerParams(
            dimension_semantics=("parallel",), vmem_limit_bytes=VMEM_LIMIT),
        name="conv_ffn_ctx" if context else "conv_ffn_lat",
    )(x, x, x, mod, g2, w_up, conv_w, conv_b, w_down)


def _cast_kernel(x_ref, o_ref):
    o_ref[...] = x_ref[...].astype(BF16)


def _to_bf16(w):
    n_l, r, c = w.shape
    tr = CAST_ROWS
    return pl.pallas_call(
        _cast_kernel,
        out_shape=jax.ShapeDtypeStruct(w.shape, BF16),
        grid=(n_l, r // tr),
        in_specs=[pl.BlockSpec((None, tr, c), lambda l, i: (l, i, 0))],
        out_specs=pl.BlockSpec((None, tr, c), lambda l, i: (l, i, 0)),
        compiler_params=pltpu.CompilerParams(
            dimension_semantics=("parallel", "parallel"), vmem_limit_bytes=VMEM_LIMIT),
        name="cast_bf16",
    )(w)


def _w_in_group_b(w):
    x0 = A_W
    dt0 = x0 + CONV_DIM
    ga0 = dt0 + 2 * SSD_HEADS
    dt = w[:, dt0:ga0]
    hg = HEADS_PER_GROUP
    blocks = []
    for g in range(SSD_GROUPS):
        blocks += [dt[:, g * hg:(g + 1) * hg], dt[:, SSD_HEADS + g * hg:SSD_HEADS + (g + 1) * hg],
                   jnp.zeros((D_MODEL, LANE - 2 * hg), w.dtype)]
    return jnp.concatenate([w[:, x0:dt0], w[:, ga0:]] + blocks, axis=1)


def _by_group(v):
    v = v.astype(F32).reshape(2, SSD_GROUPS, HEADS_PER_GROUP)
    return jnp.concatenate([v[0], v[1]], axis=1)[:, :, None]


def kernel(x_prompt, x_sample, c, cache_k, cache_v, state_ssd_fwd, state_ssd_bwd, c_ctx, w_ada, b_ada, norm1_g, w_in, q_norm_g, k_norm_g, rpb, ssd_conv_w, ssd_conv_b, a_log, dt_bias, d_skip, ssd_norm_g, w_na_out, w_ssd_out, w_o, norm2_g, w_up, ffn_conv_w, ffn_conv_b, w_down):
    xs = {True: x_prompt.reshape(N_CTX, D_MODEL), False: x_sample.reshape(N_LAT, D_MODEL)}
    cond = jnp.concatenate([c_ctx[None, :], c, jnp.zeros((16 - 1 - DEC_BATCH, D_MODEL), F32)], axis=0)
    mod = _modulation(cond, w_ada, b_ada.reshape(DEPTH, 1, 6 * D_MODEL))
    mod = mod.reshape(DEPTH, 16, 1, 6 * D_MODEL)

    ck = cache_k.reshape(DEC_BATCH, DEPTH, PAST_LEN, NA_WIDTH)
    cv = cache_v.reshape(DEC_BATCH, DEPTH, PAST_LEN, NA_WIDTH)
    h0f = state_ssd_fwd.reshape(DEC_BATCH, DEPTH, SSD_INNER, D_STATE)
    h0b = state_ssd_bwd.reshape(DEC_BATCH, DEPTH, SSD_INNER, D_STATE)
    bd = (jnp.arange(MXU_N)[:, None] // HEAD_DIM == jnp.arange(MXU_N)[None, :] // HEAD_DIM).astype(BF16)

    w_in_b, w_na_b, w_ssd_b, w_o_b, w_up_b, w_down_b = (
        _to_bf16(w) for w in (w_in, w_na_out, w_ssd_out, w_o, w_up, w_down))

    new_k, new_v, new_hf, new_hb = [], [], [], []
    for l in range(DEPTH):
        qkg = jnp.concatenate([jnp.tile(q_norm_g[l] * (HEAD_DIM ** -0.5 * LOG2E), NA_HEADS),
                               jnp.tile(k_norm_g[l], NA_HEADS)])[None, :]
        g1 = norm1_g[l][None, :]
        w_b = _w_in_group_b(w_in_b[l])
        conv_b = ssd_conv_b[l][None, :]
        pa_ctx, k32, v32 = _in_projection_a(xs[True], mod[l], g1, w_in_b, qkg, bd, context=True, layer=l)
        (pa_lat,) = _in_projection_a(xs[False], mod[l], g1, w_in_b, qkg, bd, context=False, layer=l)
        pb_ctx, dt_ctx = _in_projection_b(xs[True], mod[l], g1, w_b, ssd_conv_w[l], conv_b, context=True)
        pb_lat, dt_lat = _in_projection_b(xs[False], mod[l], g1, w_b, ssd_conv_w[l], conv_b, context=False)
        new_k.append(k32.reshape(BATCH, SEQ, NA_HEADS, HEAD_DIM))
        new_v.append(v32.reshape(BATCH, SEQ, NA_HEADS, HEAD_DIM))

        na_ctx = _context_attention(pa_ctx)
        na_lat = _neighbourhood_attention(pa_lat, ck, cv, _na_bias(_rpb_reversed(rpb[l])), l)

        dtb_g = _by_group(dt_bias[l])
        alog_g = _by_group(a_log[l])
        dsk = jnp.repeat(d_skip[l], SSD_HEADDIM)[None, :]
        ssd_ctx, hf, hb = _ssd_mixer(pa_ctx, pb_ctx, dt_ctx, dtb_g, alog_g, dsk, context=True)
        (ssd_lat,) = _ssd_mixer(pa_lat, pb_lat, dt_lat, dtb_g, alog_g, dsk, context=False,
                                layer=l, h0f=h0f, h0b=h0b)
        new_hf.append(hf.reshape(BATCH, SSD_HEADS, SSD_HEADDIM, D_STATE))
        new_hb.append(hb.reshape(BATCH, SSD_HEADS, SSD_HEADDIM, D_STATE))

        branch = {True: (na_ctx, ssd_ctx, pb_ctx), False: (na_lat, ssd_lat, pb_lat)}
        w_merge = (w_na_b, w_ssd_b, w_o_b)
        w_ffn = (w_up_b, ffn_conv_w[l], ffn_conv_b[l][None, :], w_down_b)
        for ctx in (True, False):
            na, ssd, pb = branch[ctx]
            x1 = _merge(xs[ctx], na, ssd, pb, mod[l], ssd_norm_g[l][None, :], *w_merge, context=ctx, layer=l)
            xs[ctx] = _conv_ffn(x1, mod[l], norm2_g[l][None, :], *w_ffn, context=ctx, layer=l)

    y_prompt = xs[True].reshape(BATCH, SEQ, D_MODEL)
    y_sample = xs[False].reshape(DEC_BATCH, DEC_SEQ, D_MODEL)
    return (y_prompt, y_sample, jnp.stack(new_k, axis=1), jnp.stack(new_v, axis=1),
            jnp.stack(new_hf, axis=1), jnp.stack(new_hb, axis=1))
```

```python
import functools

import jax
import jax.numpy as jnp
from jax import lax
from jax.experimental import pallas as pl
from jax.experimental.pallas import tpu as pltpu

D_MODEL = 1024
BATCH = 16
SEQ = 256
DEPTH = 2
DEC_BATCH = 8
DEC_SEQ = 2048
PAST_LEN = 256
GRID_W = 64
NA_HEADS = 16
HEAD_DIM = 64
NA_WIDTH = NA_HEADS * HEAD_DIM
WIN_R = 8
WIN_C = 16
SSD_INNER = 2 * D_MODEL
SSD_HEADDIM = 64
SSD_HEADS = SSD_INNER // SSD_HEADDIM
SSD_GROUPS = 4
D_STATE = 128
SSD_CONV = 4
CHUNK = 128
CONV_DIM = SSD_INNER + 2 * SSD_GROUPS * D_STATE
D_FF = 2816
FFN_CONV = 3
EPS = 1e-6

N_CTX = BATCH * SEQ
N_LAT = DEC_BATCH * DEC_SEQ
N_TOK = N_CTX + N_LAT
GRID_ROWS = DEC_SEQ // GRID_W
HEADS_PER_GROUP = SSD_HEADS // SSD_GROUPS
GROUP_W = HEADS_PER_GROUP * SSD_HEADDIM

LANE = 128
SUBLANE = 8
BF16_ROWS = 16
MXU_N = 256

A_Q, A_K, A_V, A_Z, A_W = 0, 1024, 2048, 3072, 5120
B_X, B_B, B_C, B_GA, B_GB, B_W = 0, 2048, 2560, 3072, 4096, 5120
DT_W = SSD_GROUPS * LANE
IN_TILE_N = 2 * MXU_N
TM_IN = 512
TM_MERGE = 512
TM_FFN = 512
FF_TILE = 256
MASK_NEG = -1e30
EXPAND_TERMS = 1
LOG2E = 1.4426950408889634
QK_SQ_TERMS = 1

NA_QROWS = 4
NA_KROWS = 12
NA_Q = NA_QROWS * GRID_W
NA_KW = NA_KROWS * GRID_W
NA_KBLK = 256
NA_STEPS = GRID_ROWS // NA_QROWS
NA_VARIANTS = 3
NA_LOCKSTEP = 4

F32 = jnp.float32
BF16 = jnp.bfloat16
NT = (((1,), (1,)), ((), ()))

VMEM_LIMIT = 56 * 1024 * 1024


def _sigmoid(x):
    return 0.5 + 0.5 * jnp.tanh(0.5 * x)


def _silu(x):
    h = 0.5 * x
    return h + h * jnp.tanh(h)


def _dot(a, b):
    return jnp.dot(a, b, preferred_element_type=F32)


def _dot_nt(a, b):
    return lax.dot_general(a, b, NT, preferred_element_type=F32)


def _split_dot(a, b, terms):
    acc = None
    r = a
    for t in range(terms):
        p = r.astype(BF16)
        d = _dot(p, b)
        acc = d if acc is None else acc + d
        if t + 1 < terms:
            r = r - p.astype(F32)
    return acc


def _mod_row(i, tm, context):
    return 0 if context else 1 + (i * tm) // DEC_SEQ


def _resident(shape):
    return pl.BlockSpec(shape, lambda *_: (0,) * len(shape), pipeline_mode=pl.Buffered(1))


def _norm_mod(xv, g, shift, scale):
    ms = jnp.mean(xv * xv, axis=-1, keepdims=True)
    return (xv * lax.rsqrt(ms + EPS) * g * (1.0 + scale) + shift).astype(BF16)


def _mod_kernel(c_ref, w_ref, b_ref, o_ref):
    s = _silu(c_ref[...]).astype(BF16)
    o_ref[...] = _dot(s, w_ref[...].astype(BF16)) + b_ref[...]


def _modulation(cond, w_ada, b_ada):
    tn = 1536
    return pl.pallas_call(
        _mod_kernel,
        out_shape=jax.ShapeDtypeStruct((DEPTH, 16, 6 * D_MODEL), F32),
        grid=(DEPTH, 6 * D_MODEL // tn),
        in_specs=[
            pl.BlockSpec((16, D_MODEL), lambda l, j: (0, 0)),
            pl.BlockSpec((None, D_MODEL, tn), lambda l, j: (l, 0, j)),
            pl.BlockSpec((None, 1, tn), lambda l, j: (l, 0, j)),
        ],
        out_specs=pl.BlockSpec((None, 16, tn), lambda l, j: (l, 0, j)),
        compiler_params=pltpu.CompilerParams(
            dimension_semantics=("parallel", "parallel"), vmem_limit_bytes=VMEM_LIMIT),
        name="adaln_mod",
    )(cond, w_ada, b_ada)


def _inproj_a_kernel(x_ref, mod_ref, g_ref, w_ref, qkg_ref, bd_ref, *rest, emit_kv):
    if emit_kv:
        _, _, p_ref, k_ref, v_ref, h_ref = rest
    else:
        p_ref, h_ref = rest
    h_ref[...] = _norm_mod(x_ref[...], g_ref[...], mod_ref[:, 0:D_MODEL], mod_ref[:, D_MODEL:2 * D_MODEL])
    for c0 in range(0, A_W, IN_TILE_N):
        cols = slice(c0, c0 + IN_TILE_N)
        acc = _dot(h_ref[...], w_ref[:, cols])
        if c0 < A_V:
            sq = acc * acc
            ss = jnp.concatenate(
                [_split_dot(sq[:, t:t + MXU_N], bd_ref[...], QK_SQ_TERMS) for t in range(0, IN_TILE_N, MXU_N)],
                axis=1)
            acc = acc * lax.rsqrt(ss * (1.0 / HEAD_DIM) + EPS) * qkg_ref[:, cols]
        p_ref[:, cols] = acc.astype(BF16)
        if emit_kv and A_K <= c0 < A_Z:
            dst, d0 = (k_ref, c0 - A_K) if c0 < A_V else (v_ref, c0 - A_V)
            for sq_i in range(acc.shape[0] // SEQ):
                dst[sq_i, :, d0:d0 + IN_TILE_N] = acc[sq_i * SEQ:(sq_i + 1) * SEQ]


def _in_projection_a(x, mod, g1, w, qkg, bd, *, context, layer=0, caches=()):
    tm = TM_IN
    n_rows = x.shape[0]
    out_shape = [jax.ShapeDtypeStruct((n_rows, A_W), BF16)]
    out_specs = [pl.BlockSpec((tm, A_W), lambda i: (i, 0))]
    extra_in, aliases = [], {}
    if context:
        out_shape += [jax.ShapeDtypeStruct(c.shape, F32) for c in caches]
        out_specs += [pl.BlockSpec((tm // SEQ, None, SEQ, NA_WIDTH), lambda i: (i, layer, 0, 0))] * 2
        extra_in = [pl.BlockSpec(memory_space=pl.ANY)] * 2
        aliases = {6: 1, 7: 2}
    return pl.pallas_call(
        functools.partial(_inproj_a_kernel, emit_kv=context),
        out_shape=tuple(out_shape),
        grid=(n_rows // tm,),
        input_output_aliases=aliases,
        in_specs=[
            pl.BlockSpec((tm, D_MODEL), lambda i: (i, 0)),
            pl.BlockSpec((None, 1, 6 * D_MODEL), lambda i: (_mod_row(i, tm, context), 0, 0)),
            pl.BlockSpec((1, D_MODEL), lambda i: (0, 0)),
            _resident((D_MODEL, A_W)),
            pl.BlockSpec((1, A_V), lambda i: (0, 0)),
            pl.BlockSpec((MXU_N, MXU_N), lambda i: (0, 0)),
        ] + extra_in,
        out_specs=tuple(out_specs),
        scratch_shapes=[pltpu.VMEM((tm, D_MODEL), BF16)],
        compiler_params=pltpu.CompilerParams(
            dimension_semantics=("parallel",), vmem_limit_bytes=VMEM_LIMIT),
        name="in_proj_a_ctx" if context else "in_proj_a_lat",
    )(x, mod, g1, w, qkg, bd, *caches)


def _inproj_b_kernel(x_ref, xp_ref, xn_ref, mod_ref, g_ref, w_ref, cw_ref, cb_ref,
                     p_ref, dt_ref, h_ref, *, tm, seq_len):
    i = pl.program_id(0)
    g = g_ref[...]
    shift = mod_ref[:, 0:D_MODEL]
    scale = mod_ref[:, D_MODEL:2 * D_MODEL]
    h_ref[0:tm, :] = _norm_mod(x_ref[...], g, shift, scale)
    h_ref[tm:tm + 2 * SUBLANE, :] = _norm_mod(
        jnp.concatenate([xp_ref[...], xn_ref[...]], axis=0), g, shift, scale)

    ext = tm + 2 * SUBLANE
    start = i * tm
    row = lax.broadcasted_iota(jnp.int32, (tm, 1), 0)
    pos = (start + row) & (seq_len - 1)
    inner_edges = seq_len < tm
    keep_before = (start & (seq_len - 1)) != 0
    keep_after = ((start + tm) & (seq_len - 1)) != 0
    for c0 in range(0, B_GA, IN_TILE_N):
        cols = slice(c0, c0 + IN_TILE_N)
        pre = _dot(h_ref[...], w_ref[:, cols])
        before = jnp.where(keep_before, pre[tm:tm + SUBLANE], 0.0)
        after = jnp.where(keep_after, pre[tm + SUBLANE:ext], 0.0)
        v = jnp.concatenate([before, pre[0:tm], after], axis=0)
        back2 = pltpu.roll(v, 2, 0)[SUBLANE:SUBLANE + tm]
        back1 = pltpu.roll(v, 1, 0)[SUBLANE:SUBLANE + tm]
        ahead = pltpu.roll(v, ext - 1, 0)[SUBLANE:SUBLANE + tm]
        if inner_edges:
            back2 = jnp.where(pos >= 2, back2, 0.0)
            back1 = jnp.where(pos >= 1, back1, 0.0)
            ahead = jnp.where(pos != seq_len - 1, ahead, 0.0)
        cw = cw_ref[:, cols]
        u = cw[0:1] * back2 + cw[1:2] * back1 + cw[2:3] * pre[0:tm] + cw[3:4] * ahead + cb_ref[:, cols]
        p_ref[:, cols] = _silu(u).astype(BF16)
    for c0 in range(B_GA, B_W, IN_TILE_N):
        cols = slice(c0, c0 + IN_TILE_N)
        p_ref[:, cols] = _dot(h_ref[0:tm, :], w_ref[:, cols]).astype(BF16)
    dt_ref[...] = _dot(h_ref[0:tm, :], w_ref[:, B_W:B_W + DT_W])


def _in_projection_b(x, mod, g1, w, conv_w, conv_b, *, context):
    tm = TM_IN
    n_rows = x.shape[0]
    hb = tm // SUBLANE
    last_h = n_rows // SUBLANE - 1
    return pl.pallas_call(
        functools.partial(_inproj_b_kernel, tm=tm, seq_len=SEQ if context else DEC_SEQ),
        out_shape=(jax.ShapeDtypeStruct((n_rows, B_W), BF16), jax.ShapeDtypeStruct((n_rows, DT_W), F32)),
        grid=(n_rows // tm,),
        in_specs=[
            pl.BlockSpec((tm, D_MODEL), lambda i: (i, 0)),
            pl.BlockSpec((SUBLANE, D_MODEL), lambda i: (jnp.maximum(i * hb - 1, 0), 0)),
            pl.BlockSpec((SUBLANE, D_MODEL), lambda i: (jnp.minimum((i + 1) * hb, last_h), 0)),
            pl.BlockSpec((None, 1, 6 * D_MODEL), lambda i: (_mod_row(i, tm, context), 0, 0)),
            pl.BlockSpec((1, D_MODEL), lambda i: (0, 0)),
            _resident((D_MODEL, B_W + DT_W)),
            pl.BlockSpec((SSD_CONV, CONV_DIM), lambda i: (0, 0)),
            pl.BlockSpec((1, CONV_DIM), lambda i: (0, 0)),
        ],
        out_specs=(pl.BlockSpec((tm, B_W), lambda i: (i, 0)), pl.BlockSpec((tm, DT_W), lambda i: (i, 0))),
        scratch_shapes=[pltpu.VMEM((tm + 2 * SUBLANE, D_MODEL), BF16)],
        compiler_params=pltpu.CompilerParams(
            dimension_semantics=("parallel",), vmem_limit_bytes=VMEM_LIMIT),
        name="in_proj_b_ctx" if context else "in_proj_b_lat",
    )(x, x, x, mod, g1, w, conv_w, conv_b)


def _head_masks():
    lane = lax.broadcasted_iota(jnp.int32, (1, LANE), 1)
    return lane < HEAD_DIM


def _ctx_attn_kernel(q_ref, k_ref, v_ref, o_ref):
    lo = _head_masks()
    for c0 in range(0, NA_WIDTH, LANE):
        cols = slice(c0, c0 + LANE)
        q = q_ref[:, cols]
        k = k_ref[:, cols]
        v = v_ref[:, cols]
        outs = []
        for hh in range(2):
            m = lo if hh == 0 else jnp.logical_not(lo)
            qm = jnp.where(m, q, jnp.zeros_like(q))
            s = _dot_nt(qm, k)
            mx = jnp.max(s, axis=-1, keepdims=True)
            p = jnp.exp2(s - mx)
            l = jnp.sum(p, axis=-1, keepdims=True)
            outs.append(_dot(p.astype(BF16), v) / l)
        o_ref[:, cols] = jnp.where(lo, outs[0], outs[1]).astype(BF16)


def _context_attention(pa):
    blk = (SEQ, NA_WIDTH)
    return pl.pallas_call(
        _ctx_attn_kernel,
        out_shape=jax.ShapeDtypeStruct((N_CTX, NA_WIDTH), BF16),
        grid=(BATCH,),
        in_specs=[
            pl.BlockSpec(blk, lambda b: (b, A_Q // NA_WIDTH)),
            pl.BlockSpec(blk, lambda b: (b, A_K // NA_WIDTH)),
            pl.BlockSpec(blk, lambda b: (b, A_V // NA_WIDTH)),
        ],
        out_specs=pl.BlockSpec(blk, lambda b: (b, 0)),
        compiler_params=pltpu.CompilerParams(
            dimension_semantics=("parallel",), vmem_limit_bytes=VMEM_LIMIT),
        name="ctx_attn",
    )(pa, pa, pa)


def _na_bias_kernel(r_ref, o_ref):
    ck = lax.broadcasted_iota(jnp.int32, (GRID_W, LANE), 0)
    ln = lax.broadcasted_iota(jnp.int32, (GRID_W, LANE), 1)
    cq = ln & (GRID_W - 1)
    cs = jnp.clip(cq - WIN_C // 2, 0, GRID_W - WIN_C)
    col_ok = jnp.logical_and(ck >= cs, ck < cs + WIN_C)
    left = ln < GRID_W
    n_dr = 2 * WIN_R - 1
    toep = []
    for dr in range(n_dr):
        base = jnp.broadcast_to(r_ref[dr:dr + 1, :] * LOG2E, (GRID_W, LANE))
        toep.append([pltpu.roll(base, s * GRID_W, 1, stride=1, stride_axis=0) for s in range(2)])
    neg = jnp.full((GRID_W, LANE), MASK_NEG, F32)

    def rel(v, j, q4):
        if v == 0:
            return j - q4 + WIN_R - 1, j < WIN_R
        if v == 1:
            return j - q4 + WIN_R // 2 - 1, q4 <= j < q4 + WIN_R
        return j - q4 - 1, j >= NA_KROWS - WIN_R

    for v in range(NA_VARIANTS):
        for j in range(NA_KROWS):
            for lb in range(NA_QROWS // 2):
                halves = []
                for s in range(2):
                    dr, ok = rel(v, j, 2 * lb + s)
                    halves.append(jnp.where(col_ok, toep[dr][s], neg) if ok else neg)
                o_ref[v, j * GRID_W:(j + 1) * GRID_W, lb * LANE:(lb + 1) * LANE] = (
                    jnp.where(left, halves[0], halves[1]))


def _na_bias(rrev):
    return pl.pallas_call(
        _na_bias_kernel,
        out_shape=jax.ShapeDtypeStruct((NA_HEADS // 2, NA_VARIANTS, NA_KW, 2 * NA_Q), F32),
        grid=(NA_HEADS,),
        in_specs=[pl.BlockSpec((None, 2 * WIN_R, LANE), lambda h: (h, 0, 0))],
        out_specs=pl.BlockSpec((None, NA_VARIANTS, NA_KW, NA_Q), lambda h: (h // 2, 0, 0, h % 2)),
        compiler_params=pltpu.CompilerParams(
            dimension_semantics=("parallel",), vmem_limit_bytes=VMEM_LIMIT),
        name="na_bias",
    )(rrev)


def _na_kernel(q_ref, k_ref, v_ref, kc_ref, vc_ref, bias_ref, o_ref, vt_ref):
    lo = _head_masks()
    kc = kc_ref[...].astype(BF16)
    vct = vc_ref[...].T.astype(BF16)
    for t in range(DEC_SEQ // NA_KBLK):
        vt_ref[t] = v_ref[t * NA_KBLK:(t + 1) * NA_KBLK, :].astype(F32).T.astype(BF16)
    n_wblk = NA_KW // NA_KBLK

    def steps(ms):
        n = range(len(ms))
        blk0 = [min(max(m - 1, 0), DEC_SEQ // NA_KBLK - n_wblk) for m in ms]
        var = [0 if m == 0 else (2 if m == NA_STEPS - 1 else 1) for m in ms]
        qg = [q_ref[m * NA_Q:(m + 1) * NA_Q, :] for m in ms]
        kw = [k_ref[b * NA_KBLK:b * NA_KBLK + NA_KW, :] for b in blk0]
        qcat = [jnp.concatenate([jnp.where(lo, q, jnp.zeros_like(q)), jnp.where(lo, jnp.zeros_like(q), q)], axis=0)
                for q in qg]
        sw = [_dot_nt(kw[i], qcat[i]) + bias_ref[var[i]] for i in n]
        sc = [_dot_nt(kc, qcat[i]) for i in n]
        mx = [jnp.maximum(jnp.max(sw[i], axis=0, keepdims=True), jnp.max(sc[i], axis=0, keepdims=True)) for i in n]
        pw = [jnp.exp2(sw[i] - mx[i]) for i in n]
        pc = [jnp.exp2(sc[i] - mx[i]) for i in n]
        l = [jnp.sum(pw[i], axis=0, keepdims=True) + jnp.sum(pc[i], axis=0, keepdims=True) for i in n]
        pwb = [p.astype(BF16) for p in pw]
        o = [_dot(vct, p.astype(BF16)) for p in pc]
        for t in range(n_wblk):
            o = [o[i] + _dot(vt_ref[blk0[i] + t], pwb[i][t * NA_KBLK:(t + 1) * NA_KBLK, :]) for i in n]
        for i in n:
            on = o[i] / l[i]
            both = jnp.concatenate([on[0:HEAD_DIM, 0:NA_Q], on[HEAD_DIM:2 * HEAD_DIM, NA_Q:2 * NA_Q]], axis=0)
            o_ref[ms[i] * NA_Q:(ms[i] + 1) * NA_Q, :] = both.T.astype(BF16)

    for m in range(0, NA_STEPS, NA_LOCKSTEP):
        steps(list(range(m, m + NA_LOCKSTEP)))


def _neighbourhood_attention(pa, cache_k, cache_v, bias, layer):
    hp = NA_HEADS // 2
    blk = (DEC_SEQ, LANE)
    cblk = (None, None, PAST_LEN, LANE)
    return pl.pallas_call(
        _na_kernel,
        out_shape=jax.ShapeDtypeStruct((N_LAT, NA_WIDTH), BF16),
        grid=(hp, DEC_BATCH),
        in_specs=[
            pl.BlockSpec(blk, lambda h, b: (b, A_Q // LANE + h)),
            pl.BlockSpec(blk, lambda h, b: (b, A_K // LANE + h)),
            pl.BlockSpec(blk, lambda h, b: (b, A_V // LANE + h)),
            pl.BlockSpec(cblk, lambda h, b: (b, layer, 0, h)),
            pl.BlockSpec(cblk, lambda h, b: (b, layer, 0, h)),
            pl.BlockSpec((None, NA_VARIANTS, NA_KW, 2 * NA_Q), lambda h, b: (h, 0, 0, 0)),
        ],
        out_specs=pl.BlockSpec(blk, lambda h, b: (b, h)),
        scratch_shapes=[pltpu.VMEM((DEC_SEQ // NA_KBLK, LANE, NA_KBLK), BF16)],
        compiler_params=pltpu.CompilerParams(
            dimension_semantics=("parallel", "parallel"), vmem_limit_bytes=VMEM_LIMIT),
        name="na_attn",
    )(pa, pa, pa, cache_k, cache_v, bias)


def _rpb_reversed(rpb_l):
    n = WIN_C - 1
    pos = rpb_l[:, :, n::-1]
    neg = rpb_l[:, :, :n:-1]
    z = jnp.zeros(rpb_l.shape[:2] + (LANE - 2 * n - 1,), F32)
    r = jnp.concatenate([pos.astype(F32), z, neg.astype(F32)], axis=-1)
    return jnp.pad(r, ((0, 0), (0, 1), (0, 0)))


def _ssd_kernel(*refs, seq_len, has_h0, emit_state):
    it = iter(refs)
    x_ref, b_ref, c_ref, z_ref, dt_ref, dtb_ref, alog_ref, dsk_ref = (next(it) for _ in range(8))
    if has_h0:
        h0f_ref, h0b_ref = next(it), next(it)
    if emit_state:
        next(it), next(it)
    y_ref = next(it)
    if emit_state:
        hf_ref, hb_ref = next(it), next(it)
    bt_s, cum_s, dtv_s, y_s, st_s = (next(it) for _ in range(5))

    n_chunks = seq_len // CHUNK
    n_dir = 2 * HEADS_PER_GROUP

    lane = lax.broadcasted_iota(jnp.int32, (1, LANE), 1)
    ii = lax.broadcasted_iota(jnp.int32, (CHUNK, CHUNK), 0)
    jj = lax.broadcasted_iota(jnp.int32, (CHUNK, CHUNK), 1)
    lower = jj < ii
    diag = jj == ii
    tril = jnp.where(jj <= ii, 1.0, 0.0).astype(BF16)
    triu = jnp.where(jj >= ii, 1.0, 0.0).astype(BF16)
    er = lax.broadcasted_iota(jnp.int32, (LANE, GROUP_W), 0)
    ec = lax.broadcasted_iota(jnp.int32, (LANE, GROUP_W), 1) // SSD_HEADDIM
    exp_f = jnp.where(er == ec, 1.0, 0.0).astype(BF16)
    exp_b = jnp.where(er == ec + HEADS_PER_GROUP, 1.0, 0.0).astype(BF16)
    lo64 = lane < SSD_HEADDIM
    is_fwd = lane < HEADS_PER_GROUP
    a_neg = -jnp.exp(alog_ref[...])
    dtb = dtb_ref[...]
    dsk = dsk_ref[...]
    row_fwd = lax.broadcasted_iota(jnp.int32, (n_dir, 1), 0) < HEADS_PER_GROUP
    pad_rows = jnp.zeros((LANE - n_dir, CHUNK), F32)

    if has_h0:
        st_s[...] = h0f_ref[...].T
    else:
        st_s[...] = jnp.zeros((D_STATE, GROUP_W), F32)

    def local(cs):
        rows = [slice(c * CHUNK, (c + 1) * CHUNK) for c in cs]
        xcb = [x_ref[r, :] for r in rows]
        bcb = [b_ref[r, :] for r in rows]
        ccb = [c_ref[r, :] for r in rows]
        btb = [b.astype(F32).T.astype(BF16) for b in bcb]
        for r, b in zip(rows, btb):
            bt_s[r, :] = b
        raw = [dt_ref[r, :].T[0:n_dir, :] + dtb for r in rows]
        e = [jnp.exp(-jnp.abs(x)) for x in raw]
        log1p_e = [jnp.where(v < 1e-4, v * (1.0 - 0.5 * v), jnp.log(1.0 + v)) for v in e]
        dt_t = [jnp.maximum(x, 0.0) + l for x, l in zip(raw, log1p_e)]
        a = [d * a_neg for d in dt_t]
        cum_t = [jnp.where(row_fwd, _split_dot(v, triu, 3), _split_dot(v, tril, 3)) for v in a]
        cum = [jnp.concatenate([v, pad_rows], axis=0).T for v in cum_t]
        dtv = [jnp.concatenate([v, pad_rows], axis=0).T for v in dt_t]
        for k, r in enumerate(rows):
            cum_s[r, :] = cum[k]
            dtv_s[r, :] = dtv[k]
        g = [_dot_nt(cc, bb) for cc, bb in zip(ccb, bcb)]
        src_t = [c_t - jnp.log(d) for c_t, d in zip(cum_t, dt_t)]
        parts = [[] for _ in cs]
        for pp in range(HEADS_PER_GROUP // 2):
            for k in range(len(cs)):
                ws = []
                for hh in (2 * pp, 2 * pp + 1):
                    hb = HEADS_PER_GROUP + hh
                    seg = jnp.where(lower, cum[k][:, hh:hh + 1] - src_t[k][hh:hh + 1, :],
                                    cum[k][:, hb:hb + 1] - src_t[k][hb:hb + 1, :])
                    both = jnp.exp(seg) + jnp.where(diag, dt_t[k][hh:hh + 1, :], 0.0)
                    ws.append((g[k] * both).astype(BF16))
                wp = jnp.concatenate(ws, axis=1)
                xp = xcb[k][:, pp * LANE:(pp + 1) * LANE]
                zero = jnp.zeros_like(xp)
                rhs = jnp.concatenate([jnp.where(lo64, xp, zero), jnp.where(lo64, zero, xp)], axis=0)
                parts[k].append(_dot(wp, rhs))
        y = [jnp.concatenate(p, axis=1) for p in parts]
        ef_x = [_split_dot(jnp.exp(v), exp_f, EXPAND_TERMS) for v in cum]
        to_end = [jnp.where(is_fwd, v[CHUNK - 1:CHUNK, :] - v, 0.0) for v in cum]
        ff_x = [_split_dot(jnp.exp(t) * d, exp_f, EXPAND_TERMS) for t, d in zip(to_end, dtv)]
        return [(rows[k], xcb[k], ccb[k], btb[k], y[k], ef_x[k], ff_x[k]) for k in range(len(cs))]

    def carry_fwd(vals):
        r, xcb, ccb, btb, y, ef_x, ff_x = vals
        xc = xcb.astype(F32)
        st = st_s[...]
        y_s[r, :] = y + _dot(ccb, st.astype(BF16)) * ef_x + dsk * xc
        st_s[...] = st * ef_x[CHUNK - 1:CHUNK, :] + _dot(btb, (xc * ff_x).astype(BF16))

    step = 4 if n_chunks % 4 == 0 else (2 if n_chunks % 2 == 0 else 1)
    for c in range(0, n_chunks, step):
        for vals in local(list(range(c, c + step))):
            carry_fwd(vals)

    if emit_state:
        hf_ref[...] = st_s[...].T
    if has_h0:
        st_s[...] = h0b_ref[...].T
    else:
        st_s[...] = jnp.zeros((D_STATE, GROUP_W), F32)

    def local_bwd(cs):
        rows = [slice(c * CHUNK, (c + 1) * CHUNK) for c in cs]
        cum = [cum_s[r, :] for r in rows]
        dtv = [dtv_s[r, :] for r in rows]
        eb_x = [_split_dot(jnp.exp(v), exp_b, EXPAND_TERMS) for v in cum]
        to_start = [jnp.where(is_fwd, 0.0, v[0:1, :] - v) for v in cum]
        fb_x = [_split_dot(jnp.exp(t) * d, exp_b, EXPAND_TERMS) for t, d in zip(to_start, dtv)]
        xw = [(x_ref[r, :].astype(F32) * f).astype(BF16) for r, f in zip(rows, fb_x)]
        gate = [_silu(z_ref[r, :].astype(F32)) for r in rows]
        return [(rows[k], eb_x[k], xw[k], gate[k]) for k in range(len(cs))]

    def carry_bwd(vals):
        r, eb_x, xw, gate = vals
        st = st_s[...]
        y = y_s[r, :] + _dot(c_ref[r, :], st.astype(BF16)) * eb_x
        st_s[...] = st * eb_x[0:1, :] + _dot(bt_s[r, :], xw)
        y_ref[r, :] = (y * gate).astype(BF16)

    for c in reversed(range(0, n_chunks, step)):
        for vals in reversed(local_bwd(list(range(c, c + step)))):
            carry_bwd(vals)

    if emit_state:
        hb_ref[...] = st_s[...].T


def _ssd_mixer(pa, pb, dt32, dtb_g, alog_g, dsk, *, context, layer=0, h0f=None, h0b=None, states=()):
    seq_len = SEQ if context else DEC_SEQ
    n_seq = BATCH if context else DEC_BATCH
    has_h0 = h0f is not None
    emit_state = context
    xw_, bw_ = GROUP_W, D_STATE

    in_specs = [
        pl.BlockSpec((seq_len, xw_), lambda b, g: (b, B_X // xw_ + g)),
        pl.BlockSpec((seq_len, bw_), lambda b, g: (b, B_B // bw_ + g)),
        pl.BlockSpec((seq_len, bw_), lambda b, g: (b, B_C // bw_ + g)),
        pl.BlockSpec((seq_len, xw_), lambda b, g: (b, A_Z // xw_ + g)),
        pl.BlockSpec((seq_len, LANE), lambda b, g: (b, g)),
        pl.BlockSpec((None, 2 * HEADS_PER_GROUP, 1), lambda b, g: (g, 0, 0)),
        pl.BlockSpec((None, 2 * HEADS_PER_GROUP, 1), lambda b, g: (g, 0, 0)),
        pl.BlockSpec((1, xw_), lambda b, g: (0, g)),
    ]
    args = [pb, pb, pb, pa, dt32, dtb_g, alog_g, dsk]
    if has_h0:
        st_spec = pl.BlockSpec((None, None, xw_, D_STATE), lambda b, g: (b, layer, g, 0))
        in_specs += [st_spec, st_spec]
        args += [h0f, h0b]
    out_shape = [jax.ShapeDtypeStruct((n_seq * seq_len, SSD_INNER), BF16)]
    out_specs = [pl.BlockSpec((seq_len, xw_), lambda b, g: (b, g))]
    aliases = {}
    if emit_state:
        aliases = {len(args): 1, len(args) + 1: 2}
        in_specs += [pl.BlockSpec(memory_space=pl.ANY)] * 2
        args += list(states)
        so = pl.BlockSpec((None, None, xw_, D_STATE), lambda b, g: (b, layer, g, 0))
        out_shape += [jax.ShapeDtypeStruct(st.shape, F32) for st in states]
        out_specs += [so, so]
    scratch = [
        pltpu.VMEM((seq_len, bw_), BF16),
        pltpu.VMEM((seq_len, LANE), F32),
        pltpu.VMEM((seq_len, LANE), F32),
        pltpu.VMEM((seq_len, xw_), F32),
        pltpu.VMEM((D_STATE, xw_), F32),
    ]
    return pl.pallas_call(
        functools.partial(_ssd_kernel, seq_len=seq_len, has_h0=has_h0, emit_state=emit_state),
        out_shape=tuple(out_shape),
        grid=(n_seq, SSD_GROUPS),
        in_specs=in_specs,
        out_specs=tuple(out_specs),
        input_output_aliases=aliases,
        scratch_shapes=scratch,
        compiler_params=pltpu.CompilerParams(
            dimension_semantics=("parallel", "parallel"), vmem_limit_bytes=VMEM_LIMIT),
        name="ssd_ctx" if context else "ssd_lat",
    )(*args)


def _merge_kernel(x_ref, na_ref, s_ref, ga_ref, gb_ref, mod_ref, sg_ref, wna_ref, wssd_ref, wo_ref, o_ref):
    y = s_ref[...].astype(F32)
    ms = jnp.mean(y * y, axis=-1, keepdims=True)
    yn = (y * lax.rsqrt(ms + EPS) * sg_ref[...]).astype(BF16)
    a = _dot(na_ref[...], wna_ref[...])
    s = _dot(yn, wssd_ref[...])
    mix = _sigmoid(ga_ref[...].astype(F32)) * a + _sigmoid(gb_ref[...].astype(F32)) * s
    o = _dot(mix.astype(BF16), wo_ref[...])
    o_ref[...] = x_ref[...] + mod_ref[:, 2 * D_MODEL:3 * D_MODEL] * o


def _merge(x, na, ssd, pb, mod, ssd_g, w_na, w_ssd, w_o, *, context):
    tm = TM_MERGE
    n_rows = x.shape[0]
    return pl.pallas_call(
        _merge_kernel,
        out_shape=jax.ShapeDtypeStruct((n_rows, D_MODEL), F32),
        grid=(n_rows // tm,),
        in_specs=[
            pl.BlockSpec((tm, D_MODEL), lambda i: (i, 0)),
            pl.BlockSpec((tm, NA_WIDTH), lambda i: (i, 0)),
            pl.BlockSpec((tm, SSD_INNER), lambda i: (i, 0)),
            pl.BlockSpec((tm, D_MODEL), lambda i: (i, B_GA // D_MODEL)),
            pl.BlockSpec((tm, D_MODEL), lambda i: (i, B_GB // D_MODEL)),
            pl.BlockSpec((None, 1, 6 * D_MODEL), lambda i: (_mod_row(i, tm, context), 0, 0)),
            pl.BlockSpec((1, SSD_INNER), lambda i: (0, 0)),
            _resident((NA_WIDTH, D_MODEL)),
            _resident((SSD_INNER, D_MODEL)),
            _resident((D_MODEL, D_MODEL)),
        ],
        out_specs=pl.BlockSpec((tm, D_MODEL), lambda i: (i, 0)),
        compiler_params=pltpu.CompilerParams(
            dimension_semantics=("parallel",), vmem_limit_bytes=VMEM_LIMIT),
        name="merge_ctx" if context else "merge_lat",
    )(x, na, ssd, pb, pb, mod, ssd_g, w_na, w_ssd, w_o)


def _ffn_kernel(x_ref, xp_ref, xn_ref, mod_ref, g_ref, wup_ref, cw_ref, cb_ref, wd_ref,
                o_ref, h_ref, act_ref, *, tm, seq_len):
    i = pl.program_id(0)
    g = g_ref[...]
    shift = mod_ref[:, 3 * D_MODEL:4 * D_MODEL]
    scale = mod_ref[:, 4 * D_MODEL:5 * D_MODEL]
    h_ref[0:tm, :] = _norm_mod(x_ref[...], g, shift, scale)
    h_ref[tm:tm + 2 * SUBLANE, :] = _norm_mod(
        jnp.concatenate([xp_ref[...], xn_ref[...]], axis=0), g, shift, scale)

    ext = tm + 2 * SUBLANE
    start = i * tm
    row = lax.broadcasted_iota(jnp.int32, (tm, 1), 0)
    pos = (start + row) & (seq_len - 1)
    inner_edges = seq_len < tm
    keep_before = (start & (seq_len - 1)) != 0
    keep_after = ((start + tm) & (seq_len - 1)) != 0

    def conv(c0):
        cols = slice(c0, c0 + FF_TILE)
        pre = _dot(h_ref[...], wup_ref[:, cols])
        before = jnp.where(keep_before, pre[tm:tm + SUBLANE], 0.0)
        after = jnp.where(keep_after, pre[tm + SUBLANE:ext], 0.0)
        v = jnp.concatenate([before, pre[0:tm], after], axis=0)
        left = pltpu.roll(v, 1, 0)[SUBLANE:SUBLANE + tm]
        right = pltpu.roll(v, ext - 1, 0)[SUBLANE:SUBLANE + tm]
        if inner_edges:
            left = jnp.where(pos != 0, left, 0.0)
            right = jnp.where(pos != seq_len - 1, right, 0.0)
        cw = cw_ref[:, cols]
        return cw[0:1] * left + cw[1:2] * pre[0:tm] + cw[2:3] * right + cb_ref[:, cols]

    for c0 in range(0, D_FF, FF_TILE):
        val = conv(c0)
        gate = conv(D_FF + c0)
        act_ref[:, c0:c0 + FF_TILE] = (_silu(gate) * val).astype(BF16)

    acc = _dot(act_ref[:, 0:FF_TILE], wd_ref[0:FF_TILE, :])
    for c0 in range(FF_TILE, D_FF, FF_TILE):
        acc = acc + _dot(act_ref[:, c0:c0 + FF_TILE], wd_ref[c0:c0 + FF_TILE, :])
    o_ref[...] = x_ref[...] + mod_ref[:, 5 * D_MODEL:6 * D_MODEL] * acc


def _conv_ffn(x, mod, g2, w_up, conv_w, conv_b, w_down, *, context):
    tm = TM_FFN
    n_rows = x.shape[0]
    hb = tm // SUBLANE
    last_h = n_rows // SUBLANE - 1
    return pl.pallas_call(
        functools.partial(_ffn_kernel, tm=tm, seq_len=SEQ if context else DEC_SEQ),
        out_shape=jax.ShapeDtypeStruct((n_rows, D_MODEL), F32),
        grid=(n_rows // tm,),
        in_specs=[
            pl.BlockSpec((tm, D_MODEL), lambda i: (i, 0)),
            pl.BlockSpec((SUBLANE, D_MODEL), lambda i: (jnp.maximum(i * hb - 1, 0), 0)),
            pl.BlockSpec((SUBLANE, D_MODEL), lambda i: (jnp.minimum((i + 1) * hb, last_h), 0)),
            pl.BlockSpec((None, 1, 6 * D_MODEL), lambda i: (_mod_row(i, tm, context), 0, 0)),
            pl.BlockSpec((1, D_MODEL), lambda i: (0, 0)),
            _resident((D_MODEL, 2 * D_FF)),
            _resident((FFN_CONV, 2 * D_FF)),
            _resident((1, 2 * D_FF)),
            _resident((D_FF, D_MODEL)),
        ],
        out_specs=pl.BlockSpec((tm, D_MODEL), lambda i: (i, 0)),
        scratch_shapes=[
            pltpu.VMEM((tm + 2 * SUBLANE, D_MODEL), BF16),
            pltpu.VMEM((tm, D_FF), BF16),
        ],
        compiler_params=pltpu.CompilerParams(
            dimension_semantics=("parallel",), vmem_limit_bytes=VMEM_LIMIT),
        name="conv_ffn_ctx" if context else "conv_ffn_lat",
    )(x, x, x, mod, g2, w_up, conv_w, conv_b, w_down)


def _w_in_group_b(w):
    x0 = A_W
    dt0 = x0 + CONV_DIM
    ga0 = dt0 + 2 * SSD_HEADS
    dt = w[:, dt0:ga0]
    hg = HEADS_PER_GROUP
    blocks = []
    for g in range(SSD_GROUPS):
        blocks += [dt[:, g * hg:(g + 1) * hg], dt[:, SSD_HEADS + g * hg:SSD_HEADS + (g + 1) * hg],
                   jnp.zeros((D_MODEL, LANE - 2 * hg), w.dtype)]
    return jnp.concatenate([w[:, x0:dt0], w[:, ga0:]] + blocks, axis=1).astype(BF16)


def _by_group(v):
    v = v.astype(F32).reshape(2, SSD_GROUPS, HEADS_PER_GROUP)
    return jnp.concatenate([v[0], v[1]], axis=1)[:, :, None]


def kernel(x_prompt, x_sample, c, cache_k, cache_v, state_ssd_fwd, state_ssd_bwd, c_ctx, w_ada, b_ada, norm1_g, w_in, q_norm_g, k_norm_g, rpb, ssd_conv_w, ssd_conv_b, a_log, dt_bias, d_skip, ssd_norm_g, w_na_out, w_ssd_out, w_o, norm2_g, w_up, ffn_conv_w, ffn_conv_b, w_down):
    xs = {True: x_prompt.reshape(N_CTX, D_MODEL), False: x_sample.reshape(N_LAT, D_MODEL)}
    cond = jnp.concatenate([c_ctx[None, :], c, jnp.zeros((16 - 1 - DEC_BATCH, D_MODEL), F32)], axis=0)
    mod = _modulation(cond, w_ada, b_ada.reshape(DEPTH, 1, 6 * D_MODEL))
    mod = mod.reshape(DEPTH, 16, 1, 6 * D_MODEL)

    ck = cache_k.reshape(DEC_BATCH, DEPTH, PAST_LEN, NA_WIDTH)
    cv = cache_v.reshape(DEC_BATCH, DEPTH, PAST_LEN, NA_WIDTH)
    h0f = state_ssd_fwd.reshape(DEC_BATCH, DEPTH, SSD_INNER, D_STATE)
    h0b = state_ssd_bwd.reshape(DEC_BATCH, DEPTH, SSD_INNER, D_STATE)
    bd = (jnp.arange(MXU_N)[:, None] // HEAD_DIM == jnp.arange(MXU_N)[None, :] // HEAD_DIM).astype(BF16)

    caches = (jnp.zeros((BATCH, DEPTH, SEQ, NA_WIDTH), F32),) * 2
    states = (jnp.zeros((BATCH, DEPTH, SSD_INNER, D_STATE), F32),) * 2
    for l in range(DEPTH):
        qkg = jnp.concatenate([jnp.tile(q_norm_g[l] * (HEAD_DIM ** -0.5 * LOG2E), NA_HEADS),
                               jnp.tile(k_norm_g[l], NA_HEADS)])[None, :]
        g1 = norm1_g[l][None, :]
        w_a = w_in[l][:, :A_W].astype(BF16)
        w_b = _w_in_group_b(w_in[l])
        conv_b = ssd_conv_b[l][None, :]
        pa_ctx, *caches = _in_projection_a(xs[True], mod[l], g1, w_a, qkg, bd, context=True, layer=l, caches=caches)
        (pa_lat,) = _in_projection_a(xs[False], mod[l], g1, w_a, qkg, bd, context=False)
        pb_ctx, dt_ctx = _in_projection_b(xs[True], mod[l], g1, w_b, ssd_conv_w[l], conv_b, context=True)
        pb_lat, dt_lat = _in_projection_b(xs[False], mod[l], g1, w_b, ssd_conv_w[l], conv_b, context=False)

        na_ctx = _context_attention(pa_ctx)
        na_lat = _neighbourhood_attention(pa_lat, ck, cv, _na_bias(_rpb_reversed(rpb[l])), l)

        dtb_g = _by_group(dt_bias[l])
        alog_g = _by_group(a_log[l])
        dsk = jnp.repeat(d_skip[l], SSD_HEADDIM)[None, :]
        ssd_ctx, *states = _ssd_mixer(pa_ctx, pb_ctx, dt_ctx, dtb_g, alog_g, dsk, context=True, layer=l,
                                      states=states)
        (ssd_lat,) = _ssd_mixer(pa_lat, pb_lat, dt_lat, dtb_g, alog_g, dsk, context=False,
                                layer=l, h0f=h0f, h0b=h0b)

        branch = {True: (na_ctx, ssd_ctx, pb_ctx), False: (na_lat, ssd_lat, pb_lat)}
        w_merge = (w_na_out[l].astype(BF16), w_ssd_out[l].astype(BF16), w_o[l].astype(BF16))
        w_ffn = (w_up[l].astype(BF16), ffn_conv_w[l], ffn_conv_b[l][None, :], w_down[l].astype(BF16))
        for ctx in (True, False):
            na, ssd, pb = branch[ctx]
            x1 = _merge(xs[ctx], na, ssd, pb, mod[l], ssd_norm_g[l][None, :], *w_merge, context=ctx)
            xs[ctx] = _conv_ffn(x1, mod[l], norm2_g[l][None, :], *w_ffn, context=ctx)

    y_prompt = xs[True].reshape(BATCH, SEQ, D_MODEL)
    y_sample = xs[False].reshape(DEC_BATCH, DEC_SEQ, D_MODEL)
    cache_shape = (BATCH, DEPTH, SEQ, NA_HEADS, HEAD_DIM)
    state_shape = (BATCH, DEPTH, SSD_HEADS, SSD_HEADDIM, D_STATE)
    return (y_prompt, y_sample, caches[0].reshape(cache_shape), caches[1].reshape(cache_shape),
            states[0].reshape(state_shape), states[1].reshape(state_shape))
```

```python
import functools

import jax
import jax.numpy as jnp
from jax import lax
from jax.experimental import pallas as pl
from jax.experimental.pallas import tpu as pltpu

D_MODEL = 1024
BATCH = 16
SEQ = 256
DEPTH = 2
DEC_BATCH = 8
DEC_SEQ = 2048
PAST_LEN = 256
GRID_W = 64
NA_HEADS = 16
HEAD_DIM = 64
NA_WIDTH = NA_HEADS * HEAD_DIM
WIN_R = 8
WIN_C = 16
SSD_INNER = 2 * D_MODEL
SSD_HEADDIM = 64
SSD_HEADS = SSD_INNER // SSD_HEADDIM
SSD_GROUPS = 4
D_STATE = 128
SSD_CONV = 4
CHUNK = 128
CONV_DIM = SSD_INNER + 2 * SSD_GROUPS * D_STATE
D_FF = 2816
FFN_CONV = 3
EPS = 1e-6

N_CTX = BATCH * SEQ
N_LAT = DEC_BATCH * DEC_SEQ
GRID_ROWS = DEC_SEQ // GRID_W
HEADS_PER_GROUP = SSD_HEADS // SSD_GROUPS
GROUP_W = HEADS_PER_GROUP * SSD_HEADDIM

LANE = 128
SUBLANE = 8
MXU_N = 256

A_Q, A_K, A_V, A_Z, A_W = 0, 1024, 2048, 3072, 5120
B_X, B_B, B_C, B_GA, B_GB, B_W = 0, 2048, 2560, 3072, 4096, 5120
DT_W = SSD_GROUPS * LANE
IN_TILE_N = 2 * MXU_N
TM_IN = 512
TM_MERGE = 512
TM_FFN = 512
FF_TILE = 256
MASK_NEG = -1e30
EXPAND_TERMS = 1
LOG2E = 1.4426950408889634
QK_SQ_TERMS = 1

NA_QROWS = 4
NA_KROWS = 12
NA_Q = NA_QROWS * GRID_W
NA_KW = NA_KROWS * GRID_W
NA_KBLK = 256
NA_STEPS = GRID_ROWS // NA_QROWS
NA_VARIANTS = 3
NA_LOCKSTEP = 4

F32 = jnp.float32
BF16 = jnp.bfloat16
NT = (((1,), (1,)), ((), ()))

VMEM_LIMIT = 56 * 1024 * 1024


def _sigmoid(x):
    return 0.5 + 0.5 * jnp.tanh(0.5 * x)


def _silu(x):
    h = 0.5 * x
    return h + h * jnp.tanh(h)


def _dot(a, b):
    return jnp.dot(a, b, preferred_element_type=F32)


def _dot_nt(a, b):
    return lax.dot_general(a, b, NT, preferred_element_type=F32)


def _split_dot(a, b, terms):
    acc = None
    r = a
    for t in range(terms):
        p = r.astype(BF16)
        d = _dot(p, b)
        acc = d if acc is None else acc + d
        if t + 1 < terms:
            r = r - p.astype(F32)
    return acc


def _mod_row(i, tm, context):
    return 0 if context else 1 + (i * tm) // DEC_SEQ


def _resident(shape):
    return pl.BlockSpec(shape, lambda *_: (0,) * len(shape), pipeline_mode=pl.Buffered(1))


def _norm_mod(xv, g, shift, scale):
    ms = jnp.mean(xv * xv, axis=-1, keepdims=True)
    return (xv * lax.rsqrt(ms + EPS) * g * (1.0 + scale) + shift).astype(BF16)


def _mod_kernel(c_ref, w_ref, b_ref, o_ref):
    s = _silu(c_ref[...]).astype(BF16)
    o_ref[...] = _dot(s, w_ref[...].astype(BF16)) + b_ref[...]


def _modulation(cond, w_ada, b_ada):
    tn = 1536
    return pl.pallas_call(
        _mod_kernel,
        out_shape=jax.ShapeDtypeStruct((DEPTH, 16, 6 * D_MODEL), F32),
        grid=(DEPTH, 6 * D_MODEL // tn),
        in_specs=[
            pl.BlockSpec((16, D_MODEL), lambda l, j: (0, 0)),
            pl.BlockSpec((None, D_MODEL, tn), lambda l, j: (l, 0, j)),
            pl.BlockSpec((None, 1, tn), lambda l, j: (l, 0, j)),
        ],
        out_specs=pl.BlockSpec((None, 16, tn), lambda l, j: (l, 0, j)),
        compiler_params=pltpu.CompilerParams(
            dimension_semantics=("parallel", "parallel"), vmem_limit_bytes=VMEM_LIMIT),
        name="adaln_mod",
    )(cond, w_ada, b_ada)


def _inproj_a_kernel(x_ref, mod_ref, g_ref, w_ref, qkg_ref, bd_ref, *rest, emit_kv):
    if emit_kv:
        _, _, p_ref, k_ref, v_ref, h_ref = rest
    else:
        p_ref, h_ref = rest
    h_ref[...] = _norm_mod(x_ref[...], g_ref[...], mod_ref[:, 0:D_MODEL], mod_ref[:, D_MODEL:2 * D_MODEL])
    for c0 in range(0, A_W, IN_TILE_N):
        cols = slice(c0, c0 + IN_TILE_N)
        acc = _dot(h_ref[...], w_ref[:, cols])
        if c0 < A_V:
            sq = acc * acc
            ss = jnp.concatenate(
                [_split_dot(sq[:, t:t + MXU_N], bd_ref[...], QK_SQ_TERMS) for t in range(0, IN_TILE_N, MXU_N)],
                axis=1)
            acc = acc * lax.rsqrt(ss * (1.0 / HEAD_DIM) + EPS) * qkg_ref[:, cols]
        p_ref[:, cols] = acc.astype(BF16)
        if emit_kv and A_K <= c0 < A_Z:
            dst, d0 = (k_ref, c0 - A_K) if c0 < A_V else (v_ref, c0 - A_V)
            for sq_i in range(acc.shape[0] // SEQ):
                dst[sq_i, :, d0:d0 + IN_TILE_N] = acc[sq_i * SEQ:(sq_i + 1) * SEQ]


def _in_projection_a(x, mod, g1, w, qkg, bd, *, context, layer=0, caches=()):
    tm = TM_IN
    n_rows = x.shape[0]
    out_shape = [jax.ShapeDtypeStruct((n_rows, A_W), BF16)]
    out_specs = [pl.BlockSpec((tm, A_W), lambda i: (i, 0))]
    extra_in, aliases = [], {}
    if context:
        out_shape += [jax.ShapeDtypeStruct(c.shape, F32) for c in caches]
        out_specs += [pl.BlockSpec((tm // SEQ, None, SEQ, NA_WIDTH), lambda i: (i, layer, 0, 0))] * 2
        extra_in = [pl.BlockSpec(memory_space=pl.ANY)] * 2
        aliases = {6: 1, 7: 2}
    return pl.pallas_call(
        functools.partial(_inproj_a_kernel, emit_kv=context),
        out_shape=tuple(out_shape),
        grid=(n_rows // tm,),
        input_output_aliases=aliases,
        in_specs=[
            pl.BlockSpec((tm, D_MODEL), lambda i: (i, 0)),
            pl.BlockSpec((None, 1, 6 * D_MODEL), lambda i: (_mod_row(i, tm, context), 0, 0)),
            pl.BlockSpec((1, D_MODEL), lambda i: (0, 0)),
            _resident((D_MODEL, A_W)),
            pl.BlockSpec((1, A_V), lambda i: (0, 0)),
            pl.BlockSpec((MXU_N, MXU_N), lambda i: (0, 0)),
        ] + extra_in,
        out_specs=tuple(out_specs),
        scratch_shapes=[pltpu.VMEM((tm, D_MODEL), BF16)],
        compiler_params=pltpu.CompilerParams(
            dimension_semantics=("parallel",), vmem_limit_bytes=VMEM_LIMIT),
        name="in_proj_a_ctx" if context else "in_proj_a_lat",
    )(x, mod, g1, w, qkg, bd, *caches)


def _inproj_b_kernel(x_ref, xp_ref, xn_ref, mod_ref, g_ref, w_ref, cw_ref, cb_ref,
                     p_ref, dt_ref, h_ref, *, tm, seq_len):
    i = pl.program_id(0)
    g = g_ref[...]
    shift = mod_ref[:, 0:D_MODEL]
    scale = mod_ref[:, D_MODEL:2 * D_MODEL]
    h_ref[0:tm, :] = _norm_mod(x_ref[...], g, shift, scale)
    h_ref[tm:tm + 2 * SUBLANE, :] = _norm_mod(
        jnp.concatenate([xp_ref[...], xn_ref[...]], axis=0), g, shift, scale)

    ext = tm + 2 * SUBLANE
    start = i * tm
    row = lax.broadcasted_iota(jnp.int32, (tm, 1), 0)
    pos = (start + row) & (seq_len - 1)
    inner_edges = seq_len < tm
    keep_before = (start & (seq_len - 1)) != 0
    keep_after = ((start + tm) & (seq_len - 1)) != 0
    for c0 in range(0, B_GA, IN_TILE_N):
        cols = slice(c0, c0 + IN_TILE_N)
        pre = _dot(h_ref[...], w_ref[:, cols])
        before = jnp.where(keep_before, pre[tm:tm + SUBLANE], 0.0)
        after = jnp.where(keep_after, pre[tm + SUBLANE:ext], 0.0)
        v = jnp.concatenate([before, pre[0:tm], after], axis=0)
        back2 = pltpu.roll(v, 2, 0)[SUBLANE:SUBLANE + tm]
        back1 = pltpu.roll(v, 1, 0)[SUBLANE:SUBLANE + tm]
        ahead = pltpu.roll(v, ext - 1, 0)[SUBLANE:SUBLANE + tm]
        if inner_edges:
            back2 = jnp.where(pos >= 2, back2, 0.0)
            back1 = jnp.where(pos >= 1, back1, 0.0)
            ahead = jnp.where(pos != seq_len - 1, ahead, 0.0)
        cw = cw_ref[:, cols]
        u = cw[0:1] * back2 + cw[1:2] * back1 + cw[2:3] * pre[0:tm] + cw[3:4] * ahead + cb_ref[:, cols]
        p_ref[:, cols] = _silu(u).astype(BF16)
    for c0 in range(B_GA, B_W, IN_TILE_N):
        cols = slice(c0, c0 + IN_TILE_N)
        p_ref[:, cols] = _dot(h_ref[0:tm, :], w_ref[:, cols]).astype(BF16)
    dt_ref[...] = _dot(h_ref[0:tm, :], w_ref[:, B_W:B_W + DT_W])


def _in_projection_b(x, mod, g1, w, conv_w, conv_b, *, context):
    tm = TM_IN
    n_rows = x.shape[0]
    hb = tm // SUBLANE
    last_h = n_rows // SUBLANE - 1
    return pl.pallas_call(
        functools.partial(_inproj_b_kernel, tm=tm, seq_len=SEQ if context else DEC_SEQ),
        out_shape=(jax.ShapeDtypeStruct((n_rows, B_W), BF16), jax.ShapeDtypeStruct((n_rows, DT_W), F32)),
        grid=(n_rows // tm,),
        in_specs=[
            pl.BlockSpec((tm, D_MODEL), lambda i: (i, 0)),
            pl.BlockSpec((SUBLANE, D_MODEL), lambda i: (jnp.maximum(i * hb - 1, 0), 0)),
            pl.BlockSpec((SUBLANE, D_MODEL), lambda i: (jnp.minimum((i + 1) * hb, last_h), 0)),
            pl.BlockSpec((None, 1, 6 * D_MODEL), lambda i: (_mod_row(i, tm, context), 0, 0)),
            pl.BlockSpec((1, D_MODEL), lambda i: (0, 0)),
            _resident((D_MODEL, B_W + DT_W)),
            pl.BlockSpec((SSD_CONV, CONV_DIM), lambda i: (0, 0)),
            pl.BlockSpec((1, CONV_DIM), lambda i: (0, 0)),
        ],
        out_specs=(pl.BlockSpec((tm, B_W), lambda i: (i, 0)), pl.BlockSpec((tm, DT_W), lambda i: (i, 0))),
        scratch_shapes=[pltpu.VMEM((tm + 2 * SUBLANE, D_MODEL), BF16)],
        compiler_params=pltpu.CompilerParams(
            dimension_semantics=("parallel",), vmem_limit_bytes=VMEM_LIMIT),
        name="in_proj_b_ctx" if context else "in_proj_b_lat",
    )(x, x, x, mod, g1, w, conv_w, conv_b)


def _head_masks():
    lane = lax.broadcasted_iota(jnp.int32, (1, LANE), 1)
    return lane < HEAD_DIM


def _ctx_attn_kernel(q_ref, k_ref, v_ref, o_ref):
    lo = _head_masks()
    for c0 in range(0, NA_WIDTH, LANE):
        cols = slice(c0, c0 + LANE)
        q = q_ref[:, cols]
        k = k_ref[:, cols]
        v = v_ref[:, cols]
        outs = []
        for hh in range(2):
            m = lo if hh == 0 else jnp.logical_not(lo)
            qm = jnp.where(m, q, jnp.zeros_like(q))
            s = _dot_nt(qm, k)
            mx = jnp.max(s, axis=-1, keepdims=True)
            p = jnp.exp2(s - mx)
            l = jnp.sum(p, axis=-1, keepdims=True)
            outs.append(_dot(p.astype(BF16), v) / l)
        o_ref[:, cols] = jnp.where(lo, outs[0], outs[1]).astype(BF16)


def _context_attention(pa):
    blk = (SEQ, NA_WIDTH)
    return pl.pallas_call(
        _ctx_attn_kernel,
        out_shape=jax.ShapeDtypeStruct((N_CTX, NA_WIDTH), BF16),
        grid=(BATCH,),
        in_specs=[
            pl.BlockSpec(blk, lambda b: (b, A_Q // NA_WIDTH)),
            pl.BlockSpec(blk, lambda b: (b, A_K // NA_WIDTH)),
            pl.BlockSpec(blk, lambda b: (b, A_V // NA_WIDTH)),
        ],
        out_specs=pl.BlockSpec(blk, lambda b: (b, 0)),
        compiler_params=pltpu.CompilerParams(
            dimension_semantics=("parallel",), vmem_limit_bytes=VMEM_LIMIT),
        name="ctx_attn",
    )(pa, pa, pa)


def _na_bias_kernel(r_ref, o_ref):
    ck = lax.broadcasted_iota(jnp.int32, (GRID_W, LANE), 0)
    ln = lax.broadcasted_iota(jnp.int32, (GRID_W, LANE), 1)
    cq = ln & (GRID_W - 1)
    cs = jnp.clip(cq - WIN_C // 2, 0, GRID_W - WIN_C)
    col_ok = jnp.logical_and(ck >= cs, ck < cs + WIN_C)
    left = ln < GRID_W
    n_dr = 2 * WIN_R - 1
    toep = []
    for dr in range(n_dr):
        base = jnp.broadcast_to(r_ref[dr:dr + 1, :] * LOG2E, (GRID_W, LANE))
        toep.append([pltpu.roll(base, s * GRID_W, 1, stride=1, stride_axis=0) for s in range(2)])
    neg = jnp.full((GRID_W, LANE), MASK_NEG, F32)

    def rel(v, j, q4):
        if v == 0:
            return j - q4 + WIN_R - 1, j < WIN_R
        if v == 1:
            return j - q4 + WIN_R // 2 - 1, q4 <= j < q4 + WIN_R
        return j - q4 - 1, j >= NA_KROWS - WIN_R

    for v in range(NA_VARIANTS):
        for j in range(NA_KROWS):
            for lb in range(NA_QROWS // 2):
                halves = []
                for s in range(2):
                    dr, ok = rel(v, j, 2 * lb + s)
                    halves.append(jnp.where(col_ok, toep[dr][s], neg) if ok else neg)
                o_ref[v, j * GRID_W:(j + 1) * GRID_W, lb * LANE:(lb + 1) * LANE] = (
                    jnp.where(left, halves[0], halves[1]))


def _na_bias(rrev):
    return pl.pallas_call(
        _na_bias_kernel,
        out_shape=jax.ShapeDtypeStruct((NA_HEADS // 2, NA_VARIANTS, NA_KW, 2 * NA_Q), F32),
        grid=(NA_HEADS,),
        in_specs=[pl.BlockSpec((None, 2 * WIN_R, LANE), lambda h: (h, 0, 0))],
        out_specs=pl.BlockSpec((None, NA_VARIANTS, NA_KW, NA_Q), lambda h: (h // 2, 0, 0, h % 2)),
        compiler_params=pltpu.CompilerParams(
            dimension_semantics=("parallel",), vmem_limit_bytes=VMEM_LIMIT),
        name="na_bias",
    )(rrev)


def _na_kernel(q_ref, k_ref, v_ref, kc_ref, vc_ref, bias_ref, o_ref, vt_ref):
    lo = _head_masks()
    kc = kc_ref[...].astype(BF16)
    vct = vc_ref[...].T.astype(BF16)
    for t in range(DEC_SEQ // NA_KBLK):
        vt_ref[t] = v_ref[t * NA_KBLK:(t + 1) * NA_KBLK, :].astype(F32).T.astype(BF16)
    n_wblk = NA_KW // NA_KBLK

    def steps(ms):
        n = range(len(ms))
        blk0 = [min(max(m - 1, 0), DEC_SEQ // NA_KBLK - n_wblk) for m in ms]
        var = [0 if m == 0 else (2 if m == NA_STEPS - 1 else 1) for m in ms]
        qg = [q_ref[m * NA_Q:(m + 1) * NA_Q, :] for m in ms]
        kw = [k_ref[b * NA_KBLK:b * NA_KBLK + NA_KW, :] for b in blk0]
        qcat = [jnp.concatenate([jnp.where(lo, q, jnp.zeros_like(q)), jnp.where(lo, jnp.zeros_like(q), q)], axis=0)
                for q in qg]
        sw = [_dot_nt(kw[i], qcat[i]) + bias_ref[var[i]] for i in n]
        sc = [_dot_nt(kc, qcat[i]) for i in n]
        mx = [jnp.maximum(jnp.max(sw[i], axis=0, keepdims=True), jnp.max(sc[i], axis=0, keepdims=True)) for i in n]
        pw = [jnp.exp2(sw[i] - mx[i]) for i in n]
        pc = [jnp.exp2(sc[i] - mx[i]) for i in n]
        l = [jnp.sum(pw[i], axis=0, keepdims=True) + jnp.sum(pc[i], axis=0, keepdims=True) for i in n]
        pwb = [p.astype(BF16) for p in pw]
        o = [_dot(vct, p.astype(BF16)) for p in pc]
        for t in range(n_wblk):
            o = [o[i] + _dot(vt_ref[blk0[i] + t], pwb[i][t * NA_KBLK:(t + 1) * NA_KBLK, :]) for i in n]
        for i in n:
            on = o[i] / l[i]
            both = jnp.concatenate([on[0:HEAD_DIM, 0:NA_Q], on[HEAD_DIM:2 * HEAD_DIM, NA_Q:2 * NA_Q]], axis=0)
            o_ref[ms[i] * NA_Q:(ms[i] + 1) * NA_Q, :] = both.T.astype(BF16)

    for m in range(0, NA_STEPS, NA_LOCKSTEP):
        steps(list(range(m, m + NA_LOCKSTEP)))


def _neighbourhood_attention(pa, cache_k, cache_v, bias, layer):
    hp = NA_HEADS // 2
    blk = (DEC_SEQ, LANE)
    cblk = (None, None, PAST_LEN, LANE)
    return pl.pallas_call(
        _na_kernel,
        out_shape=jax.ShapeDtypeStruct((N_LAT, NA_WIDTH), BF16),
        grid=(hp, DEC_BATCH),
        in_specs=[
            pl.BlockSpec(blk, lambda h, b: (b, A_Q // LANE + h)),
            pl.BlockSpec(blk, lambda h, b: (b, A_K // LANE + h)),
            pl.BlockSpec(blk, lambda h, b: (b, A_V // LANE + h)),
            pl.BlockSpec(cblk, lambda h, b: (b, layer, 0, h)),
            pl.BlockSpec(cblk, lambda h, b: (b, layer, 0, h)),
            pl.BlockSpec((None, NA_VARIANTS, NA_KW, 2 * NA_Q), lambda h, b: (h, 0, 0, 0)),
        ],
        out_specs=pl.BlockSpec(blk, lambda h, b: (b, h)),
        scratch_shapes=[pltpu.VMEM((DEC_SEQ // NA_KBLK, LANE, NA_KBLK), BF16)],
        compiler_params=pltpu.CompilerParams(
            dimension_semantics=("parallel", "parallel"), vmem_limit_bytes=VMEM_LIMIT),
        name="na_attn",
    )(pa, pa, pa, cache_k, cache_v, bias)


def _rpb_reversed(rpb_l):
    n = WIN_C - 1
    pos = rpb_l[:, :, n::-1]
    neg = rpb_l[:, :, :n:-1]
    z = jnp.zeros(rpb_l.shape[:2] + (LANE - 2 * n - 1,), F32)
    r = jnp.concatenate([pos.astype(F32), z, neg.astype(F32)], axis=-1)
    return jnp.pad(r, ((0, 0), (0, 1), (0, 0)))


def _ssd_kernel(*refs, seq_len, has_h0, emit_state):
    it = iter(refs)
    x_ref, b_ref, c_ref, z_ref, dt_ref, dtb_ref, alog_ref, dsk_ref = (next(it) for _ in range(8))
    if has_h0:
        h0f_ref, h0b_ref = next(it), next(it)
    if emit_state:
        next(it), next(it)
    y_ref = next(it)
    if emit_state:
        hf_ref, hb_ref = next(it), next(it)
    bt_s, cum_s, dtv_s, y_s, st_s = (next(it) for _ in range(5))

    n_chunks = seq_len // CHUNK
    n_dir = 2 * HEADS_PER_GROUP

    lane = lax.broadcasted_iota(jnp.int32, (1, LANE), 1)
    ii = lax.broadcasted_iota(jnp.int32, (CHUNK, CHUNK), 0)
    jj = lax.broadcasted_iota(jnp.int32, (CHUNK, CHUNK), 1)
    lower = jj < ii
    diag = jj == ii
    tril = jnp.where(jj <= ii, 1.0, 0.0).astype(BF16)
    triu = jnp.where(jj >= ii, 1.0, 0.0).astype(BF16)
    er = lax.broadcasted_iota(jnp.int32, (LANE, GROUP_W), 0)
    ec = lax.broadcasted_iota(jnp.int32, (LANE, GROUP_W), 1) // SSD_HEADDIM
    exp_f = jnp.where(er == ec, 1.0, 0.0).astype(BF16)
    exp_b = jnp.where(er == ec + HEADS_PER_GROUP, 1.0, 0.0).astype(BF16)
    lo64 = lane < SSD_HEADDIM
    is_fwd = lane < HEADS_PER_GROUP
    a_neg = -jnp.exp(alog_ref[...])
    dtb = dtb_ref[...]
    dsk = dsk_ref[...]
    row_fwd = lax.broadcasted_iota(jnp.int32, (n_dir, 1), 0) < HEADS_PER_GROUP
    pad_rows = jnp.zeros((LANE - n_dir, CHUNK), F32)

    if has_h0:
        st_s[...] = h0f_ref[...].T
    else:
        st_s[...] = jnp.zeros((D_STATE, GROUP_W), F32)

    def local(cs):
        rows = [slice(c * CHUNK, (c + 1) * CHUNK) for c in cs]
        xcb = [x_ref[r, :] for r in rows]
        bcb = [b_ref[r, :] for r in rows]
        ccb = [c_ref[r, :] for r in rows]
        btb = [b.astype(F32).T.astype(BF16) for b in bcb]
        for r, b in zip(rows, btb):
            bt_s[r, :] = b
        raw = [dt_ref[r, :].T[0:n_dir, :] + dtb for r in rows]
        e = [jnp.exp(-jnp.abs(x)) for x in raw]
        log1p_e = [jnp.where(v < 1e-4, v * (1.0 - 0.5 * v), jnp.log(1.0 + v)) for v in e]
        dt_t = [jnp.maximum(x, 0.0) + l for x, l in zip(raw, log1p_e)]
        a = [d * a_neg for d in dt_t]
        cum_t = [jnp.where(row_fwd, _split_dot(v, triu, 3), _split_dot(v, tril, 3)) for v in a]
        cum = [jnp.concatenate([v, pad_rows], axis=0).T for v in cum_t]
        dtv = [jnp.concatenate([v, pad_rows], axis=0).T for v in dt_t]
        for k, r in enumerate(rows):
            cum_s[r, :] = cum[k]
            dtv_s[r, :] = dtv[k]
        g = [_dot_nt(cc, bb) for cc, bb in zip(ccb, bcb)]
        src_t = [c_t - jnp.log(d) for c_t, d in zip(cum_t, dt_t)]
        parts = [[] for _ in cs]
        for pp in range(HEADS_PER_GROUP // 2):
            for k in range(len(cs)):
                ws = []
                for hh in (2 * pp, 2 * pp + 1):
                    hb = HEADS_PER_GROUP + hh
                    seg = jnp.where(lower, cum[k][:, hh:hh + 1] - src_t[k][hh:hh + 1, :],
                                    cum[k][:, hb:hb + 1] - src_t[k][hb:hb + 1, :])
                    both = jnp.exp(seg) + jnp.where(diag, dt_t[k][hh:hh + 1, :], 0.0)
                    ws.append((g[k] * both).astype(BF16))
                wp = jnp.concatenate(ws, axis=1)
                xp = xcb[k][:, pp * LANE:(pp + 1) * LANE]
                zero = jnp.zeros_like(xp)
                rhs = jnp.concatenate([jnp.where(lo64, xp, zero), jnp.where(lo64, zero, xp)], axis=0)
                parts[k].append(_dot(wp, rhs))
        y = [jnp.concatenate(p, axis=1) for p in parts]
        ef_x = [_split_dot(jnp.exp(v), exp_f, EXPAND_TERMS) for v in cum]
        to_end = [jnp.where(is_fwd, v[CHUNK - 1:CHUNK, :] - v, 0.0) for v in cum]
        ff_x = [_split_dot(jnp.exp(t) * d, exp_f, EXPAND_TERMS) for t, d in zip(to_end, dtv)]
        return [(rows[k], xcb[k], ccb[k], btb[k], y[k], ef_x[k], ff_x[k]) for k in range(len(cs))]

    def carry_fwd(vals):
        r, xcb, ccb, btb, y, ef_x, ff_x = vals
        xc = xcb.astype(F32)
        st = st_s[...]
        y_s[r, :] = y + _dot(ccb, st.astype(BF16)) * ef_x + dsk * xc
        st_s[...] = st * ef_x[CHUNK - 1:CHUNK, :] + _dot(btb, (xc * ff_x).astype(BF16))

    step = 4 if n_chunks % 4 == 0 else (2 if n_chunks % 2 == 0 else 1)
    for c in range(0, n_chunks, step):
        for vals in local(list(range(c, c + step))):
            carry_fwd(vals)

    if emit_state:
        hf_ref[...] = st_s[...].T
    if has_h0:
        st_s[...] = h0b_ref[...].T
    else:
        st_s[...] = jnp.zeros((D_STATE, GROUP_W), F32)

    def local_bwd(cs):
        rows = [slice(c * CHUNK, (c + 1) * CHUNK) for c in cs]
        cum = [cum_s[r, :] for r in rows]
        dtv = [dtv_s[r, :] for r in rows]
        eb_x = [_split_dot(jnp.exp(v), exp_b, EXPAND_TERMS) for v in cum]
        to_start = [jnp.where(is_fwd, 0.0, v[0:1, :] - v) for v in cum]
        fb_x = [_split_dot(jnp.exp(t) * d, exp_b, EXPAND_TERMS) for t, d in zip(to_start, dtv)]
        xw = [(x_ref[r, :].astype(F32) * f).astype(BF16) for r, f in zip(rows, fb_x)]
        gate = [_silu(z_ref[r, :].astype(F32)) for r in rows]
        return [(rows[k], eb_x[k], xw[k], gate[k]) for k in range(len(cs))]

    def carry_bwd(vals):
        r, eb_x, xw, gate = vals
        st = st_s[...]
        y = y_s[r, :] + _dot(c_ref[r, :], st.astype(BF16)) * eb_x
        st_s[...] = st * eb_x[0:1, :] + _dot(bt_s[r, :], xw)
        y_ref[r, :] = (y * gate).astype(BF16)

    for c in reversed(range(0, n_chunks, step)):
        for vals in reversed(local_bwd(list(range(c, c + step)))):
            carry_bwd(vals)

    if emit_state:
        hb_ref[...] = st_s[...].T


def _ssd_mixer(pa, pb, dt32, dtb_g, alog_g, dsk, *, context, layer=0, h0f=None, h0b=None, states=()):
    seq_len = SEQ if context else DEC_SEQ
    n_seq = BATCH if context else DEC_BATCH
    has_h0 = h0f is not None
    emit_state = context
    xw_, bw_ = GROUP_W, D_STATE

    in_specs = [
        pl.BlockSpec((seq_len, xw_), lambda b, g: (b, B_X // xw_ + g)),
        pl.BlockSpec((seq_len, bw_), lambda b, g: (b, B_B // bw_ + g)),
        pl.BlockSpec((seq_len, bw_), lambda b, g: (b, B_C // bw_ + g)),
        pl.BlockSpec((seq_len, xw_), lambda b, g: (b, A_Z // xw_ + g)),
        pl.BlockSpec((seq_len, LANE), lambda b, g: (b, g)),
        pl.BlockSpec((None, 2 * HEADS_PER_GROUP, 1), lambda b, g: (g, 0, 0)),
        pl.BlockSpec((None, 2 * HEADS_PER_GROUP, 1), lambda b, g: (g, 0, 0)),
        pl.BlockSpec((1, xw_), lambda b, g: (0, g)),
    ]
    args = [pb, pb, pb, pa, dt32, dtb_g, alog_g, dsk]
    if has_h0:
        st_spec = pl.BlockSpec((None, None, xw_, D_STATE), lambda b, g: (b, layer, g, 0))
        in_specs += [st_spec, st_spec]
        args += [h0f, h0b]
    out_shape = [jax.ShapeDtypeStruct((n_seq * seq_len, SSD_INNER), BF16)]
    out_specs = [pl.BlockSpec((seq_len, xw_), lambda b, g: (b, g))]
    aliases = {}
    if emit_state:
        aliases = {len(args): 1, len(args) + 1: 2}
        in_specs += [pl.BlockSpec(memory_space=pl.ANY)] * 2
        args += list(states)
        so = pl.BlockSpec((None, None, xw_, D_STATE), lambda b, g: (b, layer, g, 0))
        out_shape += [jax.ShapeDtypeStruct(st.shape, F32) for st in states]
        out_specs += [so, so]
    scratch = [
        pltpu.VMEM((seq_len, bw_), BF16),
        pltpu.VMEM((seq_len, LANE), F32),
        pltpu.VMEM((seq_len, LANE), F32),
        pltpu.VMEM((seq_len, xw_), F32),
        pltpu.VMEM((D_STATE, xw_), F32),
    ]
    return pl.pallas_call(
        functools.partial(_ssd_kernel, seq_len=seq_len, has_h0=has_h0, emit_state=emit_state),
        out_shape=tuple(out_shape),
        grid=(n_seq, SSD_GROUPS),
        in_specs=in_specs,
        out_specs=tuple(out_specs),
        input_output_aliases=aliases,
        scratch_shapes=scratch,
        compiler_params=pltpu.CompilerParams(
            dimension_semantics=("parallel", "parallel"), vmem_limit_bytes=VMEM_LIMIT),
        name="ssd_ctx" if context else "ssd_lat",
    )(*args)


def _merge_kernel(x_ref, na_ref, s_ref, ga_ref, gb_ref, mod_ref, sg_ref, wna_ref, wssd_ref, wo_ref, o_ref):
    y = s_ref[...].astype(F32)
    ms = jnp.mean(y * y, axis=-1, keepdims=True)
    yn = (y * lax.rsqrt(ms + EPS) * sg_ref[...]).astype(BF16)
    a = _dot(na_ref[...], wna_ref[...])
    s = _dot(yn, wssd_ref[...])
    mix = _sigmoid(ga_ref[...].astype(F32)) * a + _sigmoid(gb_ref[...].astype(F32)) * s
    o = _dot(mix.astype(BF16), wo_ref[...])
    o_ref[...] = x_ref[...] + mod_ref[:, 2 * D_MODEL:3 * D_MODEL] * o


def _merge(x, na, ssd, pb, mod, ssd_g, w_na, w_ssd, w_o, *, context):
    tm = TM_MERGE
    n_rows = x.shape[0]
    return pl.pallas_call(
        _merge_kernel,
        out_shape=jax.ShapeDtypeStruct((n_rows, D_MODEL), F32),
        grid=(n_rows // tm,),
        in_specs=[
            pl.BlockSpec((tm, D_MODEL), lambda i: (i, 0)),
            pl.BlockSpec((tm, NA_WIDTH), lambda i: (i, 0)),
            pl.BlockSpec((tm, SSD_INNER), lambda i: (i, 0)),
            pl.BlockSpec((tm, D_MODEL), lambda i: (i, B_GA // D_MODEL)),
            pl.BlockSpec((tm, D_MODEL), lambda i: (i, B_GB // D_MODEL)),
            pl.BlockSpec((None, 1, 6 * D_MODEL), lambda i: (_mod_row(i, tm, context), 0, 0)),
            pl.BlockSpec((1, SSD_INNER), lambda i: (0, 0)),
            _resident((NA_WIDTH, D_MODEL)),
            _resident((SSD_INNER, D_MODEL)),
            _resident((D_MODEL, D_MODEL)),
        ],
        out_specs=pl.BlockSpec((tm, D_MODEL), lambda i: (i, 0)),
        compiler_params=pltpu.CompilerParams(
            dimension_semantics=("parallel",), vmem_limit_bytes=VMEM_LIMIT),
        name="merge_ctx" if context else "merge_lat",
    )(x, na, ssd, pb, pb, mod, ssd_g, w_na, w_ssd, w_o)


def _ffn_kernel(x_ref, xp_ref, xn_ref, mod_ref, g_ref, wup_ref, cw_ref, cb_ref, wd_ref,
                o_ref, h_ref, act_ref, *, tm, seq_len):
    i = pl.program_id(0)
    g = g_ref[...]
    shift = mod_ref[:, 3 * D_MODEL:4 * D_MODEL]
    scale = mod_ref[:, 4 * D_MODEL:5 * D_MODEL]
    h_ref[0:tm, :] = _norm_mod(x_ref[...], g, shift, scale)
    h_ref[tm:tm + 2 * SUBLANE, :] = _norm_mod(
        jnp.concatenate([xp_ref[...], xn_ref[...]], axis=0), g, shift, scale)

    ext = tm + 2 * SUBLANE
    start = i * tm
    row = lax.broadcasted_iota(jnp.int32, (tm, 1), 0)
    pos = (start + row) & (seq_len - 1)
    inner_edges = seq_len < tm
    keep_before = (start & (seq_len - 1)) != 0
    keep_after = ((start + tm) & (seq_len - 1)) != 0

    def conv(c0):
        cols = slice(c0, c0 + FF_TILE)
        pre = _dot(h_ref[...], wup_ref[:, cols])
        before = jnp.where(keep_before, pre[tm:tm + SUBLANE], 0.0)
        after = jnp.where(keep_after, pre[tm + SUBLANE:ext], 0.0)
        v = jnp.concatenate([before, pre[0:tm], after], axis=0)
        left = pltpu.roll(v, 1, 0)[SUBLANE:SUBLANE + tm]
        right = pltpu.roll(v, ext - 1, 0)[SUBLANE:SUBLANE + tm]
        if inner_edges:
            left = jnp.where(pos != 0, left, 0.0)
            right = jnp.where(pos != seq_len - 1, right, 0.0)
        cw = cw_ref[:, cols]
        return cw[0:1] * left + cw[1:2] * pre[0:tm] + cw[2:3] * right + cb_ref[:, cols]

    for c0 in range(0, D_FF, FF_TILE):
        val = conv(c0)
        gate = conv(D_FF + c0)
        act_ref[:, c0:c0 + FF_TILE] = (_silu(gate) * val).astype(BF16)

    acc = _dot(act_ref[:, 0:FF_TILE], wd_ref[0:FF_TILE, :])
    for c0 in range(FF_TILE, D_FF, FF_TILE):
        acc = acc + _dot(act_ref[:, c0:c0 + FF_TILE], wd_ref[c0:c0 + FF_TILE, :])
    o_ref[...] = x_ref[...] + mod_ref[:, 5 * D_MODEL:6 * D_MODEL] * acc


def _conv_ffn(x, mod, g2, w_up, conv_w, conv_b, w_down, *, context):
    tm = TM_FFN
    n_rows = x.shape[0]
    hb = tm // SUBLANE
    last_h = n_rows // SUBLANE - 1
    return pl.pallas_call(
        functools.partial(_ffn_kernel, tm=tm, seq_len=SEQ if context else DEC_SEQ),
        out_shape=jax.ShapeDtypeStruct((n_rows, D_MODEL), F32),
        grid=(n_rows // tm,),
        in_specs=[
            pl.BlockSpec((tm, D_MODEL), lambda i: (i, 0)),
            pl.BlockSpec((SUBLANE, D_MODEL), lambda i: (jnp.maximum(i * hb - 1, 0), 0)),
            pl.BlockSpec((SUBLANE, D_MODEL), lambda i: (jnp.minimum((i + 1) * hb, last_h), 0)),
            pl.BlockSpec((None, 1, 6 * D_MODEL), lambda i: (_mod_row(i, tm, context), 0, 0)),
            pl.BlockSpec((1, D_MODEL), lambda i: (0, 0)),
            _resident((D_MODEL, 2 * D_FF)),
            _resident((FFN_CONV, 2 * D_FF)),
            _resident((1, 2 * D_FF)),
            _resident((D_FF, D_MODEL)),
        ],
        out_specs=pl.BlockSpec((tm, D_MODEL), lambda i: (i, 0)),
        scratch_shapes=[
            pltpu.VMEM((tm + 2 * SUBLANE, D_MODEL), BF16),
            pltpu.VMEM((tm, D_FF), BF16),
        ],
        compiler_params=pltpu.CompilerParams(
            dimension_semantics=("parallel",), vmem_limit_bytes=VMEM_LIMIT),
        name="conv_ffn_ctx" if context else "conv_ffn_lat",
    )(x, x, x, mod, g2, w_up, conv_w, conv_b, w_down)


def _w_in_group_b(w):
    x0 = A_W
    dt0 = x0 + CONV_DIM
    ga0 = dt0 + 2 * SSD_HEADS
    dt = w[:, dt0:ga0]
    hg = HEADS_PER_GROUP
    blocks = []
    for g in range(SSD_GROUPS):
        blocks += [dt[:, g * hg:(g + 1) * hg], dt[:, SSD_HEADS + g * hg:SSD_HEADS + (g + 1) * hg],
                   jnp.zeros((D_MODEL, LANE - 2 * hg), w.dtype)]
    return jnp.concatenate([w[:, x0:dt0], w[:, ga0:]] + blocks, axis=1).astype(BF16)


def _by_group(v):
    v = v.astype(F32).reshape(2, SSD_GROUPS, HEADS_PER_GROUP)
    return jnp.concatenate([v[0], v[1]], axis=1)[:, :, None]


def kernel(x_prompt, x_sample, c, cache_k, cache_v, state_ssd_fwd, state_ssd_bwd, c_ctx, w_ada, b_ada, norm1_g, w_in, q_norm_g, k_norm_g, rpb, ssd_conv_w, ssd_conv_b, a_log, dt_bias, d_skip, ssd_norm_g, w_na_out, w_ssd_out, w_o, norm2_g, w_up, ffn_conv_w, ffn_conv_b, w_down):
    xs = {True: x_prompt.reshape(N_CTX, D_MODEL), False: x_sample.reshape(N_LAT, D_MODEL)}
    cond = jnp.concatenate([c_ctx[None, :], c, jnp.zeros((16 - 1 - DEC_BATCH, D_MODEL), F32)], axis=0)
    mod = _modulation(cond, w_ada, b_ada.reshape(DEPTH, 1, 6 * D_MODEL))
    mod = mod.reshape(DEPTH, 16, 1, 6 * D_MODEL)

    ck = cache_k.reshape(DEC_BATCH, DEPTH, PAST_LEN, NA_WIDTH)
    cv = cache_v.reshape(DEC_BATCH, DEPTH, PAST_LEN, NA_WIDTH)
    h0f = state_ssd_fwd.reshape(DEC_BATCH, DEPTH, SSD_INNER, D_STATE)
    h0b = state_ssd_bwd.reshape(DEC_BATCH, DEPTH, SSD_INNER, D_STATE)
    bd = (jnp.arange(MXU_N)[:, None] // HEAD_DIM == jnp.arange(MXU_N)[None, :] // HEAD_DIM).astype(BF16)

    caches = (jnp.zeros((BATCH, DEPTH, SEQ, NA_WIDTH), F32),) * 2
    states = (jnp.zeros((BATCH, DEPTH, SSD_INNER, D_STATE), F32),) * 2
    for l in range(DEPTH):
        qkg = jnp.concatenate([jnp.tile(q_norm_g[l] * (HEAD_DIM ** -0.5 * LOG2E), NA_HEADS),
                               jnp.tile(k_norm_g[l], NA_HEADS)])[None, :]
        g1 = norm1_g[l][None, :]
        w_a = w_in[l][:, :A_W].astype(BF16)
        w_b = _w_in_group_b(w_in[l])
        conv_b = ssd_conv_b[l][None, :]
        pa_ctx, *caches = _in_projection_a(xs[True], mod[l], g1, w_a, qkg, bd, context=True, layer=l, caches=caches)
        (pa_lat,) = _in_projection_a(xs[False], mod[l], g1, w_a, qkg, bd, context=False)
        pb_ctx, dt_ctx = _in_projection_b(xs[True], mod[l], g1, w_b, ssd_conv_w[l], conv_b, context=True)
        pb_lat, dt_lat = _in_projection_b(xs[False], mod[l], g1, w_b, ssd_conv_w[l], conv_b, context=False)

        na_ctx = _context_attention(pa_ctx)
        na_lat = _neighbourhood_attention(pa_lat, ck, cv, _na_bias(_rpb_reversed(rpb[l])), l)

        dtb_g = _by_group(dt_bias[l])
        alog_g = _by_group(a_log[l])
        dsk = jnp.repeat(d_skip[l], SSD_HEADDIM)[None, :]
        ssd_ctx, *states = _ssd_mixer(pa_ctx, pb_ctx, dt_ctx, dtb_g, alog_g, dsk, context=True, layer=l,
                                      states=states)
        (ssd_lat,) = _ssd_mixer(pa_lat, pb_lat, dt_lat, dtb_g, alog_g, dsk, context=False,
                                layer=l, h0f=h0f, h0b=h0b)

        branch = {True: (na_ctx, ssd_ctx, pb_ctx), False: (na_lat, ssd_lat, pb_lat)}
        w_merge = (w_na_out[l].astype(BF16), w_ssd_out[l].astype(BF16), w_o[l].astype(BF16))
        w_ffn = (w_up[l].astype(BF16), ffn_conv_w[l], ffn_conv_b[l][None, :], w_down[l].astype(BF16))
        for ctx in (True, False):
            na, ssd, pb = branch[ctx]
            x1 = _merge(xs[ctx], na, ssd, pb, mod[l], ssd_norm_g[l][None, :], *w_merge, context=ctx)
            xs[ctx] = _conv_ffn(x1, mod[l], norm2_g[l][None, :], *w_ffn, context=ctx)

    y_prompt = xs[True].reshape(BATCH, SEQ, D_MODEL)
    y_sample = xs[False].reshape(DEC_BATCH, DEC_SEQ, D_MODEL)
    cache_shape = (BATCH, DEPTH, SEQ, NA_HEADS, HEAD_DIM)
    state_shape = (BATCH, DEPTH, SSD_HEADS, SSD_HEADDIM, D_STATE)
    return (y_prompt, y_sample, caches[0].reshape(cache_shape), caches[1].reshape(cache_shape),
            states[0].reshape(state_shape), states[1].reshape(state_shape))
```

```python
import functools

import jax
import jax.numpy as jnp
from jax import lax
from jax.experimental import pallas as pl
from jax.experimental.pallas import tpu as pltpu

D_MODEL = 1024
BATCH = 16
SEQ = 256
DEPTH = 2
DEC_BATCH = 8
DEC_SEQ = 2048
PAST_LEN = 256
GRID_W = 64
NA_HEADS = 16
HEAD_DIM = 64
NA_WIDTH = NA_HEADS * HEAD_DIM
WIN_R = 8
WIN_C = 16
SSD_INNER = 2 * D_MODEL
SSD_HEADDIM = 64
SSD_HEADS = SSD_INNER // SSD_HEADDIM
SSD_GROUPS = 4
D_STATE = 128
SSD_CONV = 4
CHUNK = 128
CONV_DIM = SSD_INNER + 2 * SSD_GROUPS * D_STATE
D_FF = 2816
FFN_CONV = 3
EPS = 1e-6

N_CTX = BATCH * SEQ
N_LAT = DEC_BATCH * DEC_SEQ
GRID_ROWS = DEC_SEQ // GRID_W
HEADS_PER_GROUP = SSD_HEADS // SSD_GROUPS
GROUP_W = HEADS_PER_GROUP * SSD_HEADDIM

LANE = 128
SUBLANE = 8
MXU_N = 256

A_Q, A_K, A_V, A_Z, A_W = 0, 1024, 2048, 3072, 5120
B_X, B_B, B_C, B_GA, B_GB, B_W = 0, 2048, 2560, 3072, 4096, 5120
DT_W = SSD_GROUPS * LANE
IN_TILE_N = 2 * MXU_N
TM_IN = 512
TM_MERGE = 512
TM_FFN = 512
FF_TILE = 256
MASK_NEG = -1e30
EXPAND_TERMS = 1
LOG2E = 1.4426950408889634
QK_SQ_TERMS = 1

NA_QROWS = 4
NA_KROWS = 12
NA_Q = NA_QROWS * GRID_W
NA_KW = NA_KROWS * GRID_W
NA_KBLK = 256
NA_STEPS = GRID_ROWS // NA_QROWS
NA_VARIANTS = 3
NA_LOCKSTEP = 4

F32 = jnp.float32
BF16 = jnp.bfloat16
NT = (((1,), (1,)), ((), ()))

VMEM_LIMIT = 56 * 1024 * 1024


def _sigmoid(x):
    return 0.5 + 0.5 * jnp.tanh(0.5 * x)


def _silu(x):
    h = 0.5 * x
    return h + h * jnp.tanh(h)


def _dot(a, b):
    return jnp.dot(a, b, preferred_element_type=F32)


def _dot_nt(a, b):
    return lax.dot_general(a, b, NT, preferred_element_type=F32)


def _split_dot(a, b, terms):
    acc = None
    r = a
    for t in range(terms):
        p = r.astype(BF16)
        d = _dot(p, b)
        acc = d if acc is None else acc + d
        if t + 1 < terms:
            r = r - p.astype(F32)
    return acc


def _mod_row(i, tm, context):
    return 0 if context else 1 + (i * tm) // DEC_SEQ


def _resident(shape):
    return pl.BlockSpec(shape, lambda *_: (0,) * len(shape), pipeline_mode=pl.Buffered(1))


def _norm_mod(xv, g, shift, scale):
    ms = jnp.mean(xv * xv, axis=-1, keepdims=True)
    return (xv * lax.rsqrt(ms + EPS) * g * (1.0 + scale) + shift).astype(BF16)


def _mod_kernel(c_ref, w_ref, b_ref, o_ref):
    s = _silu(c_ref[...]).astype(BF16)
    o_ref[...] = _dot(s, w_ref[...].astype(BF16)) + b_ref[...]


def _modulation(cond, w_ada, b_ada):
    tn = 1536
    return pl.pallas_call(
        _mod_kernel,
        out_shape=jax.ShapeDtypeStruct((DEPTH, 16, 6 * D_MODEL), F32),
        grid=(DEPTH, 6 * D_MODEL // tn),
        in_specs=[
            pl.BlockSpec((16, D_MODEL), lambda l, j: (0, 0)),
            pl.BlockSpec((None, D_MODEL, tn), lambda l, j: (l, 0, j)),
            pl.BlockSpec((None, 1, tn), lambda l, j: (l, 0, j)),
        ],
        out_specs=pl.BlockSpec((None, 16, tn), lambda l, j: (l, 0, j)),
        compiler_params=pltpu.CompilerParams(
            dimension_semantics=("parallel", "parallel"), vmem_limit_bytes=VMEM_LIMIT),
        name="adaln_mod",
    )(cond, w_ada, b_ada)


def _inproj_a_kernel(x_ref, mod_ref, g_ref, w_ref, qkg_ref, bd_ref, *rest, emit_kv):
    if emit_kv:
        _, _, p_ref, k_ref, v_ref, h_ref = rest
    else:
        p_ref, h_ref = rest
    h_ref[...] = _norm_mod(x_ref[...], g_ref[...], mod_ref[:, 0:D_MODEL], mod_ref[:, D_MODEL:2 * D_MODEL])
    for c0 in range(0, A_W, IN_TILE_N):
        cols = slice(c0, c0 + IN_TILE_N)
        acc = _dot(h_ref[...], w_ref[:, cols])
        if c0 < A_V:
            sq = acc * acc
            ss = jnp.concatenate(
                [_split_dot(sq[:, t:t + MXU_N], bd_ref[...], QK_SQ_TERMS) for t in range(0, IN_TILE_N, MXU_N)],
                axis=1)
            acc = acc * lax.rsqrt(ss * (1.0 / HEAD_DIM) + EPS) * qkg_ref[:, cols]
        p_ref[:, cols] = acc.astype(BF16)
        if emit_kv and A_K <= c0 < A_Z:
            dst, d0 = (k_ref, c0 - A_K) if c0 < A_V else (v_ref, c0 - A_V)
            for sq_i in range(acc.shape[0] // SEQ):
                dst[sq_i, :, d0:d0 + IN_TILE_N] = acc[sq_i * SEQ:(sq_i + 1) * SEQ]


def _in_projection_a(x, mod, g1, w, qkg, bd, *, context, layer=0, caches=()):
    tm = TM_IN
    n_rows = x.shape[0]
    out_shape = [jax.ShapeDtypeStruct((n_rows, A_W), BF16)]
    out_specs = [pl.BlockSpec((tm, A_W), lambda i: (i, 0))]
    extra_in, aliases = [], {}
    if context:
        out_shape += [jax.ShapeDtypeStruct(c.shape, F32) for c in caches]
        out_specs += [pl.BlockSpec((tm // SEQ, None, SEQ, NA_WIDTH), lambda i: (i, layer, 0, 0))] * 2
        extra_in = [pl.BlockSpec(memory_space=pl.ANY)] * 2
        aliases = {6: 1, 7: 2}
    return pl.pallas_call(
        functools.partial(_inproj_a_kernel, emit_kv=context),
        out_shape=tuple(out_shape),
        grid=(n_rows // tm,),
        input_output_aliases=aliases,
        in_specs=[
            pl.BlockSpec((tm, D_MODEL), lambda i: (i, 0)),
            pl.BlockSpec((None, 1, 6 * D_MODEL), lambda i: (_mod_row(i, tm, context), 0, 0)),
            pl.BlockSpec((1, D_MODEL), lambda i: (0, 0)),
            _resident((D_MODEL, A_W)),
            pl.BlockSpec((1, A_V), lambda i: (0, 0)),
            pl.BlockSpec((MXU_N, MXU_N), lambda i: (0, 0)),
        ] + extra_in,
        out_specs=tuple(out_specs),
        scratch_shapes=[pltpu.VMEM((tm, D_MODEL), BF16)],
        compiler_params=pltpu.CompilerParams(
            dimension_semantics=("parallel",), vmem_limit_bytes=VMEM_LIMIT),
        name="in_proj_a_ctx" if context else "in_proj_a_lat",
    )(x, mod, g1, w, qkg, bd, *caches)


def _inproj_b_kernel(x_ref, xp_ref, xn_ref, mod_ref, g_ref, w_ref, cw_ref, cb_ref,
                     p_ref, dt_ref, h_ref, *, tm, seq_len):
    i = pl.program_id(0)
    g = g_ref[...]
    shift = mod_ref[:, 0:D_MODEL]
    scale = mod_ref[:, D_MODEL:2 * D_MODEL]
    h_ref[0:tm, :] = _norm_mod(x_ref[...], g, shift, scale)
    h_ref[tm:tm + 2 * SUBLANE, :] = _norm_mod(
        jnp.concatenate([xp_ref[...], xn_ref[...]], axis=0), g, shift, scale)

    ext = tm + 2 * SUBLANE
    start = i * tm
    row = lax.broadcasted_iota(jnp.int32, (tm, 1), 0)
    pos = (start + row) & (seq_len - 1)
    inner_edges = seq_len < tm
    keep_before = (start & (seq_len - 1)) != 0
    keep_after = ((start + tm) & (seq_len - 1)) != 0
    for c0 in range(0, B_GA, IN_TILE_N):
        cols = slice(c0, c0 + IN_TILE_N)
        pre = _dot(h_ref[...], w_ref[:, cols])
        before = jnp.where(keep_before, pre[tm:tm + SUBLANE], 0.0)
        after = jnp.where(keep_after, pre[tm + SUBLANE:ext], 0.0)
        v = jnp.concatenate([before, pre[0:tm], after], axis=0)
        back2 = pltpu.roll(v, 2, 0)[SUBLANE:SUBLANE + tm]
        back1 = pltpu.roll(v, 1, 0)[SUBLANE:SUBLANE + tm]
        ahead = pltpu.roll(v, ext - 1, 0)[SUBLANE:SUBLANE + tm]
        if inner_edges:
            back2 = jnp.where(pos >= 2, back2, 0.0)
            back1 = jnp.where(pos >= 1, back1, 0.0)
            ahead = jnp.where(pos != seq_len - 1, ahead, 0.0)
        cw = cw_ref[:, cols]
        u = cw[0:1] * back2 + cw[1:2] * back1 + cw[2:3] * pre[0:tm] + cw[3:4] * ahead + cb_ref[:, cols]
        p_ref[:, cols] = _silu(u).astype(BF16)
    for c0 in range(B_GA, B_W, IN_TILE_N):
        cols = slice(c0, c0 + IN_TILE_N)
        p_ref[:, cols] = _dot(h_ref[0:tm, :], w_ref[:, cols]).astype(BF16)
    dt_ref[...] = _dot(h_ref[0:tm, :], w_ref[:, B_W:B_W + DT_W])


def _in_projection_b(x, mod, g1, w, conv_w, conv_b, *, context):
    tm = TM_IN
    n_rows = x.shape[0]
    hb = tm // SUBLANE
    last_h = n_rows // SUBLANE - 1
    return pl.pallas_call(
        functools.partial(_inproj_b_kernel, tm=tm, seq_len=SEQ if context else DEC_SEQ),
        out_shape=(jax.ShapeDtypeStruct((n_rows, B_W), BF16), jax.ShapeDtypeStruct((n_rows, DT_W), F32)),
        grid=(n_rows // tm,),
        in_specs=[
            pl.BlockSpec((tm, D_MODEL), lambda i: (i, 0)),
            pl.BlockSpec((SUBLANE, D_MODEL), lambda i: (jnp.maximum(i * hb - 1, 0), 0)),
            pl.BlockSpec((SUBLANE, D_MODEL), lambda i: (jnp.minimum((i + 1) * hb, last_h), 0)),
            pl.BlockSpec((None, 1, 6 * D_MODEL), lambda i: (_mod_row(i, tm, context), 0, 0)),
            pl.BlockSpec((1, D_MODEL), lambda i: (0, 0)),
            _resident((D_MODEL, B_W + DT_W)),
            pl.BlockSpec((SSD_CONV, CONV_DIM), lambda i: (0, 0)),
            pl.BlockSpec((1, CONV_DIM), lambda i: (0, 0)),
        ],
        out_specs=(pl.BlockSpec((tm, B_W), lambda i: (i, 0)), pl.BlockSpec((tm, DT_W), lambda i: (i, 0))),
        scratch_shapes=[pltpu.VMEM((tm + 2 * SUBLANE, D_MODEL), BF16)],
        compiler_params=pltpu.CompilerParams(
            dimension_semantics=("parallel",), vmem_limit_bytes=VMEM_LIMIT),
        name="in_proj_b_ctx" if context else "in_proj_b_lat",
    )(x, x, x, mod, g1, w, conv_w, conv_b)


def _head_masks():
    lane = lax.broadcasted_iota(jnp.int32, (1, LANE), 1)
    return lane < HEAD_DIM


def _ctx_attn_kernel(q_ref, k_ref, v_ref, o_ref):
    lo = _head_masks()
    for c0 in range(0, NA_WIDTH, LANE):
        cols = slice(c0, c0 + LANE)
        q = q_ref[:, cols]
        k = k_ref[:, cols]
        v = v_ref[:, cols]
        outs = []
        for hh in range(2):
            m = lo if hh == 0 else jnp.logical_not(lo)
            qm = jnp.where(m, q, jnp.zeros_like(q))
            s = _dot_nt(qm, k)
            mx = jnp.max(s, axis=-1, keepdims=True)
            p = jnp.exp2(s - mx)
            l = jnp.sum(p, axis=-1, keepdims=True)
            outs.append(_dot(p.astype(BF16), v) / l)
        o_ref[:, cols] = jnp.where(lo, outs[0], outs[1]).astype(BF16)


def _context_attention(pa):
    blk = (SEQ, NA_WIDTH)
    return pl.pallas_call(
        _ctx_attn_kernel,
        out_shape=jax.ShapeDtypeStruct((N_CTX, NA_WIDTH), BF16),
        grid=(BATCH,),
        in_specs=[
            pl.BlockSpec(blk, lambda b: (b, A_Q // NA_WIDTH)),
            pl.BlockSpec(blk, lambda b: (b, A_K // NA_WIDTH)),
            pl.BlockSpec(blk, lambda b: (b, A_V // NA_WIDTH)),
        ],
        out_specs=pl.BlockSpec(blk, lambda b: (b, 0)),
        compiler_params=pltpu.CompilerParams(
            dimension_semantics=("parallel",), vmem_limit_bytes=VMEM_LIMIT),
        name="ctx_attn",
    )(pa, pa, pa)


def _na_bias_kernel(r_ref, o_ref):
    ck = lax.broadcasted_iota(jnp.int32, (GRID_W, LANE), 0)
    ln = lax.broadcasted_iota(jnp.int32, (GRID_W, LANE), 1)
    cq = ln & (GRID_W - 1)
    cs = jnp.clip(cq - WIN_C // 2, 0, GRID_W - WIN_C)
    col_ok = jnp.logical_and(ck >= cs, ck < cs + WIN_C)
    left = ln < GRID_W
    n_dr = 2 * WIN_R - 1
    toep = []
    for dr in range(n_dr):
        base = jnp.broadcast_to(r_ref[dr:dr + 1, :] * LOG2E, (GRID_W, LANE))
        toep.append([pltpu.roll(base, s * GRID_W, 1, stride=1, stride_axis=0) for s in range(2)])
    neg = jnp.full((GRID_W, LANE), MASK_NEG, F32)

    def rel(v, j, q4):
        if v == 0:
            return j - q4 + WIN_R - 1, j < WIN_R
        if v == 1:
            return j - q4 + WIN_R // 2 - 1, q4 <= j < q4 + WIN_R
        return j - q4 - 1, j >= NA_KROWS - WIN_R

    for v in range(NA_VARIANTS):
        for j in range(NA_KROWS):
            for lb in range(NA_QROWS // 2):
                halves = []
                for s in range(2):
                    dr, ok = rel(v, j, 2 * lb + s)
                    halves.append(jnp.where(col_ok, toep[dr][s], neg) if ok else neg)
                o_ref[v, j * GRID_W:(j + 1) * GRID_W, lb * LANE:(lb + 1) * LANE] = (
                    jnp.where(left, halves[0], halves[1]))


def _na_bias(rrev):
    return pl.pallas_call(
        _na_bias_kernel,
        out_shape=jax.ShapeDtypeStruct((NA_HEADS // 2, NA_VARIANTS, NA_KW, 2 * NA_Q), F32),
        grid=(NA_HEADS,),
        in_specs=[pl.BlockSpec((None, 2 * WIN_R, LANE), lambda h: (h, 0, 0))],
        out_specs=pl.BlockSpec((None, NA_VARIANTS, NA_KW, NA_Q), lambda h: (h // 2, 0, 0, h % 2)),
        compiler_params=pltpu.CompilerParams(
            dimension_semantics=("parallel",), vmem_limit_bytes=VMEM_LIMIT),
        name="na_bias",
    )(rrev)


def _na_kernel(q_ref, k_ref, v_ref, kc_ref, vc_ref, bias_ref, o_ref, vt_ref):
    lo = _head_masks()
    kc = kc_ref[...].astype(BF16)
    vct = vc_ref[...].T.astype(BF16)
    for t in range(DEC_SEQ // NA_KBLK):
        vt_ref[t] = v_ref[t * NA_KBLK:(t + 1) * NA_KBLK, :].astype(F32).T.astype(BF16)
    n_wblk = NA_KW // NA_KBLK

    def steps(ms):
        n = range(len(ms))
        blk0 = [min(max(m - 1, 0), DEC_SEQ // NA_KBLK - n_wblk) for m in ms]
        var = [0 if m == 0 else (2 if m == NA_STEPS - 1 else 1) for m in ms]
        qg = [q_ref[m * NA_Q:(m + 1) * NA_Q, :] for m in ms]
        kw = [k_ref[b * NA_KBLK:b * NA_KBLK + NA_KW, :] for b in blk0]
        qcat = [jnp.concatenate([jnp.where(lo, q, jnp.zeros_like(q)), jnp.where(lo, jnp.zeros_like(q), q)], axis=0)
                for q in qg]
        sw = [_dot_nt(kw[i], qcat[i]) + bias_ref[var[i]] for i in n]
        sc = [_dot_nt(kc, qcat[i]) for i in n]
        mx = [jnp.maximum(jnp.max(sw[i], axis=0, keepdims=True), jnp.max(sc[i], axis=0, keepdims=True)) for i in n]
        pw = [jnp.exp2(sw[i] - mx[i]) for i in n]
        pc = [jnp.exp2(sc[i] - mx[i]) for i in n]
        l = [jnp.sum(pw[i], axis=0, keepdims=True) + jnp.sum(pc[i], axis=0, keepdims=True) for i in n]
        pwb = [p.astype(BF16) for p in pw]
        pcb = [p.astype(BF16) for p in pc]
        halves = []
        for hh in range(2):
            rows = slice(hh * HEAD_DIM, (hh + 1) * HEAD_DIM)
            qcols = slice(hh * NA_Q, (hh + 1) * NA_Q)
            o = [_dot(vct[rows, :], p[:, qcols]) for p in pcb]
            for t in range(n_wblk):
                o = [o[i] + _dot(vt_ref[blk0[i] + t, rows, :], pwb[i][t * NA_KBLK:(t + 1) * NA_KBLK, qcols])
                     for i in n]
            halves.append([o[i] / l[i][:, qcols] for i in n])
        for i in n:
            both = jnp.concatenate([halves[0][i], halves[1][i]], axis=0)
            o_ref[ms[i] * NA_Q:(ms[i] + 1) * NA_Q, :] = both.T.astype(BF16)

    for m in range(0, NA_STEPS, NA_LOCKSTEP):
        steps(list(range(m, m + NA_LOCKSTEP)))


def _neighbourhood_attention(pa, cache_k, cache_v, bias, layer):
    hp = NA_HEADS // 2
    blk = (DEC_SEQ, LANE)
    cblk = (None, None, PAST_LEN, LANE)
    return pl.pallas_call(
        _na_kernel,
        out_shape=jax.ShapeDtypeStruct((N_LAT, NA_WIDTH), BF16),
        grid=(hp, DEC_BATCH),
        in_specs=[
            pl.BlockSpec(blk, lambda h, b: (b, A_Q // LANE + h)),
            pl.BlockSpec(blk, lambda h, b: (b, A_K // LANE + h)),
            pl.BlockSpec(blk, lambda h, b: (b, A_V // LANE + h)),
            pl.BlockSpec(cblk, lambda h, b: (b, layer, 0, h)),
            pl.BlockSpec(cblk, lambda h, b: (b, layer, 0, h)),
            pl.BlockSpec((None, NA_VARIANTS, NA_KW, 2 * NA_Q), lambda h, b: (h, 0, 0, 0)),
        ],
        out_specs=pl.BlockSpec(blk, lambda h, b: (b, h)),
        scratch_shapes=[pltpu.VMEM((DEC_SEQ // NA_KBLK, LANE, NA_KBLK), BF16)],
        compiler_params=pltpu.CompilerParams(
            dimension_semantics=("parallel", "parallel"), vmem_limit_bytes=VMEM_LIMIT),
        name="na_attn",
    )(pa, pa, pa, cache_k, cache_v, bias)


def _rpb_reversed(rpb_l):
    n = WIN_C - 1
    pos = rpb_l[:, :, n::-1]
    neg = rpb_l[:, :, :n:-1]
    z = jnp.zeros(rpb_l.shape[:2] + (LANE - 2 * n - 1,), F32)
    r = jnp.concatenate([pos.astype(F32), z, neg.astype(F32)], axis=-1)
    return jnp.pad(r, ((0, 0), (0, 1), (0, 0)))


def _ssd_kernel(*refs, seq_len, has_h0, emit_state):
    it = iter(refs)
    x_ref, b_ref, c_ref, z_ref, dt_ref, dtb_ref, alog_ref, dsk_ref = (next(it) for _ in range(8))
    if has_h0:
        h0f_ref, h0b_ref = next(it), next(it)
    if emit_state:
        next(it), next(it)
    y_ref = next(it)
    if emit_state:
        hf_ref, hb_ref = next(it), next(it)
    bt_s, cum_s, dtv_s, y_s, st_s = (next(it) for _ in range(5))

    n_chunks = seq_len // CHUNK
    n_dir = 2 * HEADS_PER_GROUP

    lane = lax.broadcasted_iota(jnp.int32, (1, LANE), 1)
    ii = lax.broadcasted_iota(jnp.int32, (CHUNK, CHUNK), 0)
    jj = lax.broadcasted_iota(jnp.int32, (CHUNK, CHUNK), 1)
    lower = jj < ii
    diag = jj == ii
    tril = jnp.where(jj <= ii, 1.0, 0.0).astype(BF16)
    triu = jnp.where(jj >= ii, 1.0, 0.0).astype(BF16)
    er = lax.broadcasted_iota(jnp.int32, (LANE, GROUP_W), 0)
    ec = lax.broadcasted_iota(jnp.int32, (LANE, GROUP_W), 1) // SSD_HEADDIM
    exp_f = jnp.where(er == ec, 1.0, 0.0).astype(BF16)
    exp_b = jnp.where(er == ec + HEADS_PER_GROUP, 1.0, 0.0).astype(BF16)
    lo64 = lane < SSD_HEADDIM
    is_fwd = lane < HEADS_PER_GROUP
    a_neg = -jnp.exp(alog_ref[...])
    dtb = dtb_ref[...]
    dsk = dsk_ref[...]
    row_fwd = lax.broadcasted_iota(jnp.int32, (n_dir, 1), 0) < HEADS_PER_GROUP
    pad_rows = jnp.zeros((LANE - n_dir, CHUNK), F32)

    if has_h0:
        st_s[...] = h0f_ref[...].T
    else:
        st_s[...] = jnp.zeros((D_STATE, GROUP_W), F32)

    def local(cs):
        rows = [slice(c * CHUNK, (c + 1) * CHUNK) for c in cs]
        xcb = [x_ref[r, :] for r in rows]
        bcb = [b_ref[r, :] for r in rows]
        ccb = [c_ref[r, :] for r in rows]
        btb = [b.astype(F32).T.astype(BF16) for b in bcb]
        for r, b in zip(rows, btb):
            bt_s[r, :] = b
        raw = [dt_ref[r, :].T[0:n_dir, :] + dtb for r in rows]
        e = [jnp.exp(-jnp.abs(x)) for x in raw]
        log1p_e = [jnp.where(v < 1e-4, v * (1.0 - 0.5 * v), jnp.log(1.0 + v)) for v in e]
        dt_t = [jnp.maximum(x, 0.0) + l for x, l in zip(raw, log1p_e)]
        a = [d * a_neg for d in dt_t]
        cum_t = [jnp.where(row_fwd, _split_dot(v, triu, 3), _split_dot(v, tril, 3)) for v in a]
        cum = [jnp.concatenate([v, pad_rows], axis=0).T for v in cum_t]
        dtv = [jnp.concatenate([v, pad_rows], axis=0).T for v in dt_t]
        for k, r in enumerate(rows):
            cum_s[r, :] = cum[k]
            dtv_s[r, :] = dtv[k]
        g = [_dot_nt(cc, bb) for cc, bb in zip(ccb, bcb)]
        src_t = [c_t - jnp.log(d) for c_t, d in zip(cum_t, dt_t)]
        parts = [[] for _ in cs]
        for pp in range(HEADS_PER_GROUP // 2):
            for k in range(len(cs)):
                ws = []
                for hh in (2 * pp, 2 * pp + 1):
                    hb = HEADS_PER_GROUP + hh
                    seg = jnp.where(lower, cum[k][:, hh:hh + 1] - src_t[k][hh:hh + 1, :],
                                    cum[k][:, hb:hb + 1] - src_t[k][hb:hb + 1, :])
                    both = jnp.exp(seg) + jnp.where(diag, dt_t[k][hh:hh + 1, :], 0.0)
                    ws.append((g[k] * both).astype(BF16))
                wp = jnp.concatenate(ws, axis=1)
                xp = xcb[k][:, pp * LANE:(pp + 1) * LANE]
                zero = jnp.zeros_like(xp)
                rhs = jnp.concatenate([jnp.where(lo64, xp, zero), jnp.where(lo64, zero, xp)], axis=0)
                parts[k].append(_dot(wp, rhs))
        y = [jnp.concatenate(p, axis=1) for p in parts]
        ef_x = [_split_dot(jnp.exp(v), exp_f, EXPAND_TERMS) for v in cum]
        to_end = [jnp.where(is_fwd, v[CHUNK - 1:CHUNK, :] - v, 0.0) for v in cum]
        ff_x = [_split_dot(jnp.exp(t) * d, exp_f, EXPAND_TERMS) for t, d in zip(to_end, dtv)]
        return [(rows[k], xcb[k], ccb[k], btb[k], y[k], ef_x[k], ff_x[k]) for k in range(len(cs))]

    def carry_fwd(vals):
        r, xcb, ccb, btb, y, ef_x, ff_x = vals
        xc = xcb.astype(F32)
        st = st_s[...]
        y_s[r, :] = y + _dot(ccb, st.astype(BF16)) * ef_x + dsk * xc
        st_s[...] = st * ef_x[CHUNK - 1:CHUNK, :] + _dot(btb, (xc * ff_x).astype(BF16))

    step = 4 if n_chunks % 4 == 0 else (2 if n_chunks % 2 == 0 else 1)
    for c in range(0, n_chunks, step):
        for vals in local(list(range(c, c + step))):
            carry_fwd(vals)

    if emit_state:
        hf_ref[...] = st_s[...].T
    if has_h0:
        st_s[...] = h0b_ref[...].T
    else:
        st_s[...] = jnp.zeros((D_STATE, GROUP_W), F32)

    def local_bwd(cs):
        rows = [slice(c * CHUNK, (c + 1) * CHUNK) for c in cs]
        cum = [cum_s[r, :] for r in rows]
        dtv = [dtv_s[r, :] for r in rows]
        eb_x = [_split_dot(jnp.exp(v), exp_b, EXPAND_TERMS) for v in cum]
        to_start = [jnp.where(is_fwd, 0.0, v[0:1, :] - v) for v in cum]
        fb_x = [_split_dot(jnp.exp(t) * d, exp_b, EXPAND_TERMS) for t, d in zip(to_start, dtv)]
        xw = [(x_ref[r, :].astype(F32) * f).astype(BF16) for r, f in zip(rows, fb_x)]
        gate = [_silu(z_ref[r, :].astype(F32)) for r in rows]
        return [(rows[k], eb_x[k], xw[k], gate[k]) for k in range(len(cs))]

    def carry_bwd(vals):
        r, eb_x, xw, gate = vals
        st = st_s[...]
        y = y_s[r, :] + _dot(c_ref[r, :], st.astype(BF16)) * eb_x
        st_s[...] = st * eb_x[0:1, :] + _dot(bt_s[r, :], xw)
        y_ref[r, :] = (y * gate).astype(BF16)

    for c in reversed(range(0, n_chunks, step)):
        for vals in reversed(local_bwd(list(range(c, c + step)))):
            carry_bwd(vals)

    if emit_state:
        hb_ref[...] = st_s[...].T


def _ssd_mixer(pa, pb, dt32, dtb_g, alog_g, dsk, *, context, layer=0, h0f=None, h0b=None, states=()):
    seq_len = SEQ if context else DEC_SEQ
    n_seq = BATCH if context else DEC_BATCH
    has_h0 = h0f is not None
    emit_state = context
    xw_, bw_ = GROUP_W, D_STATE

    in_specs = [
        pl.BlockSpec((seq_len, xw_), lambda b, g: (b, B_X // xw_ + g)),
        pl.BlockSpec((seq_len, bw_), lambda b, g: (b, B_B // bw_ + g)),
        pl.BlockSpec((seq_len, bw_), lambda b, g: (b, B_C // bw_ + g)),
        pl.BlockSpec((seq_len, xw_), lambda b, g: (b, A_Z // xw_ + g)),
        pl.BlockSpec((seq_len, LANE), lambda b, g: (b, g)),
        pl.BlockSpec((None, 2 * HEADS_PER_GROUP, 1), lambda b, g: (g, 0, 0)),
        pl.BlockSpec((None, 2 * HEADS_PER_GROUP, 1), lambda b, g: (g, 0, 0)),
        pl.BlockSpec((1, xw_), lambda b, g: (0, g)),
    ]
    args = [pb, pb, pb, pa, dt32, dtb_g, alog_g, dsk]
    if has_h0:
        st_spec = pl.BlockSpec((None, None, xw_, D_STATE), lambda b, g: (b, layer, g, 0))
        in_specs += [st_spec, st_spec]
        args += [h0f, h0b]
    out_shape = [jax.ShapeDtypeStruct((n_seq * seq_len, SSD_INNER), BF16)]
    out_specs = [pl.BlockSpec((seq_len, xw_), lambda b, g: (b, g))]
    aliases = {}
    if emit_state:
        aliases = {len(args): 1, len(args) + 1: 2}
        in_specs += [pl.BlockSpec(memory_space=pl.ANY)] * 2
        args += list(states)
        so = pl.BlockSpec((None, None, xw_, D_STATE), lambda b, g: (b, layer, g, 0))
        out_shape += [jax.ShapeDtypeStruct(st.shape, F32) for st in states]
        out_specs += [so, so]
    scratch = [
        pltpu.VMEM((seq_len, bw_), BF16),
        pltpu.VMEM((seq_len, LANE), F32),
        pltpu.VMEM((seq_len, LANE), F32),
        pltpu.VMEM((seq_len, xw_), F32),
        pltpu.VMEM((D_STATE, xw_), F32),
    ]
    return pl.pallas_call(
        functools.partial(_ssd_kernel, seq_len=seq_len, has_h0=has_h0, emit_state=emit_state),
        out_shape=tuple(out_shape),
        grid=(n_seq, SSD_GROUPS),
        in_specs=in_specs,
        out_specs=tuple(out_specs),
        input_output_aliases=aliases,
        scratch_shapes=scratch,
        compiler_params=pltpu.CompilerParams(
            dimension_semantics=("parallel", "parallel"), vmem_limit_bytes=VMEM_LIMIT),
        name="ssd_ctx" if context else "ssd_lat",
    )(*args)


def _merge_kernel(x_ref, na_ref, s_ref, ga_ref, gb_ref, mod_ref, sg_ref, wna_ref, wssd_ref, wo_ref, o_ref):
    y = s_ref[...].astype(F32)
    ms = jnp.mean(y * y, axis=-1, keepdims=True)
    yn = (y * lax.rsqrt(ms + EPS) * sg_ref[...]).astype(BF16)
    a = _dot(na_ref[...], wna_ref[...])
    s = _dot(yn, wssd_ref[...])
    mix = _sigmoid(ga_ref[...].astype(F32)) * a + _sigmoid(gb_ref[...].astype(F32)) * s
    o = _dot(mix.astype(BF16), wo_ref[...])
    o_ref[...] = x_ref[...] + mod_ref[:, 2 * D_MODEL:3 * D_MODEL] * o


def _merge(x, na, ssd, pb, mod, ssd_g, w_na, w_ssd, w_o, *, context):
    tm = TM_MERGE
    n_rows = x.shape[0]
    return pl.pallas_call(
        _merge_kernel,
        out_shape=jax.ShapeDtypeStruct((n_rows, D_MODEL), F32),
        grid=(n_rows // tm,),
        in_specs=[
            pl.BlockSpec((tm, D_MODEL), lambda i: (i, 0)),
            pl.BlockSpec((tm, NA_WIDTH), lambda i: (i, 0)),
            pl.BlockSpec((tm, SSD_INNER), lambda i: (i, 0)),
            pl.BlockSpec((tm, D_MODEL), lambda i: (i, B_GA // D_MODEL)),
            pl.BlockSpec((tm, D_MODEL), lambda i: (i, B_GB // D_MODEL)),
            pl.BlockSpec((None, 1, 6 * D_MODEL), lambda i: (_mod_row(i, tm, context), 0, 0)),
            pl.BlockSpec((1, SSD_INNER), lambda i: (0, 0)),
            _resident((NA_WIDTH, D_MODEL)),
            _resident((SSD_INNER, D_MODEL)),
            _resident((D_MODEL, D_MODEL)),
        ],
        out_specs=pl.BlockSpec((tm, D_MODEL), lambda i: (i, 0)),
        compiler_params=pltpu.CompilerParams(
            dimension_semantics=("parallel",), vmem_limit_bytes=VMEM_LIMIT),
        name="merge_ctx" if context else "merge_lat",
    )(x, na, ssd, pb, pb, mod, ssd_g, w_na, w_ssd, w_o)


def _ffn_kernel(x_ref, xp_ref, xn_ref, mod_ref, g_ref, wup_ref, cw_ref, cb_ref, wd_ref,
                o_ref, h_ref, act_ref, *, tm, seq_len):
    i = pl.program_id(0)
    g = g_ref[...]
    shift = mod_ref[:, 3 * D_MODEL:4 * D_MODEL]
    scale = mod_ref[:, 4 * D_MODEL:5 * D_MODEL]
    h_ref[0:tm, :] = _norm_mod(x_ref[...], g, shift, scale)
    h_ref[tm:tm + 2 * SUBLANE, :] = _norm_mod(
        jnp.concatenate([xp_ref[...], xn_ref[...]], axis=0), g, shift, scale)

    ext = tm + 2 * SUBLANE
    start = i * tm
    row = lax.broadcasted_iota(jnp.int32, (tm, 1), 0)
    pos = (start + row) & (seq_len - 1)
    inner_edges = seq_len < tm
    keep_before = (start & (seq_len - 1)) != 0
    keep_after = ((start + tm) & (seq_len - 1)) != 0

    def conv(c0):
        cols = slice(c0, c0 + FF_TILE)
        pre = _dot(h_ref[...], wup_ref[:, cols])
        before = jnp.where(keep_before, pre[tm:tm + SUBLANE], 0.0)
        after = jnp.where(keep_after, pre[tm + SUBLANE:ext], 0.0)
        v = jnp.concatenate([before, pre[0:tm], after], axis=0)
        left = pltpu.roll(v, 1, 0)[SUBLANE:SUBLANE + tm]
        right = pltpu.roll(v, ext - 1, 0)[SUBLANE:SUBLANE + tm]
        if inner_edges:
            left = jnp.where(pos != 0, left, 0.0)
            right = jnp.where(pos != seq_len - 1, right, 0.0)
        cw = cw_ref[:, cols]
        return cw[0:1] * left + cw[1:2] * pre[0:tm] + cw[2:3] * right + cb_ref[:, cols]

    for c0 in range(0, D_FF, FF_TILE):
        val = conv(c0)
        gate = conv(D_FF + c0)
        act_ref[:, c0:c0 + FF_TILE] = (_silu(gate) * val).astype(BF16)

    acc = _dot(act_ref[:, 0:FF_TILE], wd_ref[0:FF_TILE, :])
    for c0 in range(FF_TILE, D_FF, FF_TILE):
        acc = acc + _dot(act_ref[:, c0:c0 + FF_TILE], wd_ref[c0:c0 + FF_TILE, :])
    o_ref[...] = x_ref[...] + mod_ref[:, 5 * D_MODEL:6 * D_MODEL] * acc


def _conv_ffn(x, mod, g2, w_up, conv_w, conv_b, w_down, *, context):
    tm = TM_FFN
    n_rows = x.shape[0]
    hb = tm // SUBLANE
    last_h = n_rows // SUBLANE - 1
    return pl.pallas_call(
        functools.partial(_ffn_kernel, tm=tm, seq_len=SEQ if context else DEC_SEQ),
        out_shape=jax.ShapeDtypeStruct((n_rows, D_MODEL), F32),
        grid=(n_rows // tm,),
        in_specs=[
            pl.BlockSpec((tm, D_MODEL), lambda i: (i, 0)),
            pl.BlockSpec((SUBLANE, D_MODEL), lambda i: (jnp.maximum(i * hb - 1, 0), 0)),
            pl.BlockSpec((SUBLANE, D_MODEL), lambda i: (jnp.minimum((i + 1) * hb, last_h), 0)),
            pl.BlockSpec((None, 1, 6 * D_MODEL), lambda i: (_mod_row(i, tm, context), 0, 0)),
            pl.BlockSpec((1, D_MODEL), lambda i: (0, 0)),
            _resident((D_MODEL, 2 * D_FF)),
            _resident((FFN_CONV, 2 * D_FF)),
            _resident((1, 2 * D_FF)),
            _resident((D_FF, D_MODEL)),
        ],
        out_specs=pl.BlockSpec((tm, D_MODEL), lambda i: (i, 0)),
        scratch_shapes=[
            pltpu.VMEM((tm + 2 * SUBLANE, D_MODEL), BF16),
            pltpu.VMEM((tm, D_FF), BF16),
        ],
        compiler_params=pltpu.CompilerParams(
            dimension_semantics=("parallel",), vmem_limit_bytes=VMEM_LIMIT),
        name="conv_ffn_ctx" if context else "conv_ffn_lat",
    )(x, x, x, mod, g2, w_up, conv_w, conv_b, w_down)


def _w_in_group_b(w):
    x0 = A_W
    dt0 = x0 + CONV_DIM
    ga0 = dt0 + 2 * SSD_HEADS
    dt = w[:, dt0:ga0]
    hg = HEADS_PER_GROUP
    blocks = []
    for g in range(SSD_GROUPS):
        blocks += [dt[:, g * hg:(g + 1) * hg], dt[:, SSD_HEADS + g * hg:SSD_HEADS + (g + 1) * hg],
                   jnp.zeros((D_MODEL, LANE - 2 * hg), w.dtype)]
    return jnp.concatenate([w[:, x0:dt0], w[:, ga0:]] + blocks, axis=1).astype(BF16)


def _by_group(v):
    v = v.astype(F32).reshape(2, SSD_GROUPS, HEADS_PER_GROUP)
    return jnp.concatenate([v[0], v[1]], axis=1)[:, :, None]


def kernel(x_prompt, x_sample, c, cache_k, cache_v, state_ssd_fwd, state_ssd_bwd, c_ctx, w_ada, b_ada, norm1_g, w_in, q_norm_g, k_norm_g, rpb, ssd_conv_w, ssd_conv_b, a_log, dt_bias, d_skip, ssd_norm_g, w_na_out, w_ssd_out, w_o, norm2_g, w_up, ffn_conv_w, ffn_conv_b, w_down):
    xs = {True: x_prompt.reshape(N_CTX, D_MODEL), False: x_sample.reshape(N_LAT, D_MODEL)}
    cond = jnp.concatenate([c_ctx[None, :], c, jnp.zeros((16 - 1 - DEC_BATCH, D_MODEL), F32)], axis=0)
    mod = _modulation(cond, w_ada, b_ada.reshape(DEPTH, 1, 6 * D_MODEL))
    mod = mod.reshape(DEPTH, 16, 1, 6 * D_MODEL)

    ck = cache_k.reshape(DEC_BATCH, DEPTH, PAST_LEN, NA_WIDTH)
    cv = cache_v.reshape(DEC_BATCH, DEPTH, PAST_LEN, NA_WIDTH)
    h0f = state_ssd_fwd.reshape(DEC_BATCH, DEPTH, SSD_INNER, D_STATE)
    h0b = state_ssd_bwd.reshape(DEC_BATCH, DEPTH, SSD_INNER, D_STATE)
    bd = (jnp.arange(MXU_N)[:, None] // HEAD_DIM == jnp.arange(MXU_N)[None, :] // HEAD_DIM).astype(BF16)

    caches = (jnp.zeros((BATCH, DEPTH, SEQ, NA_WIDTH), F32),) * 2
    states = (jnp.zeros((BATCH, DEPTH, SSD_INNER, D_STATE), F32),) * 2
    for l in range(DEPTH):
        qkg = jnp.concatenate([jnp.tile(q_norm_g[l] * (HEAD_DIM ** -0.5 * LOG2E), NA_HEADS),
                               jnp.tile(k_norm_g[l], NA_HEADS)])[None, :]
        g1 = norm1_g[l][None, :]
        w_a = w_in[l][:, :A_W].astype(BF16)
        w_b = _w_in_group_b(w_in[l])
        conv_b = ssd_conv_b[l][None, :]
        pa_ctx, *caches = _in_projection_a(xs[True], mod[l], g1, w_a, qkg, bd, context=True, layer=l, caches=caches)
        (pa_lat,) = _in_projection_a(xs[False], mod[l], g1, w_a, qkg, bd, context=False)
        pb_ctx, dt_ctx = _in_projection_b(xs[True], mod[l], g1, w_b, ssd_conv_w[l], conv_b, context=True)
        pb_lat, dt_lat = _in_projection_b(xs[False], mod[l], g1, w_b, ssd_conv_w[l], conv_b, context=False)

        na_ctx = _context_attention(pa_ctx)
        na_lat = _neighbourhood_attention(pa_lat, ck, cv, _na_bias(_rpb_reversed(rpb[l])), l)

        dtb_g = _by_group(dt_bias[l])
        alog_g = _by_group(a_log[l])
        dsk = jnp.repeat(d_skip[l], SSD_HEADDIM)[None, :]
        ssd_ctx, *states = _ssd_mixer(pa_ctx, pb_ctx, dt_ctx, dtb_g, alog_g, dsk, context=True, layer=l,
                                      states=states)
        (ssd_lat,) = _ssd_mixer(pa_lat, pb_lat, dt_lat, dtb_g, alog_g, dsk, context=False,
                                layer=l, h0f=h0f, h0b=h0b)

        branch = {True: (na_ctx, ssd_ctx, pb_ctx), False: (na_lat, ssd_lat, pb_lat)}
        w_merge = (w_na_out[l].astype(BF16), w_ssd_out[l].astype(BF16), w_o[l].astype(BF16))
        w_ffn = (w_up[l].astype(BF16), ffn_conv_w[l], ffn_conv_b[l][None, :], w_down[l].astype(BF16))
        for ctx in (True, False):
            na, ssd, pb = branch[ctx]
            x1 = _merge(xs[ctx], na, ssd, pb, mod[l], ssd_norm_g[l][None, :], *w_merge, context=ctx)
            xs[ctx] = _conv_ffn(x1, mod[l], norm2_g[l][None, :], *w_ffn, context=ctx)

    y_prompt = xs[True].reshape(BATCH, SEQ, D_MODEL)
    y_sample = xs[False].reshape(DEC_BATCH, DEC_SEQ, D_MODEL)
    cache_shape = (BATCH, DEPTH, SEQ, NA_HEADS, HEAD_DIM)
    state_shape = (BATCH, DEPTH, SSD_HEADS, SSD_HEADDIM, D_STATE)
    return (y_prompt, y_sample, caches[0].reshape(cache_shape), caches[1].reshape(cache_shape),
            states[0].reshape(state_shape), states[1].reshape(state_shape))
```
